```python
import jax
import jax.numpy as jnp
from jax import lax
import numpy as np

D_MODEL = 1024
BATCH = 8
SEQ = 4096
DEPTH = 1

MIX_WIDTH = D_MODEL
HG_WIDTH = MIX_WIDTH // 2
HG_HEAD_DIM = 128
HG_HEADS = HG_WIDTH // HG_HEAD_DIM
HG_EXPAND = 128
HG_FDIM = HG_HEADS * HG_EXPAND
HG_CHUNK = 64
ATT_WIDTH = MIX_WIDTH - HG_WIDTH
ATT_HEAD_DIM = 64
ATT_HEADS = ATT_WIDTH // ATT_HEAD_DIM
DILATED_PAIRS = ((128, 1), (512, 4), (2048, 16))
ATT_BLOCK = 128
D_FF = ((8 * D_MODEL + 3 * 256 - 1) // (3 * 256)) * 256
RMS_EPS = 1e-6
IN_SIZES = (HG_FDIM, HG_FDIM, HG_WIDTH, HG_WIDTH, ATT_WIDTH, ATT_WIDTH, ATT_WIDTH)
IN_WIDTH = HG_FDIM * 2 + HG_WIDTH * 2 + ATT_WIDTH * 3

kernel_name = 'hybrid_hgrn2_dilated_attn_adaln_block'


def rms_norm(x, g):
    xf = x.astype(jnp.float32)
    y = xf * lax.rsqrt(jnp.mean(xf * xf, axis=-1, keepdims=True) + RMS_EPS)
    return (y * g.astype(jnp.float32)).astype(x.dtype)


def modulate(h, shift, scale):
    return h * (1 + scale[:, None, :]) + shift[:, None, :]


def hgrn2_mixer(q, f_raw, i, g, lb, onorm_g):
    B, S = q.shape[0], q.shape[1]
    nc = S // HG_CHUNK
    lb = lb.reshape(HG_HEADS, HG_EXPAND)
    f = lb + (1.0 - lb) * jax.nn.sigmoid(f_raw.astype(jnp.float32))
    k = 1.0 - f
    log_f = jnp.log(f)
    qf = jax.nn.silu(q.astype(jnp.float32))
    vf = i.astype(jnp.float32)

    def chunks(t):
        return t.reshape(B, nc, HG_CHUNK, HG_HEADS, t.shape[-1]).transpose(1, 0, 3, 2, 4)

    qc, kc, vc = chunks(qf), chunks(k), chunks(vf)
    bc = jnp.cumsum(chunks(log_f), axis=3)
    causal = jnp.tril(jnp.ones((HG_CHUNK, HG_CHUNK), dtype=bool))

    def step(state, inp):
        q_c, k_c, v_c, b_c = inp
        o_inter = jnp.einsum('bhtk,bhkv->bhtv', q_c * jnp.exp(b_c), state)
        diff = b_c[:, :, :, None, :] - b_c[:, :, None, :, :]
        decay = jnp.where(causal[:, :, None], jnp.exp(jnp.minimum(diff, 0.0)), 0.0)
        scores = jnp.einsum('bhtk,bhsk,bhtsk->bhts', q_c, k_c, decay)
        o_intra = jnp.einsum('bhts,bhsv->bhtv', scores, v_c)
        b_last = b_c[:, :, -1, :]
        k_dec = k_c * jnp.exp(b_last[:, :, None, :] - b_c)
        state = jnp.exp(b_last)[..., None] * state + jnp.einsum('bhsk,bhsv->bhkv', k_dec, v_c)
        return state, o_inter + o_intra

    state0 = jnp.zeros((B, HG_HEADS, HG_EXPAND, HG_HEAD_DIM), jnp.float32)
    _, o = lax.scan(step, state0, (qc, kc, vc, bc))
    o = o.transpose(1, 0, 3, 2, 4).reshape(B, S, HG_HEADS, HG_HEAD_DIM)
    o = rms_norm(o, onorm_g) * jax.nn.silu(g.astype(jnp.float32))
    return o.reshape(B, S, HG_WIDTH).astype(q.dtype)


def dilated_branch(q, k, v, window, dil):
    B, H, S, E = q.shape
    span = window // dil
    seg = dil * ATT_BLOCK
    s_pad = -(-S // seg) * seg
    m = s_pad // dil
    nb = m // ATT_BLOCK

    def to_blocks(t):
        t = jnp.pad(t, ((0, 0), (0, 0), (0, s_pad - S), (0, 0)))
        t = t.reshape(B, H, m, dil, E).transpose(0, 1, 3, 2, 4)
        return t.reshape(B, H, dil, nb, ATT_BLOCK, E)

    def with_prev(t):
        prev = jnp.pad(t[:, :, :, :-1], ((0, 0), (0, 0), (0, 0), (1, 0), (0, 0), (0, 0)))
        return jnp.concatenate([prev, t], axis=4)

    qb = to_blocks(q)
    kb = with_prev(to_blocks(k))
    vb = with_prev(to_blocks(v))
    s = jnp.einsum('bhrnqe,bhrnke->bhrnqk', qb, kb).astype(jnp.float32)
    qi = jnp.arange(ATT_BLOCK)[:, None]
    kj = jnp.arange(2 * ATT_BLOCK)[None, :]
    dist = ATT_BLOCK + qi - kj
    band = (dist >= 0) & (dist <= span)
    real = (jnp.arange(nb) > 0)[:, None, None] | (kj >= ATT_BLOCK)[None]
    mask = band[None] & real
    s = jnp.where(mask, s, -jnp.inf)
    s_max = jnp.max(s, axis=-1, keepdims=True)
    p = jnp.exp(s - s_max)
    l = jnp.sum(p, axis=-1, keepdims=True)
    o = jnp.einsum('bhrnqk,bhrnke->bhrnqe', p, vb.astype(jnp.float32)) / l
    lse = (s_max + jnp.log(l))[..., 0]
    o = o.reshape(B, H, dil, m, E).transpose(0, 1, 3, 2, 4).reshape(B, H, s_pad, E)[:, :, :S]
    lse = lse.reshape(B, H, dil, m).transpose(0, 1, 3, 2).reshape(B, H, s_pad)[:, :, :S]
    return o, lse


def dilated_attention(q, k, v):
    B, S = q.shape[0], q.shape[1]
    qh = q.transpose(0, 2, 1, 3) * (ATT_HEAD_DIM ** -0.5)
    kh = k.transpose(0, 2, 1, 3)
    vh = v.transpose(0, 2, 1, 3)
    outs, lses = zip(*[dilated_branch(qh, kh, vh, w, d) for (w, d) in DILATED_PAIRS])
    weights = jax.nn.softmax(jnp.stack(lses), axis=0)
    o = jnp.sum(weights[..., None] * jnp.stack(outs), axis=0)
    return o.transpose(0, 2, 1, 3).reshape(B, S, ATT_WIDTH).astype(q.dtype)


def setup_inputs(seed: int = 0) -> dict:
    key = jax.random.key(seed)
    ks = jax.random.split(key, 14)

    def nrm(k, shape, scale):
        return jax.random.normal(k, shape, jnp.float32) * scale

    return {
        'x': nrm(ks[0], (BATCH, SEQ, D_MODEL), 1.0),
        'c': nrm(ks[1], (BATCH, D_MODEL), 1.0),
        'w_ada': nrm(ks[2], (DEPTH, D_MODEL, 6 * D_MODEL), D_MODEL ** -0.5),
        'b_ada': nrm(ks[3], (DEPTH, 6 * D_MODEL), 0.01),
        'norm1_g': 1.0 + nrm(ks[4], (DEPTH, D_MODEL), 0.01),
        'w_in': nrm(ks[5], (DEPTH, D_MODEL, IN_WIDTH), D_MODEL ** -0.5),
        'hg_lb_logits': nrm(ks[6], (DEPTH + 1, HG_FDIM), 0.1),
        'hg_onorm_g': 1.0 + nrm(ks[7], (DEPTH, HG_HEAD_DIM), 0.01),
        'att_onorm_g': 1.0 + nrm(ks[8], (DEPTH, ATT_WIDTH), 0.01),
        'w_out': nrm(ks[9], (DEPTH, MIX_WIDTH, D_MODEL), MIX_WIDTH ** -0.5),
        'norm2_g': 1.0 + nrm(ks[10], (DEPTH, D_MODEL), 0.01),
        'w_gate_up': nrm(ks[11], (DEPTH, D_MODEL, 2 * D_FF), D_MODEL ** -0.5),
        'w_down': nrm(ks[12], (DEPTH, D_FF, D_MODEL), D_FF ** -0.5),
        'final_g': 1.0 + nrm(ks[13], (D_MODEL,), 0.01),
    }


def reference(x, c, w_ada, b_ada, norm1_g, w_in, hg_lb_logits, hg_onorm_g, att_onorm_g,
              w_out, norm2_g, w_gate_up, w_down, final_g):
    B, S = x.shape[0], x.shape[1]
    lower_bounds = jnp.cumsum(jax.nn.softmax(hg_lb_logits.astype(jnp.float32), axis=0), axis=0)
    c_act = jax.nn.silu(c)
    split_at = np.cumsum(IN_SIZES)[:-1].tolist()
    for l in range(DEPTH):
        mod = c_act @ w_ada[l] + b_ada[l]
        shift1, scale1, gate1, shift2, scale2, gate2 = jnp.split(mod, 6, axis=-1)
        h = modulate(rms_norm(x, norm1_g[l]), shift1, scale1)
        hq, hf, hi, hgt, aq, ak, av = jnp.split(h @ w_in[l], split_at, axis=-1)
        hg_out = hgrn2_mixer(
            hq.reshape(B, S, HG_HEADS, HG_EXPAND),
            hf.reshape(B, S, HG_HEADS, HG_EXPAND),
            hi.reshape(B, S, HG_HEADS, HG_HEAD_DIM),
            hgt.reshape(B, S, HG_HEADS, HG_HEAD_DIM),
            lower_bounds[l], hg_onorm_g[l])
        att = dilated_attention(
            aq.reshape(B, S, ATT_HEADS, ATT_HEAD_DIM),
            ak.reshape(B, S, ATT_HEADS, ATT_HEAD_DIM),
            av.reshape(B, S, ATT_HEADS, ATT_HEAD_DIM))
        att_out = rms_norm(att, att_onorm_g[l])
        mix = jnp.concatenate([hg_out, att_out], axis=-1) @ w_out[l]
        x = x + gate1[:, None, :] * mix
        h = modulate(rms_norm(x, norm2_g[l]), shift2, scale2)
        a, u = jnp.split(h @ w_gate_up[l], 2, axis=-1)
        x = x + gate2[:, None, :] * ((jax.nn.silu(a) * u) @ w_down[l])
    return rms_norm(x, final_g)
```

```python
import functools

import numpy as np
import jax
import jax.numpy as jnp
from jax import lax
from jax.experimental import pallas as pl
from jax.experimental.pallas import tpu as pltpu

F32 = jnp.float32
BF16 = jnp.bfloat16

RMS_EPS = 1e-6
HG_HEADS = 4
HG_DIM = 128
HG_WIDTH = HG_HEADS * HG_DIM
ATT_HEAD_DIM = 64
ATT_WIDTH = 512
ATT_BLOCK = 128
DILATIONS = (1, 4, 16)
ATT_SPAN = 128
NEG_BIG = -1e30

HG_CHUNK = 128
HG_LEVELS = 7
ROW_TILE = 512
FF_CHUNK = 256
VMEM_LIMIT_BYTES = 56 * 1024 * 1024


def _silu(x):
    return x * jax.nn.sigmoid(x)


def _const_spec(shape):
    nd = len(shape)
    return pl.BlockSpec(shape, lambda *_: (0,) * nd, pipeline_mode=pl.Buffered(1))


def _mod_kernel(c_ref, w_ref, b_ref, o_ref):
    ca = _silu(c_ref[...])
    o_ref[...] = jnp.dot(ca, w_ref[...], preferred_element_type=F32) + b_ref[...]


def _mod_call(c, w_ada, b_ada):
    bsz, d = c.shape
    n = w_ada.shape[1]
    tn = 1024
    return pl.pallas_call(
        _mod_kernel,
        grid=(n // tn,),
        in_specs=[
            pl.BlockSpec((bsz, d), lambda j: (0, 0)),
            pl.BlockSpec((d, tn), lambda j: (0, j)),
            pl.BlockSpec((1, tn), lambda j: (0, j)),
        ],
        out_specs=pl.BlockSpec((bsz, tn), lambda j: (0, j)),
        out_shape=jax.ShapeDtypeStruct((bsz, n), F32),
        name="adaln_mod",
    )(c, w_ada, b_ada.reshape(1, n))


def _inproj_kernel(x_ref, sh_ref, sc_ref, g_ref, w_ref, lbl_ref,
                   q_o, k_o, lf_o, v_o, g_o, aq_o, ak_o, av_o):
    x = x_ref[...]
    ms = jnp.mean(x * x, axis=-1, keepdims=True)
    y = x * lax.rsqrt(ms + RMS_EPS) * g_ref[...]
    h = y * (1.0 + sc_ref[0]) + sh_ref[0]
    hb = h.astype(BF16)

    def proj(j):
        return jnp.dot(hb, w_ref[:, j * 512:(j + 1) * 512], preferred_element_type=F32)

    lbl = lbl_ref[...]
    e = jnp.exp(lbl - jnp.max(lbl, axis=0, keepdims=True))
    lb = e[0:1] / (e[0:1] + e[1:2])

    q_o[...] = _silu(proj(0)).astype(BF16)
    sg = jax.nn.sigmoid(proj(1))
    f = lb + (1.0 - lb) * sg
    lf_o[...] = jnp.log(f)
    k_o[...] = ((1.0 - lb) * (1.0 - sg)).astype(BF16)
    v_o[...] = proj(2).astype(BF16)
    g_o[...] = _silu(proj(3)).astype(BF16)
    aq_o[...] = proj(4) * (ATT_HEAD_DIM ** -0.5)
    ak_o[...] = proj(5)
    av_o[...] = proj(6)


def _inproj_call(x2, mod3, norm_g, w_in_bf, lb_logits, seq):
    t, d = x2.shape
    tm = ROW_TILE
    steps_per_batch = seq // tm
    row = lambda i: (i, 0)
    out_dtypes = (BF16, BF16, F32, BF16, BF16, F32, F32, F32)
    return pl.pallas_call(
        _inproj_kernel,
        grid=(t // tm,),
        in_specs=[
            pl.BlockSpec((tm, d), row),
            pl.BlockSpec((1, 1, d), lambda i: (i // steps_per_batch, 0, 0)),
            pl.BlockSpec((1, 1, d), lambda i: (i // steps_per_batch, 0, 1)),
            _const_spec((1, d)),
            _const_spec(w_in_bf.shape),
            _const_spec(lb_logits.shape),
        ],
        out_specs=[pl.BlockSpec((tm, 512), row) for _ in out_dtypes],
        out_shape=[jax.ShapeDtypeStruct((t, 512), dt) for dt in out_dtypes],
        compiler_params=pltpu.CompilerParams(
            dimension_semantics=("arbitrary",), vmem_limit_bytes=VMEM_LIMIT_BYTES),
        name="in_proj",
    )(x2, mod3, mod3, norm_g.reshape(1, d), w_in_bf, lb_logits)


def _decay_selectors():
    c = HG_CHUNK
    t = np.arange(c)[:, None]
    u = np.arange(c)[None, :]
    blocks = []
    for l in range(HG_LEVELS):
        lo = (t >> l) << l
        hi = lo + (1 << l) - 1
        second = ((t >> l) & 1) == 1
        blocks.append(np.where(second, (u >= lo) & (u <= t), (u > t) & (u <= hi)))
    blocks.append(u <= t)
    blocks.append(u > t)
    return np.concatenate(blocks, axis=0).astype(np.float32)


def _hgrn_kernel(q_ref, k_ref, lf_ref, v_ref, g_ref, p_ref, on_ref, o_ref, st_ref):
    c = HG_CHUNK

    @pl.when(pl.program_id(1) == 0)
    def _():
        st_ref[...] = jnp.zeros_like(st_ref)

    lf = lf_ref[0]
    lf_hi = lf.astype(BF16)
    lf_lo = (lf - lf_hi.astype(F32)).astype(BF16)
    lf2 = jnp.concatenate([lf_hi, lf_lo], axis=1)

    def exponent(block):
        d = jnp.dot(p_ref[block * c:(block + 1) * c, :], lf2, preferred_element_type=F32)
        return d[:, :HG_WIDTH] + d[:, HG_WIDTH:]

    q = q_ref[0].astype(F32)
    k = k_ref[0].astype(F32)
    v = v_ref[0]

    row = lax.broadcasted_iota(jnp.int32, (c, HG_WIDTH), 0)
    ti = lax.broadcasted_iota(jnp.int32, (c, c), 0)
    si = lax.broadcasted_iota(jnp.int32, (c, c), 1)
    txs = ti ^ si

    def head(a, h):
        return a[:, h * HG_DIM:(h + 1) * HG_DIM]

    def gram(a, b):
        return lax.dot_general(a, b, (((1,), (1,)), ((), ())), preferred_element_type=F32)

    qb = q_ref[0]
    kb = k_ref[0]
    scores = [jnp.where(ti == si, gram(head(qb, h), head(kb, h)), 0.0) for h in range(HG_HEADS)]
    for l in range(HG_LEVELS):
        w = jnp.exp(exponent(l))
        second = ((row >> l) & 1) == 1
        xl = (jnp.where(second, q, k) * w).astype(BF16)
        mask = (ti > si) & ((txs >> l) == 1)
        for h in range(HG_HEADS):
            xh = head(xl, h)
            scores[h] = jnp.where(mask, gram(xh, xh), scores[h])

    b = exponent(HG_LEVELS)
    qe = (q * jnp.exp(b)).astype(BF16)
    kd = (k * jnp.exp(exponent(HG_LEVELS + 1))).astype(BF16)
    chunk_decay = jnp.exp(b[c - 1:c, :])

    g = g_ref[0].astype(F32)
    outs = []
    for h in range(HG_HEADS):
        st = st_ref[h]
        vh = head(v, h)
        o = jnp.dot(scores[h].astype(BF16), vh, preferred_element_type=F32)
        o = o + gram(head(qe, h), st.astype(BF16))
        st_ref[h] = st * head(chunk_decay, h) + lax.dot_general(
            vh, head(kd, h), (((0,), (0,)), ((), ())), preferred_element_type=F32)
        ms = jnp.mean(o * o, axis=-1, keepdims=True)
        outs.append(o * lax.rsqrt(ms + RMS_EPS) * on_ref[...] * head(g, h))
    o_ref[0] = jnp.concatenate(outs, axis=1).astype(o_ref.dtype)


def _hgrn_call(q, k, lf, v, g, onorm_g):
    bsz, seq, w = q.shape
    c = HG_CHUNK
    sel = jnp.asarray(_decay_selectors(), dtype=BF16)
    blk = pl.BlockSpec((1, c, w), lambda b, i: (b, i, 0))
    return pl.pallas_call(
        _hgrn_kernel,
        grid=(bsz, seq // c),
        in_specs=[blk, blk, blk, blk, blk, _const_spec(sel.shape), _const_spec((1, HG_DIM))],
        out_specs=blk,
        out_shape=jax.ShapeDtypeStruct((bsz, seq, w), BF16),
        scratch_shapes=[pltpu.VMEM((HG_HEADS, HG_DIM, HG_DIM), F32)],
        compiler_params=pltpu.CompilerParams(
            dimension_semantics=("arbitrary", "arbitrary"), vmem_limit_bytes=VMEM_LIMIT_BYTES),
        name="hgrn2",
    )(q, k, lf, v, g, sel, onorm_g.reshape(1, HG_DIM))


def _attn_kernel(q_ref, k_ref, v_ref, o_ref, acc_ref, m_ref, l_ref):
    seq = q_ref.shape[1]
    blk = ATT_BLOCK
    lane = lax.broadcasted_iota(jnp.int32, (blk, 128), 1)
    first_head = lane < ATT_HEAD_DIM
    qi = lax.broadcasted_iota(jnp.int32, (blk, 2 * blk), 0)
    kj = lax.broadcasted_iota(jnp.int32, (blk, 2 * blk), 1)
    dist = blk + qi - kj
    band = (dist >= 0) & (dist <= ATT_SPAN)
    cur_keys = kj >= blk

    def rows(ref, start, dil):
        if dil == 1:
            return ref[0, pl.ds(pl.multiple_of(start, blk), blk), :]
        return ref[0, pl.ds(start, blk, stride=dil), :]

    def put(ref, start, dil, val):
        if dil == 1:
            ref[pl.ds(pl.multiple_of(start, blk), blk), :] = val
        else:
            ref[pl.ds(start, blk, stride=dil), :] = val

    def take(ref, start, dil):
        if dil == 1:
            return ref[pl.ds(pl.multiple_of(start, blk), blk), :]
        return ref[pl.ds(start, blk, stride=dil), :]

    for branch, dil in enumerate(DILATIONS):
        seg = dil * blk
        log_dil = dil.bit_length() - 1

        def body(idx, carry, dil=dil, seg=seg, log_dil=log_dil, branch=branch):
            r = idx & (dil - 1)
            n = idx >> log_dil
            start = r + n * seg
            prev = r + jnp.maximum(n - 1, 0) * seg
            qf = rows(q_ref, start, dil)
            kcat = jnp.concatenate([rows(k_ref, prev, dil), rows(k_ref, start, dil)],
                                   axis=0).astype(BF16)
            vcat = jnp.concatenate([rows(v_ref, prev, dil), rows(v_ref, start, dil)],
                                   axis=0).astype(BF16)
            mask = band & (cur_keys | (n > 0))
            ms, ls, os_ = [], [], []
            for hmask in (first_head, ~first_head):
                qh = jnp.where(hmask, qf, 0.0).astype(BF16)
                s = lax.dot_general(qh, kcat, (((1,), (1,)), ((), ())),
                                    preferred_element_type=F32)
                s = jnp.where(mask, s, NEG_BIG)
                mh = jnp.max(s, axis=-1, keepdims=True)
                p = jnp.exp(s - mh)
                ls.append(jnp.sum(p, axis=-1, keepdims=True))
                ms.append(mh)
                os_.append(jnp.dot(p.astype(BF16), vcat, preferred_element_type=F32))
            m_new = jnp.where(first_head, ms[0], ms[1])
            l_new = jnp.where(first_head, ls[0], ls[1])
            o_new = jnp.where(first_head, os_[0], os_[1])
            if branch > 0:
                m_old = take(m_ref, start, dil)
                m_tot = jnp.maximum(m_old, m_new)
                a_old = jnp.exp(m_old - m_tot)
                a_new = jnp.exp(m_new - m_tot)
                o_new = take(acc_ref, start, dil) * a_old + o_new * a_new
                l_new = take(l_ref, start, dil) * a_old + l_new * a_new
                m_new = m_tot
            put(acc_ref, start, dil, o_new)
            put(m_ref, start, dil, m_new)
            put(l_ref, start, dil, l_new)
            return carry

        lax.fori_loop(0, seq // blk, body, 0)

    o_ref[0] = (acc_ref[...] / l_ref[...]).astype(o_ref.dtype)


def _attn_call(aq, ak, av):
    bsz, seq, w = aq.shape
    blk = pl.BlockSpec((1, seq, 128), lambda b, hp: (b, 0, hp))
    return pl.pallas_call(
        _attn_kernel,
        grid=(bsz, w // 128),
        in_specs=[blk, blk, blk],
        out_specs=blk,
        out_shape=jax.ShapeDtypeStruct((bsz, seq, w), BF16),
        scratch_shapes=[pltpu.VMEM((seq, 128), F32) for _ in range(3)],
        compiler_params=pltpu.CompilerParams(
            dimension_semantics=("arbitrary", "arbitrary"), vmem_limit_bytes=VMEM_LIMIT_BYTES),
        name="dilated_attn",
    )(aq, ak, av)


def _tail_kernel(x_ref, hg_ref, att_ref, g1_ref, sh2_ref, sc2_ref, g2_ref,
                 an_ref, n2_ref, fin_ref, wo_ref, wgu_ref, wd_ref, o_ref):
    d_ff = wd_ref.shape[0]
    att = att_ref[...].astype(F32)
    ms = jnp.mean(att * att, axis=-1, keepdims=True)
    att_n = (att * lax.rsqrt(ms + RMS_EPS) * an_ref[...]).astype(BF16)
    mix = jnp.dot(hg_ref[...], wo_ref[:HG_WIDTH, :], preferred_element_type=F32)
    mix = mix + jnp.dot(att_n, wo_ref[HG_WIDTH:, :], preferred_element_type=F32)
    x1 = x_ref[...] + g1_ref[0] * mix

    ms = jnp.mean(x1 * x1, axis=-1, keepdims=True)
    h = x1 * lax.rsqrt(ms + RMS_EPS) * n2_ref[...]
    hb = (h * (1.0 + sc2_ref[0]) + sh2_ref[0]).astype(BF16)
    ffn = jnp.zeros_like(x1)
    for j in range(d_ff // FF_CHUNK):
        lo = j * FF_CHUNK
        a = jnp.dot(hb, wgu_ref[:, lo:lo + FF_CHUNK], preferred_element_type=F32)
        u = jnp.dot(hb, wgu_ref[:, d_ff + lo:d_ff + lo + FF_CHUNK], preferred_element_type=F32)
        act = (_silu(a) * u).astype(BF16)
        ffn = ffn + jnp.dot(act, wd_ref[lo:lo + FF_CHUNK, :], preferred_element_type=F32)
    x2 = x1 + g2_ref[0] * ffn
    ms = jnp.mean(x2 * x2, axis=-1, keepdims=True)
    o_ref[...] = x2 * lax.rsqrt(ms + RMS_EPS) * fin_ref[...]


def _tail_call(x2, hg, att, mod3, att_g, norm2_g, final_g, wo_bf, wgu_bf, wd_bf, seq):
    t, d = x2.shape
    tm = ROW_TILE
    steps_per_batch = seq // tm
    row = lambda i: (i, 0)
    mod_col = lambda col: pl.BlockSpec((1, 1, d), lambda i: (i // steps_per_batch, 0, col))
    return pl.pallas_call(
        _tail_kernel,
        grid=(t // tm,),
        in_specs=[
            pl.BlockSpec((tm, d), row),
            pl.BlockSpec((tm, HG_WIDTH), row),
            pl.BlockSpec((tm, ATT_WIDTH), row),
            mod_col(2), mod_col(3), mod_col(4), mod_col(5),
            _const_spec((1, ATT_WIDTH)), _const_spec((1, d)), _const_spec((1, d)),
            _const_spec(wo_bf.shape), _const_spec(wgu_bf.shape), _const_spec(wd_bf.shape),
        ],
        out_specs=pl.BlockSpec((tm, d), row),
        out_shape=jax.ShapeDtypeStruct((t, d), F32),
        compiler_params=pltpu.CompilerParams(
            dimension_semantics=("arbitrary",), vmem_limit_bytes=VMEM_LIMIT_BYTES),
        name="outproj_ffn",
    )(x2, hg, att, mod3, mod3, mod3, mod3,
      att_g.reshape(1, ATT_WIDTH), norm2_g.reshape(1, d), final_g.reshape(1, d),
      wo_bf, wgu_bf, wd_bf)


def kernel(x, c, w_ada, b_ada, norm1_g, w_in, hg_lb_logits, hg_onorm_g, att_onorm_g,
           w_out, norm2_g, w_gate_up, w_down, final_g):
    bsz, seq, d = x.shape
    assert w_in.shape[0] == 1 and hg_lb_logits.shape[0] == 2, "single-layer block expected"
    assert seq % (DILATIONS[-1] * ATT_BLOCK) == 0 and seq % ROW_TILE == 0
    t = bsz * seq
    x2 = x.reshape(t, d)

    mod = _mod_call(c, w_ada[0], b_ada[0])
    mod3 = mod.reshape(bsz, 1, 6 * d)

    q, k, lf, v, g, aq, ak, av = _inproj_call(
        x2, mod3, norm1_g[0], w_in[0].astype(BF16), hg_lb_logits, seq)

    as_seq = lambda a: a.reshape(bsz, seq, a.shape[-1])
    hg = _hgrn_call(as_seq(q), as_seq(k), as_seq(lf), as_seq(v), as_seq(g), hg_onorm_g[0])
    att = _attn_call(as_seq(aq), as_seq(ak), as_seq(av))

    out = _tail_call(
        x2, hg.reshape(t, HG_WIDTH), att.reshape(t, ATT_WIDTH), mod3,
        att_onorm_g[0], norm2_g[0], final_g,
        w_out[0].astype(BF16), w_gate_up[0].astype(BF16), w_down[0].astype(BF16), seq)
    return out.reshape(bsz, seq, d)
```

```python
import functools

import numpy as np
import jax
import jax.numpy as jnp
from jax import lax
from jax.experimental import pallas as pl
from jax.experimental.pallas import tpu as pltpu

F32 = jnp.float32
BF16 = jnp.bfloat16

RMS_EPS = 1e-6
HG_HEADS = 4
HG_DIM = 128
HG_WIDTH = HG_HEADS * HG_DIM
ATT_HEAD_DIM = 64
ATT_WIDTH = 512
ATT_BLOCK = 128
DILATIONS = (1, 4, 16)
ATT_SPAN = 128
NEG_BIG = -1e30
ATT_UNROLL = 8
ATT_GROUP = 4

HG_CHUNK = 128
HG_LEVELS = 7
ROW_TILE = 512
FF_CHUNK = 256
VMEM_LIMIT_BYTES = 56 * 1024 * 1024


def _silu(x):
    return x * jax.nn.sigmoid(x)


def _const_spec(shape):
    nd = len(shape)
    return pl.BlockSpec(shape, lambda *_: (0,) * nd, pipeline_mode=pl.Buffered(1))


def _mod_kernel(c_ref, w_ref, b_ref, o_ref):
    ca = _silu(c_ref[...])
    o_ref[...] = jnp.dot(ca, w_ref[...], preferred_element_type=F32) + b_ref[...]


def _mod_call(c, w_ada, b_ada):
    bsz, d = c.shape
    n = w_ada.shape[1]
    tn = 1024
    return pl.pallas_call(
        _mod_kernel,
        grid=(n // tn,),
        in_specs=[
            pl.BlockSpec((bsz, d), lambda j: (0, 0)),
            pl.BlockSpec((d, tn), lambda j: (0, j)),
            pl.BlockSpec((1, tn), lambda j: (0, j)),
        ],
        out_specs=pl.BlockSpec((bsz, tn), lambda j: (0, j)),
        out_shape=jax.ShapeDtypeStruct((bsz, n), F32),
        name="adaln_mod",
    )(c, w_ada, b_ada.reshape(1, n))


def _inproj_kernel(x_ref, sh_ref, sc_ref, g_ref, w_ref, lbl_ref,
                   q_o, k_o, lf_o, v_o, g_o, aq_o, ak_o, av_o):
    x = x_ref[...]
    ms = jnp.mean(x * x, axis=-1, keepdims=True)
    y = x * lax.rsqrt(ms + RMS_EPS) * g_ref[...]
    h = y * (1.0 + sc_ref[0]) + sh_ref[0]
    hb = h.astype(BF16)

    def proj(j):
        return jnp.dot(hb, w_ref[:, j * 512:(j + 1) * 512], preferred_element_type=F32)

    lbl = lbl_ref[...]
    e = jnp.exp(lbl - jnp.max(lbl, axis=0, keepdims=True))
    lb = e[0:1] / (e[0:1] + e[1:2])

    q_o[...] = _silu(proj(0)).astype(BF16)
    sg = jax.nn.sigmoid(proj(1))
    f = lb + (1.0 - lb) * sg
    lf_o[...] = jnp.log(f)
    k_o[...] = ((1.0 - lb) * (1.0 - sg)).astype(BF16)
    v_o[...] = proj(2).astype(BF16)
    g_o[...] = _silu(proj(3)).astype(BF16)
    aq_o[...] = proj(4) * (ATT_HEAD_DIM ** -0.5)
    ak_o[...] = proj(5)
    av_o[...] = proj(6)


def _inproj_call(x2, mod3, norm_g, w_in_bf, lb_logits, seq):
    t, d = x2.shape
    tm = ROW_TILE
    steps_per_batch = seq // tm
    row = lambda i: (i, 0)
    out_dtypes = (BF16, BF16, F32, BF16, BF16, F32, F32, F32)
    return pl.pallas_call(
        _inproj_kernel,
        grid=(t // tm,),
        in_specs=[
            pl.BlockSpec((tm, d), row),
            pl.BlockSpec((1, 1, d), lambda i: (i // steps_per_batch, 0, 0)),
            pl.BlockSpec((1, 1, d), lambda i: (i // steps_per_batch, 0, 1)),
            _const_spec((1, d)),
            _const_spec(w_in_bf.shape),
            _const_spec(lb_logits.shape),
        ],
        out_specs=[pl.BlockSpec((tm, 512), row) for _ in out_dtypes],
        out_shape=[jax.ShapeDtypeStruct((t, 512), dt) for dt in out_dtypes],
        compiler_params=pltpu.CompilerParams(
            dimension_semantics=("arbitrary",), vmem_limit_bytes=VMEM_LIMIT_BYTES),
        name="in_proj",
    )(x2, mod3, mod3, norm_g.reshape(1, d), w_in_bf, lb_logits)


def _decay_selectors():
    c = HG_CHUNK
    t = np.arange(c)[:, None]
    u = np.arange(c)[None, :]
    blocks = []
    for l in range(HG_LEVELS):
        lo = (t >> l) << l
        hi = lo + (1 << l) - 1
        second = ((t >> l) & 1) == 1
        blocks.append(np.where(second, (u >= lo) & (u <= t), (u > t) & (u <= hi)))
    blocks.append(u <= t)
    blocks.append(u > t)
    return np.concatenate(blocks, axis=0).astype(np.float32)


def _hgrn_kernel(q_ref, k_ref, lf_ref, v_ref, g_ref, p_ref, on_ref, o_ref, st_ref):
    c = HG_CHUNK

    @pl.when(pl.program_id(1) == 0)
    def _():
        st_ref[...] = jnp.zeros_like(st_ref)

    lf = lf_ref[0]
    lf_hi = lf.astype(BF16)
    lf_lo = (lf - lf_hi.astype(F32)).astype(BF16)
    lf2 = jnp.concatenate([lf_hi, lf_lo], axis=1)

    def exponent(block):
        d = jnp.dot(p_ref[block * c:(block + 1) * c, :], lf2, preferred_element_type=F32)
        return d[:, :HG_WIDTH] + d[:, HG_WIDTH:]

    q = q_ref[0].astype(F32)
    k = k_ref[0].astype(F32)
    v = v_ref[0]

    row = lax.broadcasted_iota(jnp.int32, (c, HG_WIDTH), 0)
    ti = lax.broadcasted_iota(jnp.int32, (c, c), 0)
    si = lax.broadcasted_iota(jnp.int32, (c, c), 1)
    txs = ti ^ si

    def head(a, h):
        return a[:, h * HG_DIM:(h + 1) * HG_DIM]

    def gram(a, b):
        return lax.dot_general(a, b, (((1,), (1,)), ((), ())), preferred_element_type=F32)

    qb = q_ref[0]
    kb = k_ref[0]
    scores = [jnp.where(ti == si, gram(head(qb, h), head(kb, h)), 0.0) for h in range(HG_HEADS)]
    for l in range(HG_LEVELS):
        w = jnp.exp(exponent(l))
        second = ((row >> l) & 1) == 1
        xl = (jnp.where(second, q, k) * w).astype(BF16)
        mask = (ti > si) & ((txs >> l) == 1)
        for h in range(HG_HEADS):
            xh = head(xl, h)
            scores[h] = jnp.where(mask, gram(xh, xh), scores[h])

    b = exponent(HG_LEVELS)
    qe = (q * jnp.exp(b)).astype(BF16)
    kd = (k * jnp.exp(exponent(HG_LEVELS + 1))).astype(BF16)
    chunk_decay = jnp.exp(b[c - 1:c, :])

    g = g_ref[0].astype(F32)
    outs = []
    for h in range(HG_HEADS):
        st = st_ref[h]
        vh = head(v, h)
        o = jnp.dot(scores[h].astype(BF16), vh, preferred_element_type=F32)
        o = o + gram(head(qe, h), st.astype(BF16))
        st_ref[h] = st * head(chunk_decay, h) + lax.dot_general(
            vh, head(kd, h), (((0,), (0,)), ((), ())), preferred_element_type=F32)
        ms = jnp.mean(o * o, axis=-1, keepdims=True)
        outs.append(o * lax.rsqrt(ms + RMS_EPS) * on_ref[...] * head(g, h))
    o_ref[0] = jnp.concatenate(outs, axis=1).astype(o_ref.dtype)


def _hgrn_call(q, k, lf, v, g, onorm_g):
    bsz, seq, w = q.shape
    c = HG_CHUNK
    sel = jnp.asarray(_decay_selectors(), dtype=BF16)
    blk = pl.BlockSpec((1, c, w), lambda b, i: (b, i, 0))
    return pl.pallas_call(
        _hgrn_kernel,
        grid=(bsz, seq // c),
        in_specs=[blk, blk, blk, blk, blk, _const_spec(sel.shape), _const_spec((1, HG_DIM))],
        out_specs=blk,
        out_shape=jax.ShapeDtypeStruct((bsz, seq, w), BF16),
        scratch_shapes=[pltpu.VMEM((HG_HEADS, HG_DIM, HG_DIM), F32)],
        compiler_params=pltpu.CompilerParams(
            dimension_semantics=("arbitrary", "arbitrary"), vmem_limit_bytes=VMEM_LIMIT_BYTES),
        name="hgrn2",
    )(q, k, lf, v, g, sel, onorm_g.reshape(1, HG_DIM))


def _attn_kernel(q_ref, k_ref, v_ref, o_ref, qd_ref, kd_ref, vd_ref, acc_ref, m_ref, l_ref):
    seq = q_ref.shape[1]
    blk = ATT_BLOCK
    grp = ATT_GROUP
    quarter = seq // grp
    slab = blk // grp

    def deinterleave(i, carry):
        base = pl.multiple_of(i * blk, blk)
        for src, dst in ((q_ref, qd_ref), (k_ref, kd_ref), (v_ref, vd_ref)):
            for c in range(grp):
                dst[pl.ds(c * quarter + base, blk), :] = (
                    src[0, pl.ds(c + grp * base, blk, stride=grp), :])
        return carry

    lax.fori_loop(0, quarter // blk, deinterleave, 0)

    lane = lax.broadcasted_iota(jnp.int32, (blk, 128), 1)
    first_head = lane < ATT_HEAD_DIM
    qi = lax.broadcasted_iota(jnp.int32, (blk, 2 * blk), 0)
    kj = lax.broadcasted_iota(jnp.int32, (blk, 2 * blk), 1)

    def band_of(qpos, kpos):
        dist = blk + qpos - kpos
        return (dist >= 0) & (dist <= ATT_SPAN)

    def unslab(a):
        return (a & blk) + grp * (a & (slab - 1)) + ((a & (blk - 1)) // slab)

    band = band_of(qi, kj)
    band_slab = band_of(unslab(qi), unslab(kj))
    cur_keys = kj >= blk
    own_keys = (lax.broadcasted_iota(jnp.int32, (blk, blk), 1)
                <= lax.broadcasted_iota(jnp.int32, (blk, blk), 0))

    def attend(qf, kcat, vcat, mask):
        q2 = jnp.concatenate([jnp.where(first_head, qf, 0.0), jnp.where(first_head, 0.0, qf)],
                             axis=0).astype(BF16)
        s = lax.dot_general(q2, kcat, (((1,), (1,)), ((), ())), preferred_element_type=F32)
        s = jnp.where(jnp.concatenate([mask, mask], axis=0), s, NEG_BIG)
        m2 = jnp.max(s, axis=-1, keepdims=True)
        p = jnp.exp(s - m2)
        l2 = jnp.sum(p, axis=-1, keepdims=True)
        o2 = jnp.dot(p.astype(BF16), vcat, preferred_element_type=F32)
        return (jnp.where(first_head, m2[:blk], m2[blk:]),
                jnp.where(first_head, l2[:blk], l2[blk:]),
                jnp.where(first_head, o2[:blk], o2[blk:]))

    def merge(old, new):
        m_old, l_old, o_old = old
        m_new, l_new, o_new = new
        m_tot = jnp.maximum(m_old, m_new)
        a_old = jnp.exp(m_old - m_tot)
        a_new = jnp.exp(m_new - m_tot)
        return m_tot, l_old * a_old + l_new * a_new, o_old * a_old + o_new * a_new

    stats = (m_ref, l_ref, acc_ref)

    blocks_per_res = quarter // blk

    def body4(idx, carry):
        c = idx // blocks_per_res
        n = idx % blocks_per_res
        cur = pl.ds(pl.multiple_of(c * quarter + n * blk, blk), blk)
        prev = pl.ds(pl.multiple_of(c * quarter + jnp.maximum(n - 1, 0) * blk, blk), blk)
        kcat = jnp.concatenate([kd_ref[prev, :], kd_ref[cur, :]], axis=0).astype(BF16)
        vcat = jnp.concatenate([vd_ref[prev, :], vd_ref[cur, :]], axis=0).astype(BF16)
        new = attend(qd_ref[cur, :], kcat, vcat, band & (cur_keys | (n > 0)))
        for ref, val in zip(stats, new):
            ref[cur, :] = val
        return carry

    lax.fori_loop(0, seq // blk, body4, 0, unroll=ATT_UNROLL)

    def slabs(ref, n):
        return jnp.concatenate(
            [ref[pl.ds(pl.multiple_of(c * quarter + n * slab, slab), slab), :] for c in range(grp)],
            axis=0)

    def body1(n, carry):
        p = jnp.maximum(n - 1, 0)
        kcat = jnp.concatenate([slabs(kd_ref, p), slabs(kd_ref, n)], axis=0).astype(BF16)
        vcat = jnp.concatenate([slabs(vd_ref, p), slabs(vd_ref, n)], axis=0).astype(BF16)
        new = attend(slabs(qd_ref, n), kcat, vcat, band_slab & (cur_keys | (n > 0)))
        merged = merge(tuple(slabs(ref, n) for ref in stats), new)
        for ref, val in zip(stats, merged):
            for c in range(grp):
                ref[pl.ds(pl.multiple_of(c * quarter + n * slab, slab), slab), :] = (
                    val[c * slab:(c + 1) * slab])
        return carry

    lax.fori_loop(0, seq // blk, body1, 0, unroll=ATT_UNROLL)

    sub_blocks = seq // (DILATIONS[-1] * blk)

    def body16(idx, carry):
        c = idx // grp
        e = idx % grp
        k_prev = v_prev = None
        for n in range(sub_blocks):
            rows = pl.ds(c * quarter + e + n * grp * blk, blk, stride=grp)
            k_cur = kd_ref[rows, :].astype(BF16)
            v_cur = vd_ref[rows, :].astype(BF16)
            if n == 0:
                new = attend(qd_ref[rows, :], k_cur, v_cur, own_keys)
            else:
                new = attend(qd_ref[rows, :], jnp.concatenate([k_prev, k_cur], axis=0),
                             jnp.concatenate([v_prev, v_cur], axis=0), band)
            merged = merge(tuple(ref[rows, :] for ref in stats), new)
            for ref, val in zip(stats, merged):
                ref[rows, :] = val
            k_prev, v_prev = k_cur, v_cur
        return carry

    lax.fori_loop(0, DILATIONS[-1], body16, 0, unroll=max(1, ATT_UNROLL // sub_blocks))

    def finish(i, carry):
        base = pl.multiple_of(i * blk, blk)
        for c in range(grp):
            rows = pl.ds(c * quarter + base, blk)
            m_ref[pl.ds(c + grp * base, blk, stride=grp), :] = acc_ref[rows, :] / l_ref[rows, :]
        return carry

    lax.fori_loop(0, quarter // blk, finish, 0)

    def emit(i, carry):
        rows = pl.ds(pl.multiple_of(i * blk, blk), blk)
        o_ref[0, rows, :] = m_ref[rows, :].astype(o_ref.dtype)
        return carry

    lax.fori_loop(0, seq // blk, emit, 0)


def _attn_call(aq, ak, av):
    bsz, seq, w = aq.shape
    blk = pl.BlockSpec((1, seq, 128), lambda b, hp: (b, 0, hp))
    return pl.pallas_call(
        _attn_kernel,
        grid=(bsz, w // 128),
        in_specs=[blk, blk, blk],
        out_specs=blk,
        out_shape=jax.ShapeDtypeStruct((bsz, seq, w), BF16),
        scratch_shapes=[pltpu.VMEM((seq, 128), F32) for _ in range(6)],
        compiler_params=pltpu.CompilerParams(
            dimension_semantics=("arbitrary", "arbitrary"), vmem_limit_bytes=VMEM_LIMIT_BYTES),
        name="dilated_attn",
    )(aq, ak, av)


def _tail_kernel(x_ref, hg_ref, att_ref, g1_ref, sh2_ref, sc2_ref, g2_ref,
                 an_ref, n2_ref, fin_ref, wo_ref, wgu_ref, wd_ref, o_ref):
    d_ff = wd_ref.shape[0]
    att = att_ref[...].astype(F32)
    ms = jnp.mean(att * att, axis=-1, keepdims=True)
    att_n = (att * lax.rsqrt(ms + RMS_EPS) * an_ref[...]).astype(BF16)
    mix = jnp.dot(hg_ref[...], wo_ref[:HG_WIDTH, :], preferred_element_type=F32)
    mix = mix + jnp.dot(att_n, wo_ref[HG_WIDTH:, :], preferred_element_type=F32)
    x1 = x_ref[...] + g1_ref[0] * mix

    ms = jnp.mean(x1 * x1, axis=-1, keepdims=True)
    h = x1 * lax.rsqrt(ms + RMS_EPS) * n2_ref[...]
    hb = (h * (1.0 + sc2_ref[0]) + sh2_ref[0]).astype(BF16)
    ffn = jnp.zeros_like(x1)
    for j in range(d_ff // FF_CHUNK):
        lo = j * FF_CHUNK
        a = jnp.dot(hb, wgu_ref[:, lo:lo + FF_CHUNK], preferred_element_type=F32)
        u = jnp.dot(hb, wgu_ref[:, d_ff + lo:d_ff + lo + FF_CHUNK], preferred_element_type=F32)
        act = (_silu(a) * u).astype(BF16)
        ffn = ffn + jnp.dot(act, wd_ref[lo:lo + FF_CHUNK, :], preferred_element_type=F32)
    x2 = x1 + g2_ref[0] * ffn
    ms = jnp.mean(x2 * x2, axis=-1, keepdims=True)
    o_ref[...] = x2 * lax.rsqrt(ms + RMS_EPS) * fin_ref[...]


def _tail_call(x2, hg, att, mod3, att_g, norm2_g, final_g, wo_bf, wgu_bf, wd_bf, seq):
    t, d = x2.shape
    tm = ROW_TILE
    steps_per_batch = seq // tm
    row = lambda i: (i, 0)
    mod_col = lambda col: pl.BlockSpec((1, 1, d), lambda i: (i // steps_per_batch, 0, col))
    return pl.pallas_call(
        _tail_kernel,
        grid=(t // tm,),
        in_specs=[
            pl.BlockSpec((tm, d), row),
            pl.BlockSpec((tm, HG_WIDTH), row),
            pl.BlockSpec((tm, ATT_WIDTH), row),
            mod_col(2), mod_col(3), mod_col(4), mod_col(5),
            _const_spec((1, ATT_WIDTH)), _const_spec((1, d)), _const_spec((1, d)),
            _const_spec(wo_bf.shape), _const_spec(wgu_bf.shape), _const_spec(wd_bf.shape),
        ],
        out_specs=pl.BlockSpec((tm, d), row),
        out_shape=jax.ShapeDtypeStruct((t, d), F32),
        compiler_params=pltpu.CompilerParams(
            dimension_semantics=("arbitrary",), vmem_limit_bytes=VMEM_LIMIT_BYTES),
        name="outproj_ffn",
    )(x2, hg, att, mod3, mod3, mod3, mod3,
      att_g.reshape(1, ATT_WIDTH), norm2_g.reshape(1, d), final_g.reshape(1, d),
      wo_bf, wgu_bf, wd_bf)


def kernel(x, c, w_ada, b_ada, norm1_g, w_in, hg_lb_logits, hg_onorm_g, att_onorm_g,
           w_out, norm2_g, w_gate_up, w_down, final_g):
    bsz, seq, d = x.shape
    assert w_in.shape[0] == 1 and hg_lb_logits.shape[0] == 2, "single-layer block expected"
    assert seq % (DILATIONS[-1] * ATT_BLOCK) == 0 and seq % ROW_TILE == 0
    t = bsz * seq
    x2 = x.reshape(t, d)

    mod = _mod_call(c, w_ada[0], b_ada[0])
    mod3 = mod.reshape(bsz, 1, 6 * d)

    q, k, lf, v, g, aq, ak, av = _inproj_call(
        x2, mod3, norm1_g[0], w_in[0].astype(BF16), hg_lb_logits, seq)

    as_seq = lambda a: a.reshape(bsz, seq, a.shape[-1])
    hg = _hgrn_call(as_seq(q), as_seq(k), as_seq(lf), as_seq(v), as_seq(g), hg_onorm_g[0])
    att = _attn_call(as_seq(aq), as_seq(ak), as_seq(av))

    out = _tail_call(
        x2, hg.reshape(t, HG_WIDTH), att.reshape(t, ATT_WIDTH), mod3,
        att_onorm_g[0], norm2_g[0], final_g,
        w_out[0].astype(BF16), w_gate_up[0].astype(BF16), w_down[0].astype(BF16), seq)
    return out.reshape(bsz, seq, d)
```

```python
import functools

import numpy as np
import jax
import jax.numpy as jnp
from jax import lax
from jax.experimental import pallas as pl
from jax.experimental.pallas import tpu as pltpu

F32 = jnp.float32
BF16 = jnp.bfloat16

RMS_EPS = 1e-6
HG_HEADS = 4
HG_DIM = 128
HG_WIDTH = HG_HEADS * HG_DIM
ATT_HEAD_DIM = 64
ATT_WIDTH = 512
ATT_BLOCK = 128
DILATIONS = (1, 4, 16)
ATT_SPAN = 128
NEG_BIG = -1e30
ATT_UNROLL = 8
ATT_GROUP = 4

HG_CHUNK = 128
HG_LEVELS = 7
HG_HALF = HG_CHUNK // 2
HG_SAFE_LOG2_DECAY = -85.0
HG_GROUP = 4
ROW_TILE = 512
FF_CHUNK = 256
VMEM_LIMIT_BYTES = 56 * 1024 * 1024


def _silu(x):
    return x * jax.nn.sigmoid(x)


def _const_spec(shape):
    nd = len(shape)
    return pl.BlockSpec(shape, lambda *_: (0,) * nd, pipeline_mode=pl.Buffered(1))


def _mod_kernel(c_ref, w_ref, b_ref, o_ref):
    ca = _silu(c_ref[...])
    o_ref[...] = jnp.dot(ca, w_ref[...], preferred_element_type=F32) + b_ref[...]


def _mod_call(c, w_ada, b_ada):
    bsz, d = c.shape
    n = w_ada.shape[1]
    tn = 1024
    return pl.pallas_call(
        _mod_kernel,
        grid=(n // tn,),
        in_specs=[
            pl.BlockSpec((bsz, d), lambda j: (0, 0)),
            pl.BlockSpec((d, tn), lambda j: (0, j)),
            pl.BlockSpec((1, tn), lambda j: (0, j)),
        ],
        out_specs=pl.BlockSpec((bsz, tn), lambda j: (0, j)),
        out_shape=jax.ShapeDtypeStruct((bsz, n), F32),
        name="adaln_mod",
    )(c, w_ada, b_ada.reshape(1, n))


def _inproj_kernel(x_ref, sh_ref, sc_ref, g_ref, w_ref, lbl_ref, on_ref,
                   q_o, k_o, lf_o, v_o, g_o, aq_o, ak_o, av_o, dmin_o):
    x = x_ref[...]
    ms = jnp.mean(x * x, axis=-1, keepdims=True)
    y = x * lax.rsqrt(ms + RMS_EPS) * g_ref[...]
    h = y * (1.0 + sc_ref[0]) + sh_ref[0]
    hb = h.astype(BF16)

    def proj(j):
        return jnp.dot(hb, w_ref[:, j * 512:(j + 1) * 512], preferred_element_type=F32)

    lbl = lbl_ref[...]
    e = jnp.exp(lbl - jnp.max(lbl, axis=0, keepdims=True))
    lb = e[0:1] / (e[0:1] + e[1:2])

    q_o[...] = _silu(proj(0)).astype(BF16)
    sg = jax.nn.sigmoid(proj(1))
    f = lb + (1.0 - lb) * sg
    lf = jnp.log2(f)
    lf_o[...] = lf
    tm = lf.shape[0]
    grp_id = lax.broadcasted_iota(jnp.int32, (tm // HG_HALF, tm), 0)
    row_id = lax.broadcasted_iota(jnp.int32, (tm // HG_HALF, tm), 1)
    member = (row_id // HG_HALF == grp_id).astype(BF16)
    totals = jnp.dot(member, lf.astype(BF16), preferred_element_type=F32)
    dmin_o[0] = jnp.broadcast_to(jnp.min(totals, axis=-1, keepdims=True), dmin_o.shape[1:])
    k_o[...] = ((1.0 - lb) * (1.0 - sg)).astype(BF16)
    v_o[...] = proj(2).astype(BF16)
    g_o[...] = (_silu(proj(3)) * on_ref[...]).astype(BF16)
    aq_o[...] = proj(4) * (ATT_HEAD_DIM ** -0.5)
    ak_o[...] = proj(5)
    av_o[...] = proj(6)


def _inproj_call(x2, mod3, norm_g, w_in_bf, lb_logits, hg_gain, seq):
    t, d = x2.shape
    tm = ROW_TILE
    steps_per_batch = seq // tm
    row = lambda i: (i, 0)
    out_dtypes = (BF16, BF16, F32, BF16, BF16, F32, F32, F32)
    return pl.pallas_call(
        _inproj_kernel,
        grid=(t // tm,),
        in_specs=[
            pl.BlockSpec((tm, d), row),
            pl.BlockSpec((1, 1, d), lambda i: (i // steps_per_batch, 0, 0)),
            pl.BlockSpec((1, 1, d), lambda i: (i // steps_per_batch, 0, 1)),
            _const_spec((1, d)),
            _const_spec(w_in_bf.shape),
            _const_spec(lb_logits.shape),
            _const_spec((1, HG_WIDTH)),
        ],
        out_specs=[pl.BlockSpec((tm, 512), row) for _ in out_dtypes]
        + [pl.BlockSpec((1, tm // HG_HALF, 128), lambda i: (i, 0, 0))],
        out_shape=[jax.ShapeDtypeStruct((t, 512), dt) for dt in out_dtypes]
        + [jax.ShapeDtypeStruct((t // tm, tm // HG_HALF, 128), F32)],
        compiler_params=pltpu.CompilerParams(
            dimension_semantics=("arbitrary",), vmem_limit_bytes=VMEM_LIMIT_BYTES),
        name="in_proj",
    )(x2, mod3, mod3, norm_g.reshape(1, d), w_in_bf, lb_logits,
      jnp.tile(hg_gain.reshape(1, HG_DIM), (1, HG_HEADS)))


def _decay_selectors():
    c = HG_CHUNK
    t = np.arange(c)[:, None]
    u = np.arange(c)[None, :]
    blocks = []
    for l in range(HG_LEVELS):
        lo = (t >> l) << l
        hi = lo + (1 << l) - 1
        second = ((t >> l) & 1) == 1
        blocks.append(np.where(second, (u >= lo) & (u <= t), (u > t) & (u <= hi)))
    blocks.append(u <= t)
    blocks.append(u > t)
    return np.concatenate(blocks, axis=0).astype(np.float32)


def _hgrn_kernel(safe_ref, q_ref, k_ref, lf_ref, v_ref, g_ref, p_ref, o_ref, st_ref):
    c = HG_CHUNK
    half = HG_HALF
    step = pl.program_id(0) * pl.num_programs(1) + pl.program_id(1)

    @pl.when(pl.program_id(1) == 0)
    def _():
        st_ref[...] = jnp.zeros_like(st_ref)

    def head(a, h):
        return a[:, h * HG_DIM:(h + 1) * HG_DIM]

    def gram(a, b):
        return lax.dot_general(a, b, (((1,), (1,)), ((), ())), preferred_element_type=F32)

    def exponents(lf, block):
        lf_hi = lf.astype(BF16)
        lf_lo = (lf - lf_hi.astype(F32)).astype(BF16)
        lf2 = jnp.concatenate([lf_hi, lf_lo], axis=0)
        return jnp.dot(p_ref[block * c:(block + 1) * c, :], lf2, preferred_element_type=F32)

    def scaled(x_bf, log2_scale):
        return x_bf * jnp.exp2(log2_scale).astype(BF16)

    def normed(rows, o, h):
        ms = jnp.mean(o * o, axis=-1, keepdims=True)
        return o * lax.rsqrt(ms + RMS_EPS) * head(g_ref[0, rows, :], h).astype(F32)

    safe = safe_ref[step] > 0

    @pl.when(safe)
    def _():
        top_mask = (lax.broadcasted_iota(jnp.int32, (half, half), 1)
                    <= lax.broadcasted_iota(jnp.int32, (half, half), 0))
        bot_mask = (lax.broadcasted_iota(jnp.int32, (half, c), 1)
                    <= lax.broadcasted_iota(jnp.int32, (half, c), 0) + half)
        chunks = range(HG_GROUP)
        heads = range(HG_HEADS)
        rows = [slice(j * c, (j + 1) * c) for j in chunks]
        b = [exponents(lf_ref[0, rows[j], :], HG_LEVELS) for j in chunks]
        b_mid = [b[j][half - 1:half, :] for j in chunks]
        b_last = [b[j][c - 1:c, :] for j in chunks]
        q = [q_ref[0, rows[j], :] for j in chunks]
        k = [k_ref[0, rows[j], :] for j in chunks]
        qe = [scaled(q[j], b[j]) for j in chunks]
        k_top = [scaled(k[j][:half], -b[j][:half]) for j in chunks]
        k_mid = [scaled(k[j], b_mid[j] - b[j]) for j in chunks]
        q_bot = [scaled(q[j][half:], b[j][half:] - b_mid[j]) for j in chunks]
        s_top = [[jnp.where(top_mask, gram(head(qe[j], h)[:half], head(k_top[j], h)), 0.0)
                  .astype(BF16) for h in heads] for j in chunks]
        s_bot = [[jnp.where(bot_mask, gram(head(q_bot[j], h), head(k_mid[j], h)), 0.0)
                  .astype(BF16) for h in heads] for j in chunks]
        v = [v_ref[0, rows[j], :] for j in chunks]
        intra = [[jnp.concatenate(
            [jnp.dot(s_top[j][h], head(v[j], h)[:half], preferred_element_type=F32),
             jnp.dot(s_bot[j][h], head(v[j], h), preferred_element_type=F32)], axis=0)
            for h in heads] for j in chunks]
        kv = [[lax.dot_general(head(v[j], h), head(k_mid[j], h), (((0,), (0,)), ((), ())),
                               preferred_element_type=F32)
               * head(jnp.exp2(b_last[j] - b_mid[j]), h) for h in heads] for j in chunks]
        states = [st_ref[h] for h in heads]
        for j in chunks:
            chunk_decay = jnp.exp2(b_last[j])
            outs = []
            for h in heads:
                o = intra[j][h] + gram(head(qe[j], h), states[h].astype(BF16))
                states[h] = states[h] * head(chunk_decay, h) + kv[j][h]
                outs.append(normed(rows[j], o, h))
            o_ref[0, rows[j], :] = jnp.concatenate(outs, axis=1).astype(o_ref.dtype)
        for h in heads:
            st_ref[h] = states[h]

    @pl.when(jnp.logical_not(safe))
    def _():
        row = lax.broadcasted_iota(jnp.int32, (c, HG_WIDTH), 0)
        ti = lax.broadcasted_iota(jnp.int32, (c, c), 0)
        si = lax.broadcasted_iota(jnp.int32, (c, c), 1)
        txs = ti ^ si

        def one_chunk(j, carry):
            rows = pl.ds(pl.multiple_of(j * c, c), c)
            q = q_ref[0, rows, :]
            k = k_ref[0, rows, :]
            v = v_ref[0, rows, :]
            lf = lf_ref[0, rows, :]
            scores = [jnp.where(ti == si, gram(head(q, h), head(k, h)), 0.0)
                      for h in range(HG_HEADS)]
            for l in range(HG_LEVELS):
                second = ((row >> l) & 1) == 1
                xl = scaled(jnp.where(second, q, k), exponents(lf, l))
                mask = (ti > si) & ((txs >> l) == 1)
                for h in range(HG_HEADS):
                    xh = head(xl, h)
                    scores[h] = jnp.where(mask, gram(xh, xh), scores[h])
            b = exponents(lf, HG_LEVELS)
            qe = scaled(q, b)
            k_end = scaled(k, exponents(lf, HG_LEVELS + 1))
            chunk_decay = jnp.exp2(b[c - 1:c, :])
            outs = []
            for h in range(HG_HEADS):
                st = st_ref[h]
                o = jnp.dot(scores[h].astype(BF16), head(v, h), preferred_element_type=F32)
                o = o + gram(head(qe, h), st.astype(BF16))
                st_ref[h] = st * head(chunk_decay, h) + lax.dot_general(
                    head(v, h), head(k_end, h), (((0,), (0,)), ((), ())),
                    preferred_element_type=F32)
                outs.append(normed(rows, o, h))
            o_ref[0, rows, :] = jnp.concatenate(outs, axis=1).astype(o_ref.dtype)
            return carry

        lax.fori_loop(0, HG_GROUP, one_chunk, 0)


def _hgrn_call(q, k, lf, v, g, safe):
    bsz, seq, w = q.shape
    rows = HG_GROUP * HG_CHUNK
    sel = _decay_selectors()
    sel = jnp.asarray(np.concatenate([sel, sel], axis=1), dtype=BF16)
    blk = pl.BlockSpec((1, rows, w), lambda b, i, safe_ref: (b, i, 0))
    return pl.pallas_call(
        _hgrn_kernel,
        grid_spec=pltpu.PrefetchScalarGridSpec(
            num_scalar_prefetch=1,
            grid=(bsz, seq // rows),
            in_specs=[blk, blk, blk, blk, blk, _const_spec(sel.shape)],
            out_specs=blk,
            scratch_shapes=[pltpu.VMEM((HG_HEADS, HG_DIM, HG_DIM), F32)],
        ),
        out_shape=jax.ShapeDtypeStruct((bsz, seq, w), BF16),
        compiler_params=pltpu.CompilerParams(
            dimension_semantics=("arbitrary", "arbitrary"), vmem_limit_bytes=VMEM_LIMIT_BYTES),
        name="hgrn2",
    )(safe, q, k, lf, v, g, sel)


def _attn_kernel(q_ref, k_ref, v_ref, o_ref, qd_ref, kd_ref, vd_ref, acc_ref, m_ref, l_ref):
    seq = q_ref.shape[1]
    blk = ATT_BLOCK
    grp = ATT_GROUP
    quarter = seq // grp
    slab = blk // grp

    def deinterleave(i, carry):
        base = pl.multiple_of(i * blk, blk)
        for src, dst in ((q_ref, qd_ref), (k_ref, kd_ref), (v_ref, vd_ref)):
            for c in range(grp):
                dst[pl.ds(c * quarter + base, blk), :] = (
                    src[0, pl.ds(c + grp * base, blk, stride=grp), :])
        return carry

    lax.fori_loop(0, quarter // blk, deinterleave, 0)

    lane = lax.broadcasted_iota(jnp.int32, (blk, 128), 1)
    first_head = lane < ATT_HEAD_DIM
    qi = lax.broadcasted_iota(jnp.int32, (blk, 2 * blk), 0)
    kj = lax.broadcasted_iota(jnp.int32, (blk, 2 * blk), 1)

    def band_of(qpos, kpos):
        dist = blk + qpos - kpos
        return (dist >= 0) & (dist <= ATT_SPAN)

    def unslab(a):
        return (a & blk) + grp * (a & (slab - 1)) + ((a & (blk - 1)) // slab)

    band = band_of(qi, kj)
    band_slab = band_of(unslab(qi), unslab(kj))
    cur_keys = kj >= blk
    own_keys = (lax.broadcasted_iota(jnp.int32, (blk, blk), 1)
                <= lax.broadcasted_iota(jnp.int32, (blk, blk), 0))

    def attend(qf, kcat, vcat, mask):
        q2 = jnp.concatenate([jnp.where(first_head, qf, 0.0), jnp.where(first_head, 0.0, qf)],
                             axis=0).astype(BF16)
        s = lax.dot_general(q2, kcat, (((1,), (1,)), ((), ())), preferred_element_type=F32)
        s = jnp.where(jnp.concatenate([mask, mask], axis=0), s, NEG_BIG)
        m2 = jnp.max(s, axis=-1, keepdims=True)
        p = jnp.exp(s - m2)
        l2 = jnp.sum(p, axis=-1, keepdims=True)
        o2 = jnp.dot(p.astype(BF16), vcat, preferred_element_type=F32)
        return (jnp.where(first_head, m2[:blk], m2[blk:]),
                jnp.where(first_head, l2[:blk], l2[blk:]),
                jnp.where(first_head, o2[:blk], o2[blk:]))

    def merge(old, new):
        m_old, l_old, o_old = old
        m_new, l_new, o_new = new
        m_tot = jnp.maximum(m_old, m_new)
        a_old = jnp.exp(m_old - m_tot)
        a_new = jnp.exp(m_new - m_tot)
        return m_tot, l_old * a_old + l_new * a_new, o_old * a_old + o_new * a_new

    stats = (m_ref, l_ref, acc_ref)

    blocks_per_res = quarter // blk

    def body4(idx, carry):
        c = idx // blocks_per_res
        n = idx % blocks_per_res
        cur = pl.ds(pl.multiple_of(c * quarter + n * blk, blk), blk)
        prev = pl.ds(pl.multiple_of(c * quarter + jnp.maximum(n - 1, 0) * blk, blk), blk)
        kcat = jnp.concatenate([kd_ref[prev, :], kd_ref[cur, :]], axis=0).astype(BF16)
        vcat = jnp.concatenate([vd_ref[prev, :], vd_ref[cur, :]], axis=0).astype(BF16)
        new = attend(qd_ref[cur, :], kcat, vcat, band & (cur_keys | (n > 0)))
        for ref, val in zip(stats, new):
            ref[cur, :] = val
        return carry

    lax.fori_loop(0, seq // blk, body4, 0, unroll=ATT_UNROLL)

    def slabs(ref, n):
        return jnp.concatenate(
            [ref[pl.ds(pl.multiple_of(c * quarter + n * slab, slab), slab), :] for c in range(grp)],
            axis=0)

    def body1(n, carry):
        p = jnp.maximum(n - 1, 0)
        kcat = jnp.concatenate([slabs(kd_ref, p), slabs(kd_ref, n)], axis=0).astype(BF16)
        vcat = jnp.concatenate([slabs(vd_ref, p), slabs(vd_ref, n)], axis=0).astype(BF16)
        new = attend(slabs(qd_ref, n), kcat, vcat, band_slab & (cur_keys | (n > 0)))
        merged = merge(tuple(slabs(ref, n) for ref in stats), new)
        for ref, val in zip(stats, merged):
            for c in range(grp):
                ref[pl.ds(pl.multiple_of(c * quarter + n * slab, slab), slab), :] = (
                    val[c * slab:(c + 1) * slab])
        return carry

    lax.fori_loop(0, seq // blk, body1, 0, unroll=ATT_UNROLL)

    sub_blocks = seq // (DILATIONS[-1] * blk)

    def body16(idx, carry):
        c = idx // grp
        e = idx % grp
        k_prev = v_prev = None
        for n in range(sub_blocks):
            rows = pl.ds(c * quarter + e + n * grp * blk, blk, stride=grp)
            k_cur = kd_ref[rows, :].astype(BF16)
            v_cur = vd_ref[rows, :].astype(BF16)
            if n == 0:
                new = attend(qd_ref[rows, :], k_cur, v_cur, own_keys)
            else:
                new = attend(qd_ref[rows, :], jnp.concatenate([k_prev, k_cur], axis=0),
                             jnp.concatenate([v_prev, v_cur], axis=0), band)
            merged = merge(tuple(ref[rows, :] for ref in stats), new)
            for ref, val in zip(stats, merged):
                ref[rows, :] = val
            k_prev, v_prev = k_cur, v_cur
        return carry

    lax.fori_loop(0, DILATIONS[-1], body16, 0, unroll=max(1, ATT_UNROLL // sub_blocks))

    def finish(i, carry):
        base = pl.multiple_of(i * blk, blk)
        for c in range(grp):
            rows = pl.ds(c * quarter + base, blk)
            m_ref[pl.ds(c + grp * base, blk, stride=grp), :] = acc_ref[rows, :] / l_ref[rows, :]
        return carry

    lax.fori_loop(0, quarter // blk, finish, 0)

    def emit(i, carry):
        rows = pl.ds(pl.multiple_of(i * blk, blk), blk)
        o_ref[0, rows, :] = m_ref[rows, :].astype(o_ref.dtype)
        return carry

    lax.fori_loop(0, seq // blk, emit, 0)


def _attn_call(aq, ak, av):
    bsz, seq, w = aq.shape
    blk = pl.BlockSpec((1, seq, 128), lambda b, hp: (b, 0, hp))
    return pl.pallas_call(
        _attn_kernel,
        grid=(bsz, w // 128),
        in_specs=[blk, blk, blk],
        out_specs=blk,
        out_shape=jax.ShapeDtypeStruct((bsz, seq, w), BF16),
        scratch_shapes=[pltpu.VMEM((seq, 128), F32) for _ in range(6)],
        compiler_params=pltpu.CompilerParams(
            dimension_semantics=("arbitrary", "arbitrary"), vmem_limit_bytes=VMEM_LIMIT_BYTES),
        name="dilated_attn",
    )(aq, ak, av)


def _tail_kernel(x_ref, hg_ref, att_ref, g1_ref, sh2_ref, sc2_ref, g2_ref,
                 an_ref, n2_ref, fin_ref, wo_ref, wgu_ref, wd_ref, o_ref):
    d_ff = wd_ref.shape[0]
    att = att_ref[...].astype(F32)
    ms = jnp.mean(att * att, axis=-1, keepdims=True)
    att_n = (att * lax.rsqrt(ms + RMS_EPS) * an_ref[...]).astype(BF16)
    mix = jnp.dot(hg_ref[...], wo_ref[:HG_WIDTH, :], preferred_element_type=F32)
    mix = mix + jnp.dot(att_n, wo_ref[HG_WIDTH:, :], preferred_element_type=F32)
    x1 = x_ref[...] + g1_ref[0] * mix

    ms = jnp.mean(x1 * x1, axis=-1, keepdims=True)
    h = x1 * lax.rsqrt(ms + RMS_EPS) * n2_ref[...]
    hb = (h * (1.0 + sc2_ref[0]) + sh2_ref[0]).astype(BF16)
    ffn = jnp.zeros_like(x1)
    for j in range(d_ff // FF_CHUNK):
        lo = j * FF_CHUNK
        a = jnp.dot(hb, wgu_ref[:, lo:lo + FF_CHUNK], preferred_element_type=F32)
        u = jnp.dot(hb, wgu_ref[:, d_ff + lo:d_ff + lo + FF_CHUNK], preferred_element_type=F32)
        act = (_silu(a) * u).astype(BF16)
        ffn = ffn + jnp.dot(act, wd_ref[lo:lo + FF_CHUNK, :], preferred_element_type=F32)
    x2 = x1 + g2_ref[0] * ffn
    ms = jnp.mean(x2 * x2, axis=-1, keepdims=True)
    o_ref[...] = x2 * lax.rsqrt(ms + RMS_EPS) * fin_ref[...]


def _tail_call(x2, hg, att, mod3, att_g, norm2_g, final_g, wo_bf, wgu_bf, wd_bf, seq):
    t, d = x2.shape
    tm = ROW_TILE
    steps_per_batch = seq // tm
    row = lambda i: (i, 0)
    mod_col = lambda col: pl.BlockSpec((1, 1, d), lambda i: (i // steps_per_batch, 0, col))
    return pl.pallas_call(
        _tail_kernel,
        grid=(t // tm,),
        in_specs=[
            pl.BlockSpec((tm, d), row),
            pl.BlockSpec((tm, HG_WIDTH), row),
            pl.BlockSpec((tm, ATT_WIDTH), row),
            mod_col(2), mod_col(3), mod_col(4), mod_col(5),
            _const_spec((1, ATT_WIDTH)), _const_spec((1, d)), _const_spec((1, d)),
            _const_spec(wo_bf.shape), _const_spec(wgu_bf.shape), _const_spec(wd_bf.shape),
        ],
        out_specs=pl.BlockSpec((tm, d), row),
        out_shape=jax.ShapeDtypeStruct((t, d), F32),
        compiler_params=pltpu.CompilerParams(
            dimension_semantics=("arbitrary",), vmem_limit_bytes=VMEM_LIMIT_BYTES),
        name="outproj_ffn",
    )(x2, hg, att, mod3, mod3, mod3, mod3,
      att_g.reshape(1, ATT_WIDTH), norm2_g.reshape(1, d), final_g.reshape(1, d),
      wo_bf, wgu_bf, wd_bf)


def kernel(x, c, w_ada, b_ada, norm1_g, w_in, hg_lb_logits, hg_onorm_g, att_onorm_g,
           w_out, norm2_g, w_gate_up, w_down, final_g):
    bsz, seq, d = x.shape
    assert w_in.shape[0] == 1 and hg_lb_logits.shape[0] == 2, "single-layer block expected"
    assert seq % (DILATIONS[-1] * ATT_BLOCK) == 0 and seq % ROW_TILE == 0
    t = bsz * seq
    x2 = x.reshape(t, d)

    mod = _mod_call(c, w_ada[0], b_ada[0])
    mod3 = mod.reshape(bsz, 1, 6 * d)

    q, k, lf, v, g, aq, ak, av, half_decay = _inproj_call(
        x2, mod3, norm1_g[0], w_in[0].astype(BF16), hg_lb_logits, hg_onorm_g[0], seq)
    safe = (jnp.min(half_decay[:, :, 0].reshape(-1, HG_GROUP * HG_CHUNK // HG_HALF), axis=1)
            >= HG_SAFE_LOG2_DECAY).astype(jnp.int32)

    as_seq = lambda a: a.reshape(bsz, seq, a.shape[-1])
    hg = _hgrn_call(as_seq(q), as_seq(k), as_seq(lf), as_seq(v), as_seq(g), safe)
    att = _attn_call(as_seq(aq), as_seq(ak), as_seq(av))

    out = _tail_call(
        x2, hg.reshape(t, HG_WIDTH), att.reshape(t, ATT_WIDTH), mod3,
        att_onorm_g[0], norm2_g[0], final_g,
        w_out[0].astype(BF16), w_gate_up[0].astype(BF16), w_down[0].astype(BF16), seq)
    return out.reshape(bsz, seq, d)
```

```python
import functools

import numpy as np
import jax
import jax.numpy as jnp
from jax import lax
from jax.experimental import pallas as pl
from jax.experimental.pallas import tpu as pltpu

F32 = jnp.float32
BF16 = jnp.bfloat16

RMS_EPS = 1e-6
HG_HEADS = 4
HG_DIM = 128
HG_WIDTH = HG_HEADS * HG_DIM
ATT_HEAD_DIM = 64
ATT_WIDTH = 512
ATT_BLOCK = 128
DILATIONS = (1, 4, 16)
ATT_SPAN = 128
NEG_BIG = -1e30
ATT_ITEMS = 8
LOG2_E = 1.4426950408889634
ATT_GROUP = 4

HG_CHUNK = 128
HG_LEVELS = 7
HG_HALF = HG_CHUNK // 2
HG_SAFE_LOG2_DECAY = -85.0
HG_GROUP = 4
ROW_TILE = 512
FF_CHUNK = 256
VMEM_LIMIT_BYTES = 56 * 1024 * 1024


def _silu(x):
    return x * jax.nn.sigmoid(x)


def _const_spec(shape):
    nd = len(shape)
    return pl.BlockSpec(shape, lambda *_: (0,) * nd, pipeline_mode=pl.Buffered(1))


def _mod_kernel(c_ref, w_ref, b_ref, o_ref):
    ca = _silu(c_ref[...])
    o_ref[...] = jnp.dot(ca, w_ref[...], preferred_element_type=F32) + b_ref[...]


def _mod_call(c, w_ada, b_ada):
    bsz, d = c.shape
    n = w_ada.shape[1]
    tn = 1024
    return pl.pallas_call(
        _mod_kernel,
        grid=(n // tn,),
        in_specs=[
            pl.BlockSpec((bsz, d), lambda j: (0, 0)),
            pl.BlockSpec((d, tn), lambda j: (0, j)),
            pl.BlockSpec((1, tn), lambda j: (0, j)),
        ],
        out_specs=pl.BlockSpec((bsz, tn), lambda j: (0, j)),
        out_shape=jax.ShapeDtypeStruct((bsz, n), F32),
        name="adaln_mod",
    )(c, w_ada, b_ada.reshape(1, n))


def _inproj_kernel(x_ref, sh_ref, sc_ref, g_ref, w_ref, lbl_ref, on_ref,
                   q_o, k_o, lf_o, v_o, g_o, aq_o, ak_o, av_o, dmin_o):
    x = x_ref[...]
    ms = jnp.mean(x * x, axis=-1, keepdims=True)
    y = x * lax.rsqrt(ms + RMS_EPS) * g_ref[...]
    h = y * (1.0 + sc_ref[0]) + sh_ref[0]
    hb = h.astype(BF16)

    def proj(j):
        return jnp.dot(hb, w_ref[:, j * 512:(j + 1) * 512], preferred_element_type=F32)

    lbl = lbl_ref[...]
    e = jnp.exp(lbl - jnp.max(lbl, axis=0, keepdims=True))
    lb = e[0:1] / (e[0:1] + e[1:2])

    q_o[...] = _silu(proj(0)).astype(BF16)
    sg = jax.nn.sigmoid(proj(1))
    f = lb + (1.0 - lb) * sg
    lf = jnp.log2(f)
    lf_o[...] = lf
    tm = lf.shape[0]
    grp_id = lax.broadcasted_iota(jnp.int32, (tm // HG_HALF, tm), 0)
    row_id = lax.broadcasted_iota(jnp.int32, (tm // HG_HALF, tm), 1)
    member = (row_id // HG_HALF == grp_id).astype(BF16)
    totals = jnp.dot(member, lf.astype(BF16), preferred_element_type=F32)
    dmin_o[0] = jnp.broadcast_to(jnp.min(totals, axis=-1, keepdims=True), dmin_o.shape[1:])
    k_o[...] = ((1.0 - lb) * (1.0 - sg)).astype(BF16)
    v_o[...] = proj(2).astype(BF16)
    g_o[...] = (_silu(proj(3)) * on_ref[...]).astype(BF16)
    aq_o[...] = proj(4) * (ATT_HEAD_DIM ** -0.5 * LOG2_E)
    ak_o[...] = proj(5)
    av_o[...] = proj(6)


def _inproj_call(x2, mod3, norm_g, w_in_bf, lb_logits, hg_gain, seq):
    t, d = x2.shape
    tm = ROW_TILE
    steps_per_batch = seq // tm
    row = lambda i: (i, 0)
    out_dtypes = (BF16, BF16, F32, BF16, BF16, F32, F32, F32)
    return pl.pallas_call(
        _inproj_kernel,
        grid=(t // tm,),
        in_specs=[
            pl.BlockSpec((tm, d), row),
            pl.BlockSpec((1, 1, d), lambda i: (i // steps_per_batch, 0, 0)),
            pl.BlockSpec((1, 1, d), lambda i: (i // steps_per_batch, 0, 1)),
            _const_spec((1, d)),
            _const_spec(w_in_bf.shape),
            _const_spec(lb_logits.shape),
            _const_spec((1, HG_WIDTH)),
        ],
        out_specs=[pl.BlockSpec((tm, 512), row) for _ in out_dtypes]
        + [pl.BlockSpec((1, tm // HG_HALF, 128), lambda i: (i, 0, 0))],
        out_shape=[jax.ShapeDtypeStruct((t, 512), dt) for dt in out_dtypes]
        + [jax.ShapeDtypeStruct((t // tm, tm // HG_HALF, 128), F32)],
        compiler_params=pltpu.CompilerParams(
            dimension_semantics=("arbitrary",), vmem_limit_bytes=VMEM_LIMIT_BYTES),
        name="in_proj",
    )(x2, mod3, mod3, norm_g.reshape(1, d), w_in_bf, lb_logits,
      jnp.tile(hg_gain.reshape(1, HG_DIM), (1, HG_HEADS)))


def _decay_selectors():
    c = HG_CHUNK
    t = np.arange(c)[:, None]
    u = np.arange(c)[None, :]
    blocks = []
    for l in range(HG_LEVELS):
        lo = (t >> l) << l
        hi = lo + (1 << l) - 1
        second = ((t >> l) & 1) == 1
        blocks.append(np.where(second, (u >= lo) & (u <= t), (u > t) & (u <= hi)))
    blocks.append(u <= t)
    blocks.append(u > t)
    return np.concatenate(blocks, axis=0).astype(np.float32)


def _hgrn_kernel(safe_ref, q_ref, k_ref, lf_ref, v_ref, g_ref, p_ref, o_ref, st_ref):
    c = HG_CHUNK
    half = HG_HALF
    step = pl.program_id(0) * pl.num_programs(1) + pl.program_id(1)

    @pl.when(pl.program_id(1) == 0)
    def _():
        st_ref[...] = jnp.zeros_like(st_ref)

    def head(a, h):
        return a[:, h * HG_DIM:(h + 1) * HG_DIM]

    def gram(a, b):
        return lax.dot_general(a, b, (((1,), (1,)), ((), ())), preferred_element_type=F32)

    def exponents(lf, block):
        lf_hi = lf.astype(BF16)
        lf_lo = (lf - lf_hi.astype(F32)).astype(BF16)
        lf2 = jnp.concatenate([lf_hi, lf_lo], axis=0)
        return jnp.dot(p_ref[block * c:(block + 1) * c, :], lf2, preferred_element_type=F32)

    def scaled(x_bf, log2_scale):
        return x_bf * jnp.exp2(log2_scale).astype(BF16)

    def normed(rows, o, h):
        ms = jnp.mean(o * o, axis=-1, keepdims=True)
        return o * lax.rsqrt(ms + RMS_EPS) * head(g_ref[0, rows, :], h).astype(F32)

    safe = safe_ref[step] > 0

    @pl.when(safe)
    def _():
        top_mask = (lax.broadcasted_iota(jnp.int32, (half, half), 1)
                    <= lax.broadcasted_iota(jnp.int32, (half, half), 0))
        bot_mask = (lax.broadcasted_iota(jnp.int32, (half, c), 1)
                    <= lax.broadcasted_iota(jnp.int32, (half, c), 0) + half)
        chunks = range(HG_GROUP)
        heads = range(HG_HEADS)
        rows = [slice(j * c, (j + 1) * c) for j in chunks]
        b = [exponents(lf_ref[0, rows[j], :], HG_LEVELS) for j in chunks]
        b_mid = [b[j][half - 1:half, :] for j in chunks]
        b_last = [b[j][c - 1:c, :] for j in chunks]
        q = [q_ref[0, rows[j], :] for j in chunks]
        k = [k_ref[0, rows[j], :] for j in chunks]
        qe = [scaled(q[j], b[j]) for j in chunks]
        k_top = [scaled(k[j][:half], -b[j][:half]) for j in chunks]
        k_mid = [scaled(k[j], b_mid[j] - b[j]) for j in chunks]
        q_bot = [scaled(q[j][half:], b[j][half:] - b_mid[j]) for j in chunks]
        s_top = [[jnp.where(top_mask, gram(head(qe[j], h)[:half], head(k_top[j], h)), 0.0)
                  .astype(BF16) for h in heads] for j in chunks]
        s_bot = [[jnp.where(bot_mask, gram(head(q_bot[j], h), head(k_mid[j], h)), 0.0)
                  .astype(BF16) for h in heads] for j in chunks]
        v = [v_ref[0, rows[j], :] for j in chunks]
        intra = [[jnp.concatenate(
            [jnp.dot(s_top[j][h], head(v[j], h)[:half], preferred_element_type=F32),
             jnp.dot(s_bot[j][h], head(v[j], h), preferred_element_type=F32)], axis=0)
            for h in heads] for j in chunks]
        kv = [[lax.dot_general(head(v[j], h), head(k_mid[j], h), (((0,), (0,)), ((), ())),
                               preferred_element_type=F32)
               * head(jnp.exp2(b_last[j] - b_mid[j]), h) for h in heads] for j in chunks]
        states = [st_ref[h] for h in heads]
        for j in chunks:
            chunk_decay = jnp.exp2(b_last[j])
            outs = []
            for h in heads:
                o = intra[j][h] + gram(head(qe[j], h), states[h].astype(BF16))
                states[h] = states[h] * head(chunk_decay, h) + kv[j][h]
                outs.append(normed(rows[j], o, h))
            o_ref[0, rows[j], :] = jnp.concatenate(outs, axis=1).astype(o_ref.dtype)
        for h in heads:
            st_ref[h] = states[h]

    @pl.when(jnp.logical_not(safe))
    def _():
        row = lax.broadcasted_iota(jnp.int32, (c, HG_WIDTH), 0)
        ti = lax.broadcasted_iota(jnp.int32, (c, c), 0)
        si = lax.broadcasted_iota(jnp.int32, (c, c), 1)
        txs = ti ^ si

        def one_chunk(j, carry):
            rows = pl.ds(pl.multiple_of(j * c, c), c)
            q = q_ref[0, rows, :]
            k = k_ref[0, rows, :]
            v = v_ref[0, rows, :]
            lf = lf_ref[0, rows, :]
            scores = [jnp.where(ti == si, gram(head(q, h), head(k, h)), 0.0)
                      for h in range(HG_HEADS)]
            for l in range(HG_LEVELS):
                second = ((row >> l) & 1) == 1
                xl = scaled(jnp.where(second, q, k), exponents(lf, l))
                mask = (ti > si) & ((txs >> l) == 1)
                for h in range(HG_HEADS):
                    xh = head(xl, h)
                    scores[h] = jnp.where(mask, gram(xh, xh), scores[h])
            b = exponents(lf, HG_LEVELS)
            qe = scaled(q, b)
            k_end = scaled(k, exponents(lf, HG_LEVELS + 1))
            chunk_decay = jnp.exp2(b[c - 1:c, :])
            outs = []
            for h in range(HG_HEADS):
                st = st_ref[h]
                o = jnp.dot(scores[h].astype(BF16), head(v, h), preferred_element_type=F32)
                o = o + gram(head(qe, h), st.astype(BF16))
                st_ref[h] = st * head(chunk_decay, h) + lax.dot_general(
                    head(v, h), head(k_end, h), (((0,), (0,)), ((), ())),
                    preferred_element_type=F32)
                outs.append(normed(rows, o, h))
            o_ref[0, rows, :] = jnp.concatenate(outs, axis=1).astype(o_ref.dtype)
            return carry

        lax.fori_loop(0, HG_GROUP, one_chunk, 0)


def _hgrn_call(q, k, lf, v, g, safe):
    bsz, seq, w = q.shape
    rows = HG_GROUP * HG_CHUNK
    sel = _decay_selectors()
    sel = jnp.asarray(np.concatenate([sel, sel], axis=1), dtype=BF16)
    blk = pl.BlockSpec((1, rows, w), lambda b, i, safe_ref: (b, i, 0))
    return pl.pallas_call(
        _hgrn_kernel,
        grid_spec=pltpu.PrefetchScalarGridSpec(
            num_scalar_prefetch=1,
            grid=(bsz, seq // rows),
            in_specs=[blk, blk, blk, blk, blk, _const_spec(sel.shape)],
            out_specs=blk,
            scratch_shapes=[pltpu.VMEM((HG_HEADS, HG_DIM, HG_DIM), F32)],
        ),
        out_shape=jax.ShapeDtypeStruct((bsz, seq, w), BF16),
        compiler_params=pltpu.CompilerParams(
            dimension_semantics=("arbitrary", "arbitrary"), vmem_limit_bytes=VMEM_LIMIT_BYTES),
        name="hgrn2",
    )(safe, q, k, lf, v, g, sel)


def _attn_biases():
    blk, grp = ATT_BLOCK, ATT_GROUP
    slab = blk // grp
    a = np.arange(blk)
    unslab = grp * (a % slab) + a // slab
    full, own = [], []
    for pos in (a, unslab):
        kpos = np.concatenate([pos, pos + blk])
        dist = blk + pos[:, None] - kpos[None, :]
        keep = (dist >= 0) & (dist <= ATT_SPAN)
        full.append(np.where(np.concatenate([keep, keep], axis=0), 0.0, NEG_BIG))
        keep_own = keep[:, blk:]
        own.append(np.where(np.concatenate([keep_own, keep_own], axis=0), 0.0, NEG_BIG))
    return np.stack(full).astype(np.float32), np.stack(own).astype(np.float32)


def _attn_kernel(q_ref, k_ref, v_ref, full_ref, own_ref, o_ref,
                 qd_ref, kd_ref, vd_ref, acc_ref, m_ref, l_ref):
    seq = q_ref.shape[1]
    blk = ATT_BLOCK
    grp = ATT_GROUP
    quarter = seq // grp
    slab = blk // grp

    def deinterleave(i, carry):
        base = pl.multiple_of(i * blk, blk)
        for src, dst in ((q_ref, qd_ref), (k_ref, kd_ref), (v_ref, vd_ref)):
            for c in range(grp):
                dst[pl.ds(c * quarter + base, blk), :] = (
                    src[0, pl.ds(c + grp * base, blk, stride=grp), :])
        return carry

    lax.fori_loop(0, quarter // blk, deinterleave, 0)

    first_head = lax.broadcasted_iota(jnp.int32, (blk, 128), 1) < ATT_HEAD_DIM
    stats = (m_ref, l_ref, acc_ref)

    def gather(ref, slices):
        parts = [ref[sl, :] for sl in slices]
        return parts[0] if len(parts) == 1 else jnp.concatenate(parts, axis=0)

    def scatter(ref, slices, val):
        n = val.shape[0] // len(slices)
        for i, sl in enumerate(slices):
            ref[sl, :] = val[i * n:(i + 1) * n]

    def process(items, merge):
        def keys(ref, cur, prev):
            own = gather(ref, cur)
            if prev is not None:
                own = jnp.concatenate([gather(ref, prev), own], axis=0)
            return own.astype(BF16)

        def scores(item):
            cur, prev, bias = item
            qf = gather(qd_ref, cur)
            q2 = jnp.concatenate([jnp.where(first_head, qf, 0.0), jnp.where(first_head, 0.0, qf)],
                                 axis=0).astype(BF16)
            return lax.dot_general(q2, keys(kd_ref, cur, prev), (((1,), (1,)), ((), ())),
                                   preferred_element_type=F32) + bias

        def softmax(s):
            m2 = jnp.max(s, axis=-1, keepdims=True)
            p = jnp.exp2(s - m2)
            return m2, jnp.sum(p, axis=-1, keepdims=True), p.astype(BF16)

        def values(item, state):
            cur, prev, _ = item
            m2, l2, p = state
            return m2, l2, jnp.dot(p, keys(vd_ref, cur, prev), preferred_element_type=F32)

        def commit(item, state):
            cur = item[0]
            new = tuple(jnp.where(first_head, x[:blk], x[blk:]) for x in state)
            if merge:
                m_old, l_old, o_old = (gather(ref, cur) for ref in stats)
                m_new, l_new, o_new = new
                m_tot = jnp.maximum(m_old, m_new)
                a_old = jnp.exp2(m_old - m_tot)
                a_new = jnp.exp2(m_new - m_tot)
                new = (m_tot, l_old * a_old + l_new * a_new, o_old * a_old + o_new * a_new)
            for ref, val in zip(stats, new):
                scatter(ref, cur, val)

        stages = (lambda it, st: scores(it), lambda it, st: softmax(st), values, commit)
        state = [None] * len(items)
        for t in range(len(items) + len(stages) - 1):
            for k in reversed(range(len(stages))):
                i = t - k
                if 0 <= i < len(items):
                    state[i] = stages[k](items[i], state[i])

    def rows4(c, n):
        return [pl.ds(pl.multiple_of(c * quarter + n * blk, blk), blk)]

    per_trip = ATT_ITEMS // grp

    def items4(n0, first):
        return [(rows4(c, n0 + j), None, own_ref[0]) if first and j == 0
                else (rows4(c, n0 + j), rows4(c, n0 + j - 1), full_ref[0])
                for j in range(per_trip) for c in range(grp)]

    process(items4(0, True), merge=False)

    def body4(i, carry):
        process(items4(i * per_trip, False), merge=False)
        return carry

    lax.fori_loop(1, quarter // (blk * per_trip), body4, 0)

    def rows1(n):
        return [pl.ds(pl.multiple_of(c * quarter + n * slab, slab), slab) for c in range(grp)]

    process([(rows1(0), None, own_ref[1])]
            + [(rows1(n), rows1(n - 1), full_ref[1]) for n in range(1, ATT_ITEMS)], merge=True)

    def body1(i, carry):
        n0 = i * ATT_ITEMS
        process([(rows1(n0 + j), rows1(n0 + j - 1), full_ref[1]) for j in range(ATT_ITEMS)],
                merge=True)
        return carry

    lax.fori_loop(1, seq // (blk * ATT_ITEMS), body1, 0)

    sub_blocks = seq // (DILATIONS[-1] * blk)

    def rows16(c, e, n):
        return [pl.ds(c * quarter + e + n * grp * blk, blk, stride=grp)]

    def body16(c, carry):
        items = []
        for e in range(grp):
            items.append((rows16(c, e, 0), None, own_ref[0]))
            for n in range(1, sub_blocks):
                items.append((rows16(c, e, n), rows16(c, e, n - 1), full_ref[0]))
        process(items, merge=True)
        return carry

    lax.fori_loop(0, grp, body16, 0)

    def finish(i, carry):
        base = pl.multiple_of(i * blk, blk)
        for c in range(grp):
            rows = pl.ds(c * quarter + base, blk)
            m_ref[pl.ds(c + grp * base, blk, stride=grp), :] = acc_ref[rows, :] / l_ref[rows, :]
        return carry

    lax.fori_loop(0, quarter // blk, finish, 0)

    def emit(i, carry):
        rows = pl.ds(pl.multiple_of(i * blk, blk), blk)
        o_ref[0, rows, :] = m_ref[rows, :].astype(o_ref.dtype)
        return carry

    lax.fori_loop(0, seq // blk, emit, 0)


def _attn_call(aq, ak, av):
    bsz, seq, w = aq.shape
    full, own = (jnp.asarray(x) for x in _attn_biases())
    blk = pl.BlockSpec((1, seq, 128), lambda b, hp: (b, 0, hp))
    return pl.pallas_call(
        _attn_kernel,
        grid=(bsz, w // 128),
        in_specs=[blk, blk, blk, _const_spec(full.shape), _const_spec(own.shape)],
        out_specs=blk,
        out_shape=jax.ShapeDtypeStruct((bsz, seq, w), BF16),
        scratch_shapes=[pltpu.VMEM((seq, 128), F32) for _ in range(6)],
        compiler_params=pltpu.CompilerParams(
            dimension_semantics=("arbitrary", "arbitrary"), vmem_limit_bytes=VMEM_LIMIT_BYTES),
        name="dilated_attn",
    )(aq, ak, av, full, own)


def _tail_kernel(x_ref, hg_ref, att_ref, g1_ref, sh2_ref, sc2_ref, g2_ref,
                 an_ref, n2_ref, fin_ref, wo_ref, wgu_ref, wd_ref, o_ref):
    d_ff = wd_ref.shape[0]
    att = att_ref[...].astype(F32)
    ms = jnp.mean(att * att, axis=-1, keepdims=True)
    att_n = (att * lax.rsqrt(ms + RMS_EPS) * an_ref[...]).astype(BF16)
    mix = jnp.dot(hg_ref[...], wo_ref[:HG_WIDTH, :], preferred_element_type=F32)
    mix = mix + jnp.dot(att_n, wo_ref[HG_WIDTH:, :], preferred_element_type=F32)
    x1 = x_ref[...] + g1_ref[0] * mix

    ms = jnp.mean(x1 * x1, axis=-1, keepdims=True)
    h = x1 * lax.rsqrt(ms + RMS_EPS) * n2_ref[...]
    hb = (h * (1.0 + sc2_ref[0]) + sh2_ref[0]).astype(BF16)
    ffn = jnp.zeros_like(x1)
    for j in range(d_ff // FF_CHUNK):
        lo = j * FF_CHUNK
        a = jnp.dot(hb, wgu_ref[:, lo:lo + FF_CHUNK], preferred_element_type=F32)
        u = jnp.dot(hb, wgu_ref[:, d_ff + lo:d_ff + lo + FF_CHUNK], preferred_element_type=F32)
        act = (_silu(a) * u).astype(BF16)
        ffn = ffn + jnp.dot(act, wd_ref[lo:lo + FF_CHUNK, :], preferred_element_type=F32)
    x2 = x1 + g2_ref[0] * ffn
    ms = jnp.mean(x2 * x2, axis=-1, keepdims=True)
    o_ref[...] = x2 * lax.rsqrt(ms + RMS_EPS) * fin_ref[...]


def _tail_call(x2, hg, att, mod3, att_g, norm2_g, final_g, wo_bf, wgu_bf, wd_bf, seq):
    t, d = x2.shape
    tm = ROW_TILE
    steps_per_batch = seq // tm
    row = lambda i: (i, 0)
    mod_col = lambda col: pl.BlockSpec((1, 1, d), lambda i: (i // steps_per_batch, 0, col))
    return pl.pallas_call(
        _tail_kernel,
        grid=(t // tm,),
        in_specs=[
            pl.BlockSpec((tm, d), row),
            pl.BlockSpec((tm, HG_WIDTH), row),
            pl.BlockSpec((tm, ATT_WIDTH), row),
            mod_col(2), mod_col(3), mod_col(4), mod_col(5),
            _const_spec((1, ATT_WIDTH)), _const_spec((1, d)), _const_spec((1, d)),
            _const_spec(wo_bf.shape), _const_spec(wgu_bf.shape), _const_spec(wd_bf.shape),
        ],
        out_specs=pl.BlockSpec((tm, d), row),
        out_shape=jax.ShapeDtypeStruct((t, d), F32),
        compiler_params=pltpu.CompilerParams(
            dimension_semantics=("arbitrary",), vmem_limit_bytes=VMEM_LIMIT_BYTES),
        name="outproj_ffn",
    )(x2, hg, att, mod3, mod3, mod3, mod3,
      att_g.reshape(1, ATT_WIDTH), norm2_g.reshape(1, d), final_g.reshape(1, d),
      wo_bf, wgu_bf, wd_bf)


def kernel(x, c, w_ada, b_ada, norm1_g, w_in, hg_lb_logits, hg_onorm_g, att_onorm_g,
           w_out, norm2_g, w_gate_up, w_down, final_g):
    bsz, seq, d = x.shape
    assert w_in.shape[0] == 1 and hg_lb_logits.shape[0] == 2, "single-layer block expected"
    assert seq % (DILATIONS[-1] * ATT_BLOCK) == 0 and seq % ROW_TILE == 0
    t = bsz * seq
    x2 = x.reshape(t, d)

    mod = _mod_call(c, w_ada[0], b_ada[0])
    mod3 = mod.reshape(bsz, 1, 6 * d)

    q, k, lf, v, g, aq, ak, av, half_decay = _inproj_call(
        x2, mod3, norm1_g[0], w_in[0].astype(BF16), hg_lb_logits, hg_onorm_g[0], seq)
    safe = (jnp.min(half_decay[:, :, 0].reshape(-1, HG_GROUP * HG_CHUNK // HG_HALF), axis=1)
            >= HG_SAFE_LOG2_DECAY).astype(jnp.int32)

    as_seq = lambda a: a.reshape(bsz, seq, a.shape[-1])
    hg = _hgrn_call(as_seq(q), as_seq(k), as_seq(lf), as_seq(v), as_seq(g), safe)
    att = _attn_call(as_seq(aq), as_seq(ak), as_seq(av))

    out = _tail_call(
        x2, hg.reshape(t, HG_WIDTH), att.reshape(t, ATT_WIDTH), mod3,
        att_onorm_g[0], norm2_g[0], final_g,
        w_out[0].astype(BF16), w_gate_up[0].astype(BF16), w_down[0].astype(BF16), seq)
    return out.reshape(bsz, seq, d)
```

```python
import functools

import numpy as np
import jax
import jax.numpy as jnp
from jax import lax
from jax.experimental import pallas as pl
from jax.experimental.pallas import tpu as pltpu

F32 = jnp.float32
BF16 = jnp.bfloat16

RMS_EPS = 1e-6
HG_HEADS = 4
HG_DIM = 128
HG_WIDTH = HG_HEADS * HG_DIM
ATT_HEAD_DIM = 64
ATT_WIDTH = 512
ATT_BLOCK = 128
DILATIONS = (1, 4, 16)
ATT_SPAN = 128
NEG_BIG = -1e30
ATT_ITEMS = 32
LOG2_E = 1.4426950408889634
ATT_GROUP = 4

HG_CHUNK = 128
HG_LEVELS = 7
HG_HALF = HG_CHUNK // 2
HG_SAFE_LOG2_DECAY = -85.0
HG_GROUP = 4
ROW_TILE = 512
FF_CHUNK = 256
PROJ_COLS = 512
VMEM_LIMIT_BYTES = 56 * 1024 * 1024


def _silu(x):
    return x * jax.nn.sigmoid(x)


def _const_spec(shape):
    nd = len(shape)
    return pl.BlockSpec(shape, lambda *_: (0,) * nd, pipeline_mode=pl.Buffered(1))


def _mod_kernel(c_ref, w_ref, b_ref, o_ref):
    ca = _silu(c_ref[...])
    o_ref[...] = jnp.dot(ca, w_ref[...], preferred_element_type=F32) + b_ref[...]


def _mod_call(c, w_ada, b_ada):
    bsz, d = c.shape
    n = w_ada.shape[1]
    tn = 1024
    return pl.pallas_call(
        _mod_kernel,
        grid=(n // tn,),
        in_specs=[
            pl.BlockSpec((bsz, d), lambda j: (0, 0)),
            pl.BlockSpec((d, tn), lambda j: (0, j)),
            pl.BlockSpec((1, tn), lambda j: (0, j)),
        ],
        out_specs=pl.BlockSpec((bsz, tn), lambda j: (0, j)),
        out_shape=jax.ShapeDtypeStruct((bsz, n), F32),
        name="adaln_mod",
    )(c, w_ada, b_ada.reshape(1, n))


def _inproj_kernel(x_ref, sh_ref, sc_ref, g_ref, w_ref, lbl_ref, on_ref,
                   q_o, k_o, lfh_o, lfl_o, v_o, g_o, aq_o, ak_o, av_o, dmin_o):
    x = x_ref[...]
    ms = jnp.mean(x * x, axis=-1, keepdims=True)
    y = x * lax.rsqrt(ms + RMS_EPS) * g_ref[...]
    h = y * (1.0 + sc_ref[0]) + sh_ref[0]
    hb = h.astype(BF16)
    tm = hb.shape[0]

    lbl = lbl_ref[...]
    e = jnp.exp(lbl - jnp.max(lbl, axis=0, keepdims=True))
    lb = e[0:1] / (e[0:1] + e[1:2])

    grp_id = lax.broadcasted_iota(jnp.int32, (tm // HG_HALF, tm), 0)
    row_id = lax.broadcasted_iota(jnp.int32, (tm // HG_HALF, tm), 1)
    member = (row_id // HG_HALF == grp_id).astype(BF16)

    dmin = None
    for lo in range(0, 512, PROJ_COLS):
        cols = slice(lo, lo + PROJ_COLS)

        def proj(j):
            return jnp.dot(hb, w_ref[:, j * 512 + lo:j * 512 + lo + PROJ_COLS],
                           preferred_element_type=F32)

        sg = jax.nn.sigmoid(proj(1))
        lbc = lb[:, cols]
        lf = jnp.log2(lbc + (1.0 - lbc) * sg)
        lf_hi = lf.astype(BF16)
        lfh_o[:, cols] = lf_hi
        lfl_o[:, cols] = (lf - lf_hi.astype(F32)).astype(BF16)
        k_o[:, cols] = ((1.0 - lbc) * (1.0 - sg)).astype(BF16)
        totals = jnp.min(jnp.dot(member, lf_hi, preferred_element_type=F32),
                         axis=-1, keepdims=True)
        dmin = totals if dmin is None else jnp.minimum(dmin, totals)
        ak_o[:, cols] = proj(5)
        q_o[:, cols] = _silu(proj(0)).astype(BF16)
        av_o[:, cols] = proj(6)
        g_o[:, cols] = (_silu(proj(3)) * on_ref[:, cols]).astype(BF16)
        v_o[:, cols] = proj(2).astype(BF16)
        aq_o[:, cols] = proj(4) * (ATT_HEAD_DIM ** -0.5 * LOG2_E)
    dmin_o[0] = jnp.broadcast_to(dmin, dmin_o.shape[1:])


def _inproj_call(x2, mod3, norm_g, w_in_bf, lb_logits, hg_gain, seq):
    t, d = x2.shape
    tm = ROW_TILE
    steps_per_batch = seq // tm
    row = lambda i: (i, 0)
    out_dtypes = (BF16, BF16, BF16, BF16, BF16, BF16, F32, F32, F32)
    return pl.pallas_call(
        _inproj_kernel,
        grid=(t // tm,),
        in_specs=[
            pl.BlockSpec((tm, d), row),
            pl.BlockSpec((1, 1, d), lambda i: (i // steps_per_batch, 0, 0)),
            pl.BlockSpec((1, 1, d), lambda i: (i // steps_per_batch, 0, 1)),
            _const_spec((1, d)),
            _const_spec(w_in_bf.shape),
            _const_spec(lb_logits.shape),
            _const_spec((1, HG_WIDTH)),
        ],
        out_specs=[pl.BlockSpec((tm, 512), row) for _ in out_dtypes]
        + [pl.BlockSpec((1, tm // HG_HALF, 128), lambda i: (i, 0, 0))],
        out_shape=[jax.ShapeDtypeStruct((t, 512), dt) for dt in out_dtypes]
        + [jax.ShapeDtypeStruct((t // tm, tm // HG_HALF, 128), F32)],
        compiler_params=pltpu.CompilerParams(
            dimension_semantics=("arbitrary",), vmem_limit_bytes=VMEM_LIMIT_BYTES),
        name="in_proj",
    )(x2, mod3, mod3, norm_g.reshape(1, d), w_in_bf, lb_logits,
      jnp.tile(hg_gain.reshape(1, HG_DIM), (1, HG_HEADS)))


def _decay_selectors():
    c = HG_CHUNK
    t = np.arange(c)[:, None]
    u = np.arange(c)[None, :]
    blocks = []
    for l in range(HG_LEVELS):
        lo = (t >> l) << l
        hi = lo + (1 << l) - 1
        second = ((t >> l) & 1) == 1
        blocks.append(np.where(second, (u >= lo) & (u <= t), (u > t) & (u <= hi)))
    blocks.append(u <= t)
    blocks.append(u > t)
    return np.concatenate(blocks, axis=0).astype(np.float32)


def _hgrn_kernel(safe_ref, q_ref, k_ref, lfh_ref, lfl_ref, v_ref, g_ref, p_ref, o_ref, st_ref):
    c = HG_CHUNK
    half = HG_HALF
    step = pl.program_id(0) * pl.num_programs(1) + pl.program_id(1)

    @pl.when(pl.program_id(1) == 0)
    def _():
        st_ref[...] = jnp.zeros_like(st_ref)

    def head(a, h):
        return a[:, h * HG_DIM:(h + 1) * HG_DIM]

    def gram(a, b):
        return lax.dot_general(a, b, (((1,), (1,)), ((), ())), preferred_element_type=F32)

    def log_decay(rows):
        return jnp.concatenate([lfh_ref[0, rows, :], lfl_ref[0, rows, :]], axis=0)

    def exponents(lf2, block):
        return jnp.dot(p_ref[block * c:(block + 1) * c, :], lf2, preferred_element_type=F32)

    def scaled(x_bf, log2_scale):
        return x_bf * jnp.exp2(log2_scale).astype(BF16)

    def normed(rows, o, h):
        ms = jnp.mean(o * o, axis=-1, keepdims=True)
        return (o * lax.rsqrt(ms + RMS_EPS)).astype(BF16) * head(g_ref[0, rows, :], h)

    safe = safe_ref[step] > 0

    @pl.when(safe)
    def _():
        top_mask = (lax.broadcasted_iota(jnp.int32, (half, half), 1)
                    <= lax.broadcasted_iota(jnp.int32, (half, half), 0))
        bot_mask = (lax.broadcasted_iota(jnp.int32, (half, c), 1)
                    <= lax.broadcasted_iota(jnp.int32, (half, c), 0) + half)
        chunks = range(HG_GROUP)
        heads = range(HG_HEADS)
        rows = [slice(j * c, (j + 1) * c) for j in chunks]
        b = [exponents(log_decay(rows[j]), HG_LEVELS) for j in chunks]
        b_mid = [b[j][half - 1:half, :] for j in chunks]
        b_last = [b[j][c - 1:c, :] for j in chunks]
        q = [q_ref[0, rows[j], :] for j in chunks]
        k = [k_ref[0, rows[j], :] for j in chunks]
        qe = [scaled(q[j], b[j]) for j in chunks]
        k_top = [scaled(k[j][:half], -b[j][:half]) for j in chunks]
        k_mid = [scaled(k[j], b_mid[j] - b[j]) for j in chunks]
        q_bot = [scaled(q[j][half:], b[j][half:] - b_mid[j]) for j in chunks]
        s_top = [[jnp.where(top_mask, gram(head(qe[j], h)[:half], head(k_top[j], h)), 0.0)
                  .astype(BF16) for h in heads] for j in chunks]
        s_bot = [[jnp.where(bot_mask, gram(head(q_bot[j], h), head(k_mid[j], h)), 0.0)
                  .astype(BF16) for h in heads] for j in chunks]
        v = [v_ref[0, rows[j], :] for j in chunks]
        intra = [[jnp.concatenate(
            [jnp.dot(s_top[j][h], head(v[j], h)[:half], preferred_element_type=F32),
             jnp.dot(s_bot[j][h], head(v[j], h), preferred_element_type=F32)], axis=0)
            for h in heads] for j in chunks]
        kv = [[lax.dot_general(head(v[j], h), head(k_mid[j], h), (((0,), (0,)), ((), ())),
                               preferred_element_type=F32)
               * head(jnp.exp2(b_last[j] - b_mid[j]), h) for h in heads] for j in chunks]
        states = [st_ref[h] for h in heads]
        for j in chunks:
            chunk_decay = jnp.exp2(b_last[j])
            outs = []
            for h in heads:
                o = intra[j][h] + gram(head(qe[j], h), states[h].astype(BF16))
                states[h] = states[h] * head(chunk_decay, h) + kv[j][h]
                outs.append(normed(rows[j], o, h))
            o_ref[0, rows[j], :] = jnp.concatenate(outs, axis=1).astype(o_ref.dtype)
        for h in heads:
            st_ref[h] = states[h]

    @pl.when(jnp.logical_not(safe))
    def _():
        row = lax.broadcasted_iota(jnp.int32, (c, HG_WIDTH), 0)
        ti = lax.broadcasted_iota(jnp.int32, (c, c), 0)
        si = lax.broadcasted_iota(jnp.int32, (c, c), 1)
        txs = ti ^ si

        def one_chunk(j, carry):
            rows = pl.ds(pl.multiple_of(j * c, c), c)
            q = q_ref[0, rows, :]
            k = k_ref[0, rows, :]
            v = v_ref[0, rows, :]
            lf = log_decay(rows)
            scores = [jnp.where(ti == si, gram(head(q, h), head(k, h)), 0.0)
                      for h in range(HG_HEADS)]
            for l in range(HG_LEVELS):
                second = ((row >> l) & 1) == 1
                xl = scaled(jnp.where(second, q, k), exponents(lf, l))
                mask = (ti > si) & ((txs >> l) == 1)
                for h in range(HG_HEADS):
                    xh = head(xl, h)
                    scores[h] = jnp.where(mask, gram(xh, xh), scores[h])
            b = exponents(lf, HG_LEVELS)
            qe = scaled(q, b)
            k_end = scaled(k, exponents(lf, HG_LEVELS + 1))
            chunk_decay = jnp.exp2(b[c - 1:c, :])
            outs = []
            for h in range(HG_HEADS):
                st = st_ref[h]
                o = jnp.dot(scores[h].astype(BF16), head(v, h), preferred_element_type=F32)
                o = o + gram(head(qe, h), st.astype(BF16))
                st_ref[h] = st * head(chunk_decay, h) + lax.dot_general(
                    head(v, h), head(k_end, h), (((0,), (0,)), ((), ())),
                    preferred_element_type=F32)
                outs.append(normed(rows, o, h))
            o_ref[0, rows, :] = jnp.concatenate(outs, axis=1).astype(o_ref.dtype)
            return carry

        lax.fori_loop(0, HG_GROUP, one_chunk, 0)


def _hgrn_call(q, k, lf_hi, lf_lo, v, g, safe):
    bsz, seq, w = q.shape
    rows = HG_GROUP * HG_CHUNK
    sel = _decay_selectors()
    sel = jnp.asarray(np.concatenate([sel, sel], axis=1), dtype=BF16)
    blk = pl.BlockSpec((1, rows, w), lambda b, i, safe_ref: (b, i, 0))
    return pl.pallas_call(
        _hgrn_kernel,
        grid_spec=pltpu.PrefetchScalarGridSpec(
            num_scalar_prefetch=1,
            grid=(bsz, seq // rows),
            in_specs=[blk, blk, blk, blk, blk, blk, _const_spec(sel.shape)],
            out_specs=blk,
            scratch_shapes=[pltpu.VMEM((HG_HEADS, HG_DIM, HG_DIM), F32)],
        ),
        out_shape=jax.ShapeDtypeStruct((bsz, seq, w), BF16),
        compiler_params=pltpu.CompilerParams(
            dimension_semantics=("arbitrary", "arbitrary"), vmem_limit_bytes=VMEM_LIMIT_BYTES),
        name="hgrn2",
    )(safe, q, k, lf_hi, lf_lo, v, g, sel)


def _attn_biases():
    blk, grp = ATT_BLOCK, ATT_GROUP
    slab = blk // grp
    a = np.arange(blk)
    unslab = grp * (a % slab) + a // slab
    full, own = [], []
    for pos in (a, unslab):
        kpos = np.concatenate([pos, pos + blk])
        dist = blk + pos[:, None] - kpos[None, :]
        keep = (dist >= 0) & (dist <= ATT_SPAN)
        full.append(np.where(np.concatenate([keep, keep], axis=0), 0.0, NEG_BIG))
        keep_own = keep[:, blk:]
        own.append(np.where(np.concatenate([keep_own, keep_own], axis=0), 0.0, NEG_BIG))
    return np.stack(full).astype(np.float32), np.stack(own).astype(np.float32)


def _attn_kernel(q_ref, k_ref, v_ref, full_ref, own_ref, o_ref,
                 qd_ref, kd_ref, vd_ref, acc_ref, m_ref, l_ref):
    seq = q_ref.shape[1]
    blk = ATT_BLOCK
    grp = ATT_GROUP
    quarter = seq // grp
    slab = blk // grp

    def deinterleave(i, carry):
        base = pl.multiple_of(i * blk, blk)
        for src, dst in ((q_ref, qd_ref), (k_ref, kd_ref), (v_ref, vd_ref)):
            for c in range(grp):
                dst[pl.ds(c * quarter + base, blk), :] = (
                    src[0, pl.ds(c + grp * base, blk, stride=grp), :])
        return carry

    lax.fori_loop(0, quarter // blk, deinterleave, 0)

    first_head = lax.broadcasted_iota(jnp.int32, (blk, 128), 1) < ATT_HEAD_DIM
    stats = (m_ref, l_ref, acc_ref)

    def gather(ref, slices):
        parts = [ref[sl, :] for sl in slices]
        return parts[0] if len(parts) == 1 else jnp.concatenate(parts, axis=0)

    def scatter(ref, slices, val):
        n = val.shape[0] // len(slices)
        for i, sl in enumerate(slices):
            ref[sl, :] = val[i * n:(i + 1) * n]

    def process(items, merge):
        def keys(ref, cur, prev):
            own = gather(ref, cur)
            if prev is not None:
                own = jnp.concatenate([gather(ref, prev), own], axis=0)
            return own.astype(BF16)

        def scores(item):
            cur, prev, bias = item
            qf = gather(qd_ref, cur)
            q2 = jnp.concatenate([jnp.where(first_head, qf, 0.0), jnp.where(first_head, 0.0, qf)],
                                 axis=0).astype(BF16)
            return lax.dot_general(q2, keys(kd_ref, cur, prev), (((1,), (1,)), ((), ())),
                                   preferred_element_type=F32) + bias

        def softmax(s):
            m2 = jnp.max(s, axis=-1, keepdims=True)
            p = jnp.exp2(s - m2)
            return m2, jnp.sum(p, axis=-1, keepdims=True), p.astype(BF16)

        def values(item, state):
            cur, prev, _ = item
            m2, l2, p = state
            return m2, l2, jnp.dot(p, keys(vd_ref, cur, prev), preferred_element_type=F32)

        def commit(item, state):
            cur = item[0]
            new = tuple(jnp.where(first_head, x[:blk], x[blk:]) for x in state)
            if merge:
                m_old, l_old, o_old = (gather(ref, cur) for ref in stats)
                m_new, l_new, o_new = new
                m_tot = jnp.maximum(m_old, m_new)
                a_old = jnp.exp2(m_old - m_tot)
                a_new = jnp.exp2(m_new - m_tot)
                new = (m_tot, l_old * a_old + l_new * a_new, o_old * a_old + o_new * a_new)
            for ref, val in zip(stats, new):
                scatter(ref, cur, val)

        stages = (lambda it, st: scores(it), lambda it, st: softmax(st), values, commit)
        state = [None] * len(items)
        for t in range(len(items) + len(stages) - 1):
            for k in reversed(range(len(stages))):
                i = t - k
                if 0 <= i < len(items):
                    state[i] = stages[k](items[i], state[i])

    def rows4(c, n):
        return [pl.ds(pl.multiple_of(c * quarter + n * blk, blk), blk)]

    per_trip = ATT_ITEMS // grp

    def items4(n0, first):
        return [(rows4(c, n0 + j), None, own_ref[0]) if first and j == 0
                else (rows4(c, n0 + j), rows4(c, n0 + j - 1), full_ref[0])
                for j in range(per_trip) for c in range(grp)]

    process(items4(0, True), merge=False)

    def body4(i, carry):
        process(items4(i * per_trip, False), merge=False)
        return carry

    lax.fori_loop(1, quarter // (blk * per_trip), body4, 0)

    def rows1(n):
        return [pl.ds(pl.multiple_of(c * quarter + n * slab, slab), slab) for c in range(grp)]

    process([(rows1(0), None, own_ref[1])]
            + [(rows1(n), rows1(n - 1), full_ref[1]) for n in range(1, ATT_ITEMS)], merge=True)

    def body1(i, carry):
        n0 = i * ATT_ITEMS
        process([(rows1(n0 + j), rows1(n0 + j - 1), full_ref[1]) for j in range(ATT_ITEMS)],
                merge=True)
        return carry

    lax.fori_loop(1, seq // (blk * ATT_ITEMS), body1, 0)

    sub_blocks = seq // (DILATIONS[-1] * blk)

    def rows16(c, e, n):
        return [pl.ds(c * quarter + e + n * grp * blk, blk, stride=grp)]

    res_per_trip = max(1, ATT_ITEMS // (grp * sub_blocks))

    def body16(i, carry):
        items = []
        for j in range(res_per_trip):
            c = i * res_per_trip + j
            for e in range(grp):
                items.append((rows16(c, e, 0), None, own_ref[0]))
                for n in range(1, sub_blocks):
                    items.append((rows16(c, e, n), rows16(c, e, n - 1), full_ref[0]))
        process(items, merge=True)
        return carry

    lax.fori_loop(0, grp // res_per_trip, body16, 0)

    def finish(i, carry):
        base = pl.multiple_of(i * blk, blk)
        for c in range(grp):
            rows = pl.ds(c * quarter + base, blk)
            m_ref[pl.ds(c + grp * base, blk, stride=grp), :] = acc_ref[rows, :] / l_ref[rows, :]
        return carry

    lax.fori_loop(0, quarter // blk, finish, 0)

    def emit(i, carry):
        rows = pl.ds(pl.multiple_of(i * blk, blk), blk)
        o_ref[0, rows, :] = m_ref[rows, :].astype(o_ref.dtype)
        return carry

    lax.fori_loop(0, seq // blk, emit, 0)


def _attn_call(aq, ak, av):
    bsz, seq, w = aq.shape
    full, own = (jnp.asarray(x) for x in _attn_biases())
    blk = pl.BlockSpec((1, seq, 128), lambda b, hp: (b, 0, hp))
    return pl.pallas_call(
        _attn_kernel,
        grid=(bsz, w // 128),
        in_specs=[blk, blk, blk, _const_spec(full.shape), _const_spec(own.shape)],
        out_specs=blk,
        out_shape=jax.ShapeDtypeStruct((bsz, seq, w), BF16),
        scratch_shapes=[pltpu.VMEM((seq, 128), F32) for _ in range(6)],
        compiler_params=pltpu.CompilerParams(
            dimension_semantics=("arbitrary", "arbitrary"), vmem_limit_bytes=VMEM_LIMIT_BYTES),
        name="dilated_attn",
    )(aq, ak, av, full, own)


def _tail_kernel(x_ref, hg_ref, att_ref, g1_ref, sh2_ref, sc2_ref, g2_ref,
                 an_ref, n2_ref, fin_ref, wo_ref, wgu_ref, wd_ref, o_ref):
    d_ff = wd_ref.shape[0]
    att = att_ref[...].astype(F32)
    ms = jnp.mean(att * att, axis=-1, keepdims=True)
    att_n = (att * lax.rsqrt(ms + RMS_EPS) * an_ref[...]).astype(BF16)
    mix = jnp.dot(hg_ref[...], wo_ref[:HG_WIDTH, :], preferred_element_type=F32)
    mix = mix + jnp.dot(att_n, wo_ref[HG_WIDTH:, :], preferred_element_type=F32)
    x1 = x_ref[...] + g1_ref[0] * mix

    ms = jnp.mean(x1 * x1, axis=-1, keepdims=True)
    h = x1 * lax.rsqrt(ms + RMS_EPS) * n2_ref[...]
    hb = (h * (1.0 + sc2_ref[0]) + sh2_ref[0]).astype(BF16)
    ffn = jnp.zeros_like(x1)
    for j in range(d_ff // FF_CHUNK):
        lo = j * FF_CHUNK
        a = jnp.dot(hb, wgu_ref[:, lo:lo + FF_CHUNK], preferred_element_type=F32)
        u = jnp.dot(hb, wgu_ref[:, d_ff + lo:d_ff + lo + FF_CHUNK], preferred_element_type=F32)
        act = (_silu(a) * u).astype(BF16)
        ffn = ffn + jnp.dot(act, wd_ref[lo:lo + FF_CHUNK, :], preferred_element_type=F32)
    x2 = x1 + g2_ref[0] * ffn
    ms = jnp.mean(x2 * x2, axis=-1, keepdims=True)
    o_ref[...] = x2 * lax.rsqrt(ms + RMS_EPS) * fin_ref[...]


def _tail_call(x2, hg, att, mod3, att_g, norm2_g, final_g, wo_bf, wgu_bf, wd_bf, seq):
    t, d = x2.shape
    tm = ROW_TILE
    steps_per_batch = seq // tm
    row = lambda i: (i, 0)
    mod_col = lambda col: pl.BlockSpec((1, 1, d), lambda i: (i // steps_per_batch, 0, col))
    return pl.pallas_call(
        _tail_kernel,
        grid=(t // tm,),
        in_specs=[
            pl.BlockSpec((tm, d), row),
            pl.BlockSpec((tm, HG_WIDTH), row),
            pl.BlockSpec((tm, ATT_WIDTH), row),
            mod_col(2), mod_col(3), mod_col(4), mod_col(5),
            _const_spec((1, ATT_WIDTH)), _const_spec((1, d)), _const_spec((1, d)),
            _const_spec(wo_bf.shape), _const_spec(wgu_bf.shape), _const_spec(wd_bf.shape),
        ],
        out_specs=pl.BlockSpec((tm, d), row),
        out_shape=jax.ShapeDtypeStruct((t, d), F32),
        compiler_params=pltpu.CompilerParams(
            dimension_semantics=("arbitrary",), vmem_limit_bytes=VMEM_LIMIT_BYTES),
        name="outproj_ffn",
    )(x2, hg, att, mod3, mod3, mod3, mod3,
      att_g.reshape(1, ATT_WIDTH), norm2_g.reshape(1, d), final_g.reshape(1, d),
      wo_bf, wgu_bf, wd_bf)


def kernel(x, c, w_ada, b_ada, norm1_g, w_in, hg_lb_logits, hg_onorm_g, att_onorm_g,
           w_out, norm2_g, w_gate_up, w_down, final_g):
    bsz, seq, d = x.shape
    assert w_in.shape[0] == 1 and hg_lb_logits.shape[0] == 2, "single-layer block expected"
    assert seq % (DILATIONS[-1] * ATT_BLOCK) == 0 and seq % ROW_TILE == 0
    t = bsz * seq
    x2 = x.reshape(t, d)

    mod = _mod_call(c, w_ada[0], b_ada[0])
    mod3 = mod.reshape(bsz, 1, 6 * d)

    q, k, lf_hi, lf_lo, v, g, aq, ak, av, half_decay = _inproj_call(
        x2, mod3, norm1_g[0], w_in[0].astype(BF16), hg_lb_logits, hg_onorm_g[0], seq)
    safe = (jnp.min(half_decay[:, :, 0].reshape(-1, HG_GROUP * HG_CHUNK // HG_HALF), axis=1)
            >= HG_SAFE_LOG2_DECAY).astype(jnp.int32)

    as_seq = lambda a: a.reshape(bsz, seq, a.shape[-1])
    hg = _hgrn_call(as_seq(q), as_seq(k), as_seq(lf_hi), as_seq(lf_lo), as_seq(v), as_seq(g), safe)
    att = _attn_call(as_seq(aq), as_seq(ak), as_seq(av))

    out = _tail_call(
        x2, hg.reshape(t, HG_WIDTH), att.reshape(t, ATT_WIDTH), mod3,
        att_onorm_g[0], norm2_g[0], final_g,
        w_out[0].astype(BF16), w_gate_up[0].astype(BF16), w_down[0].astype(BF16), seq)
    return out.reshape(bsz, seq, d)
```

```python
import functools

import numpy as np
import jax
import jax.numpy as jnp
from jax import lax
from jax.experimental import pallas as pl
from jax.experimental.pallas import tpu as pltpu

F32 = jnp.float32
BF16 = jnp.bfloat16

RMS_EPS = 1e-6
HG_HEADS = 4
HG_DIM = 128
HG_WIDTH = HG_HEADS * HG_DIM
ATT_HEAD_DIM = 64
ATT_WIDTH = 512
ATT_BLOCK = 128
DILATIONS = (1, 4, 16)
ATT_SPAN = 128
NEG_BIG = -1e30
ATT_ITEMS = 32
LOG2_E = 1.4426950408889634
ATT_GROUP = 4

HG_CHUNK = 128
HG_LEVELS = 7
HG_HALF = HG_CHUNK // 2
HG_SAFE_LOG2_DECAY = -85.0
HG_GROUP = 8
ROW_TILE = 512
FF_CHUNK = 256
PROJ_COLS = 512
VMEM_LIMIT_BYTES = 56 * 1024 * 1024


def _silu(x):
    return x * jax.nn.sigmoid(x)


def _const_spec(shape):
    nd = len(shape)
    return pl.BlockSpec(shape, lambda *_: (0,) * nd, pipeline_mode=pl.Buffered(1))


def _mod_kernel(c_ref, w_ref, b_ref, o_ref):
    ca = _silu(c_ref[...])
    o_ref[...] = jnp.dot(ca, w_ref[...], preferred_element_type=F32) + b_ref[...]


def _mod_call(c, w_ada, b_ada):
    bsz, d = c.shape
    n = w_ada.shape[1]
    tn = 1024
    return pl.pallas_call(
        _mod_kernel,
        grid=(n // tn,),
        in_specs=[
            pl.BlockSpec((bsz, d), lambda j: (0, 0)),
            pl.BlockSpec((d, tn), lambda j: (0, j)),
            pl.BlockSpec((1, tn), lambda j: (0, j)),
        ],
        out_specs=pl.BlockSpec((bsz, tn), lambda j: (0, j)),
        out_shape=jax.ShapeDtypeStruct((bsz, n), F32),
        name="adaln_mod",
    )(c, w_ada, b_ada.reshape(1, n))


def _inproj_kernel(x_ref, sh_ref, sc_ref, g_ref, w_ref, lbl_ref, on_ref,
                   q_o, k_o, lfh_o, lfl_o, v_o, g_o, aq_o, ak_o, av_o, dmin_o):
    x = x_ref[...]
    ms = jnp.mean(x * x, axis=-1, keepdims=True)
    y = x * lax.rsqrt(ms + RMS_EPS) * g_ref[...]
    h = y * (1.0 + sc_ref[0]) + sh_ref[0]
    hb = h.astype(BF16)
    tm = hb.shape[0]

    lbl = lbl_ref[...]
    e = jnp.exp(lbl - jnp.max(lbl, axis=0, keepdims=True))
    lb = e[0:1] / (e[0:1] + e[1:2])

    grp_id = lax.broadcasted_iota(jnp.int32, (tm // HG_HALF, tm), 0)
    row_id = lax.broadcasted_iota(jnp.int32, (tm // HG_HALF, tm), 1)
    member = (row_id // HG_HALF == grp_id).astype(BF16)

    dmin = None
    for lo in range(0, 512, PROJ_COLS):
        cols = slice(lo, lo + PROJ_COLS)

        def proj(j):
            return jnp.dot(hb, w_ref[:, j * 512 + lo:j * 512 + lo + PROJ_COLS],
                           preferred_element_type=F32)

        sg = jax.nn.sigmoid(proj(1))
        lbc = lb[:, cols]
        lf = jnp.log2(lbc + (1.0 - lbc) * sg)
        lf_hi = lf.astype(BF16)
        lfh_o[:, cols] = lf_hi
        lfl_o[:, cols] = (lf - lf_hi.astype(F32)).astype(BF16)
        k_o[:, cols] = ((1.0 - lbc) * (1.0 - sg)).astype(BF16)
        ak_o[:, cols] = proj(5)
        q_o[:, cols] = _silu(proj(0)).astype(BF16)
        av_o[:, cols] = proj(6)
        g_o[:, cols] = (_silu(proj(3)) * on_ref[:, cols]).astype(BF16)
        v_o[:, cols] = proj(2).astype(BF16)
        aq_o[:, cols] = proj(4) * (ATT_HEAD_DIM ** -0.5 * LOG2_E)
        totals = jnp.min(jnp.dot(member, lf_hi, preferred_element_type=F32),
                         axis=-1, keepdims=True)
        dmin = totals if dmin is None else jnp.minimum(dmin, totals)
    dmin_o[0] = jnp.broadcast_to(dmin, dmin_o.shape[1:])


def _inproj_call(x2, mod3, norm_g, w_in_bf, lb_logits, hg_gain, seq):
    t, d = x2.shape
    tm = ROW_TILE
    steps_per_batch = seq // tm
    row = lambda i: (i, 0)
    out_dtypes = (BF16, BF16, BF16, BF16, BF16, BF16, F32, F32, F32)
    return pl.pallas_call(
        _inproj_kernel,
        grid=(t // tm,),
        in_specs=[
            pl.BlockSpec((tm, d), row),
            pl.BlockSpec((1, 1, d), lambda i: (i // steps_per_batch, 0, 0)),
            pl.BlockSpec((1, 1, d), lambda i: (i // steps_per_batch, 0, 1)),
            _const_spec((1, d)),
            _const_spec(w_in_bf.shape),
            _const_spec(lb_logits.shape),
            _const_spec((1, HG_WIDTH)),
        ],
        out_specs=[pl.BlockSpec((tm, 512), row) for _ in out_dtypes]
        + [pl.BlockSpec((1, tm // HG_HALF, 128), lambda i: (i, 0, 0))],
        out_shape=[jax.ShapeDtypeStruct((t, 512), dt) for dt in out_dtypes]
        + [jax.ShapeDtypeStruct((t // tm, tm // HG_HALF, 128), F32)],
        compiler_params=pltpu.CompilerParams(
            dimension_semantics=("arbitrary",), vmem_limit_bytes=VMEM_LIMIT_BYTES),
        name="in_proj",
    )(x2, mod3, mod3, norm_g.reshape(1, d), w_in_bf, lb_logits,
      jnp.tile(hg_gain.reshape(1, HG_DIM), (1, HG_HEADS)))


def _decay_selectors():
    c = HG_CHUNK
    t = np.arange(c)[:, None]
    u = np.arange(c)[None, :]
    blocks = []
    for l in range(HG_LEVELS):
        lo = (t >> l) << l
        hi = lo + (1 << l) - 1
        second = ((t >> l) & 1) == 1
        blocks.append(np.where(second, (u >= lo) & (u <= t), (u > t) & (u <= hi)))
    blocks.append(u <= t)
    blocks.append(u > t)
    return np.concatenate(blocks, axis=0).astype(np.float32)


def _hgrn_kernel(safe_ref, q_ref, k_ref, lfh_ref, lfl_ref, v_ref, g_ref, p_ref, o_ref, st_ref):
    c = HG_CHUNK
    half = HG_HALF
    step = pl.program_id(0) * pl.num_programs(1) + pl.program_id(1)

    @pl.when(pl.program_id(1) == 0)
    def _():
        st_ref[...] = jnp.zeros_like(st_ref)

    def head(a, h):
        return a[:, h * HG_DIM:(h + 1) * HG_DIM]

    def gram(a, b):
        return lax.dot_general(a, b, (((1,), (1,)), ((), ())), preferred_element_type=F32)

    def log_decay(rows):
        return jnp.concatenate([lfh_ref[0, rows, :], lfl_ref[0, rows, :]], axis=0)

    def exponents(lf2, block):
        return jnp.dot(p_ref[block * c:(block + 1) * c, :], lf2, preferred_element_type=F32)

    def scaled(x_bf, log2_scale):
        return x_bf * jnp.exp2(log2_scale).astype(BF16)

    def normed(rows, o, h):
        ms = jnp.mean(o * o, axis=-1, keepdims=True)
        return (o * lax.rsqrt(ms + RMS_EPS)).astype(BF16) * head(g_ref[0, rows, :], h)

    safe = safe_ref[step] > 0

    @pl.when(safe)
    def _():
        top_mask = (lax.broadcasted_iota(jnp.int32, (half, half), 1)
                    <= lax.broadcasted_iota(jnp.int32, (half, half), 0))
        bot_mask = (lax.broadcasted_iota(jnp.int32, (half, c), 1)
                    <= lax.broadcasted_iota(jnp.int32, (half, c), 0) + half)
        heads = range(HG_HEADS)

        def decay(j):
            rows = slice(j * c, (j + 1) * c)
            return rows, exponents(log_decay(rows), HG_LEVELS)

        def scale(st):
            rows, b = st
            b_mid = b[half - 1:half, :]
            q = q_ref[0, rows, :]
            k = k_ref[0, rows, :]
            return dict(rows=rows, b_mid=b_mid, b_last=b[c - 1:c, :], qe=scaled(q, b),
                        k_top=scaled(k[:half], -b[:half]), k_mid=scaled(k, b_mid - b),
                        q_bot=scaled(q[half:], b[half:] - b_mid))

        def score(st):
            st["s_top"] = [jnp.where(top_mask, gram(head(st["qe"], h)[:half],
                                                    head(st["k_top"], h)), 0.0).astype(BF16)
                           for h in heads]
            st["s_bot"] = [jnp.where(bot_mask, gram(head(st["q_bot"], h),
                                                    head(st["k_mid"], h)), 0.0).astype(BF16)
                           for h in heads]
            return st

        def apply(st):
            v = v_ref[0, st["rows"], :]
            st["intra"] = [jnp.concatenate(
                [jnp.dot(st["s_top"][h], head(v, h)[:half], preferred_element_type=F32),
                 jnp.dot(st["s_bot"][h], head(v, h), preferred_element_type=F32)], axis=0)
                for h in heads]
            tail_decay = jnp.exp2(st["b_last"] - st["b_mid"])
            st["kv"] = [lax.dot_general(head(v, h), head(st["k_mid"], h),
                                        (((0,), (0,)), ((), ())), preferred_element_type=F32)
                        * head(tail_decay, h) for h in heads]
            return st

        states = [st_ref[h] for h in heads]

        def recur(st):
            chunk_decay = jnp.exp2(st["b_last"])
            outs = []
            for h in heads:
                o = st["intra"][h] + gram(head(st["qe"], h), states[h].astype(BF16))
                states[h] = states[h] * head(chunk_decay, h) + st["kv"][h]
                outs.append(normed(st["rows"], o, h))
            o_ref[0, st["rows"], :] = jnp.concatenate(outs, axis=1).astype(o_ref.dtype)
            return None

        stages = (scale, score, apply, recur)
        live = [None] * HG_GROUP
        for t in range(HG_GROUP + len(stages)):
            for kk in reversed(range(len(stages) + 1)):
                j = t - kk
                if 0 <= j < HG_GROUP:
                    live[j] = decay(j) if kk == 0 else stages[kk - 1](live[j])
        for h in heads:
            st_ref[h] = states[h]


    @pl.when(jnp.logical_not(safe))
    def _():
        row = lax.broadcasted_iota(jnp.int32, (c, HG_WIDTH), 0)
        ti = lax.broadcasted_iota(jnp.int32, (c, c), 0)
        si = lax.broadcasted_iota(jnp.int32, (c, c), 1)
        txs = ti ^ si

        def one_chunk(j, carry):
            rows = pl.ds(pl.multiple_of(j * c, c), c)
            q = q_ref[0, rows, :]
            k = k_ref[0, rows, :]
            v = v_ref[0, rows, :]
            lf = log_decay(rows)
            scores = [jnp.where(ti == si, gram(head(q, h), head(k, h)), 0.0)
                      for h in range(HG_HEADS)]
            for l in range(HG_LEVELS):
                second = ((row >> l) & 1) == 1
                xl = scaled(jnp.where(second, q, k), exponents(lf, l))
                mask = (ti > si) & ((txs >> l) == 1)
                for h in range(HG_HEADS):
                    xh = head(xl, h)
                    scores[h] = jnp.where(mask, gram(xh, xh), scores[h])
            b = exponents(lf, HG_LEVELS)
            qe = scaled(q, b)
            k_end = scaled(k, exponents(lf, HG_LEVELS + 1))
            chunk_decay = jnp.exp2(b[c - 1:c, :])
            outs = []
            for h in range(HG_HEADS):
                st = st_ref[h]
                o = jnp.dot(scores[h].astype(BF16), head(v, h), preferred_element_type=F32)
                o = o + gram(head(qe, h), st.astype(BF16))
                st_ref[h] = st * head(chunk_decay, h) + lax.dot_general(
                    head(v, h), head(k_end, h), (((0,), (0,)), ((), ())),
                    preferred_element_type=F32)
                outs.append(normed(rows, o, h))
            o_ref[0, rows, :] = jnp.concatenate(outs, axis=1).astype(o_ref.dtype)
            return carry

        lax.fori_loop(0, HG_GROUP, one_chunk, 0)


def _hgrn_call(q, k, lf_hi, lf_lo, v, g, safe):
    bsz, seq, w = q.shape
    rows = HG_GROUP * HG_CHUNK
    sel = _decay_selectors()
    sel = jnp.asarray(np.concatenate([sel, sel], axis=1), dtype=BF16)
    blk = pl.BlockSpec((1, rows, w), lambda b, i, safe_ref: (b, i, 0))
    return pl.pallas_call(
        _hgrn_kernel,
        grid_spec=pltpu.PrefetchScalarGridSpec(
            num_scalar_prefetch=1,
            grid=(bsz, seq // rows),
            in_specs=[blk, blk, blk, blk, blk, blk, _const_spec(sel.shape)],
            out_specs=blk,
            scratch_shapes=[pltpu.VMEM((HG_HEADS, HG_DIM, HG_DIM), F32)],
        ),
        out_shape=jax.ShapeDtypeStruct((bsz, seq, w), BF16),
        compiler_params=pltpu.CompilerParams(
            dimension_semantics=("arbitrary", "arbitrary"), vmem_limit_bytes=VMEM_LIMIT_BYTES),
        name="hgrn2",
    )(safe, q, k, lf_hi, lf_lo, v, g, sel)


def _attn_biases():
    blk, grp = ATT_BLOCK, ATT_GROUP
    slab = blk // grp
    a = np.arange(blk)
    unslab = grp * (a % slab) + a // slab
    full, own = [], []
    for pos in (a, unslab):
        kpos = np.concatenate([pos, pos + blk])
        dist = blk + pos[:, None] - kpos[None, :]
        keep = (dist >= 0) & (dist <= ATT_SPAN)
        full.append(np.where(np.concatenate([keep, keep], axis=0), 0.0, NEG_BIG))
        keep_own = keep[:, blk:]
        own.append(np.where(np.concatenate([keep_own, keep_own], axis=0), 0.0, NEG_BIG))
    return np.stack(full).astype(np.float32), np.stack(own).astype(np.float32)


def _attn_kernel(q_ref, k_ref, v_ref, full_ref, own_ref, o_ref,
                 qd_ref, kd_ref, vd_ref, acc_ref, m_ref, l_ref):
    seq = q_ref.shape[1]
    blk = ATT_BLOCK
    grp = ATT_GROUP
    quarter = seq // grp
    slab = blk // grp

    def deinterleave(i, carry):
        base = pl.multiple_of(i * blk, blk)
        for src, dst in ((q_ref, qd_ref), (k_ref, kd_ref), (v_ref, vd_ref)):
            for c in range(grp):
                dst[pl.ds(c * quarter + base, blk), :] = (
                    src[0, pl.ds(c + grp * base, blk, stride=grp), :])
        return carry

    lax.fori_loop(0, quarter // blk, deinterleave, 0)

    first_head = lax.broadcasted_iota(jnp.int32, (blk, 128), 1) < ATT_HEAD_DIM
    stats = (m_ref, l_ref, acc_ref)

    def gather(ref, slices):
        parts = [ref[sl, :] for sl in slices]
        return parts[0] if len(parts) == 1 else jnp.concatenate(parts, axis=0)

    def scatter(ref, slices, val):
        n = val.shape[0] // len(slices)
        for i, sl in enumerate(slices):
            ref[sl, :] = val[i * n:(i + 1) * n]

    def process(items, merge):
        def keys(ref, cur, prev):
            own = gather(ref, cur)
            if prev is not None:
                own = jnp.concatenate([gather(ref, prev), own], axis=0)
            return own.astype(BF16)

        def scores(item):
            cur, prev, bias = item
            qf = gather(qd_ref, cur)
            q2 = jnp.concatenate([jnp.where(first_head, qf, 0.0), jnp.where(first_head, 0.0, qf)],
                                 axis=0).astype(BF16)
            return lax.dot_general(q2, keys(kd_ref, cur, prev), (((1,), (1,)), ((), ())),
                                   preferred_element_type=F32) + bias

        def softmax(s):
            m2 = jnp.max(s, axis=-1, keepdims=True)
            p = jnp.exp2(s - m2)
            return m2, jnp.sum(p, axis=-1, keepdims=True), p.astype(BF16)

        def values(item, state):
            cur, prev, _ = item
            m2, l2, p = state
            return m2, l2, jnp.dot(p, keys(vd_ref, cur, prev), preferred_element_type=F32)

        def commit(item, state):
            cur = item[0]
            new = tuple(jnp.where(first_head, x[:blk], x[blk:]) for x in state)
            if merge:
                m_old, l_old, o_old = (gather(ref, cur) for ref in stats)
                m_new, l_new, o_new = new
                m_tot = jnp.maximum(m_old, m_new)
                a_old = jnp.exp2(m_old - m_tot)
                a_new = jnp.exp2(m_new - m_tot)
                new = (m_tot, l_old * a_old + l_new * a_new, o_old * a_old + o_new * a_new)
            for ref, val in zip(stats, new):
                scatter(ref, cur, val)

        stages = (lambda it, st: scores(it), lambda it, st: softmax(st), values, commit)
        state = [None] * len(items)
        for t in range(len(items) + len(stages) - 1):
            for k in reversed(range(len(stages))):
                i = t - k
                if 0 <= i < len(items):
                    state[i] = stages[k](items[i], state[i])

    def rows4(c, n):
        return [pl.ds(pl.multiple_of(c * quarter + n * blk, blk), blk)]

    per_trip = ATT_ITEMS // grp

    def items4(n0, first):
        return [(rows4(c, n0 + j), None, own_ref[0]) if first and j == 0
                else (rows4(c, n0 + j), rows4(c, n0 + j - 1), full_ref[0])
                for j in range(per_trip) for c in range(grp)]

    process(items4(0, True), merge=False)

    def body4(i, carry):
        process(items4(i * per_trip, False), merge=False)
        return carry

    lax.fori_loop(1, quarter // (blk * per_trip), body4, 0)

    def rows1(n):
        return [pl.ds(pl.multiple_of(c * quarter + n * slab, slab), slab) for c in range(grp)]

    process([(rows1(0), None, own_ref[1])]
            + [(rows1(n), rows1(n - 1), full_ref[1]) for n in range(1, ATT_ITEMS)], merge=True)

    def body1(i, carry):
        n0 = i * ATT_ITEMS
        process([(rows1(n0 + j), rows1(n0 + j - 1), full_ref[1]) for j in range(ATT_ITEMS)],
                merge=True)
        return carry

    lax.fori_loop(1, seq // (blk * ATT_ITEMS), body1, 0)

    sub_blocks = seq // (DILATIONS[-1] * blk)

    def rows16(c, e, n):
        return [pl.ds(c * quarter + e + n * grp * blk, blk, stride=grp)]

    res_per_trip = max(1, ATT_ITEMS // (grp * sub_blocks))

    def body16(i, carry):
        items = []
        for j in range(res_per_trip):
            c = i * res_per_trip + j
            for e in range(grp):
                items.append((rows16(c, e, 0), None, own_ref[0]))
                for n in range(1, sub_blocks):
                    items.append((rows16(c, e, n), rows16(c, e, n - 1), full_ref[0]))
        process(items, merge=True)
        return carry

    lax.fori_loop(0, grp // res_per_trip, body16, 0)

    def finish(i, carry):
        base = pl.multiple_of(i * blk, blk)
        for c in range(grp):
            rows = pl.ds(c * quarter + base, blk)
            m_ref[pl.ds(c + grp * base, blk, stride=grp), :] = acc_ref[rows, :] / l_ref[rows, :]
        return carry

    lax.fori_loop(0, quarter // blk, finish, 0)

    def emit(i, carry):
        rows = pl.ds(pl.multiple_of(i * blk, blk), blk)
        o_ref[0, rows, :] = m_ref[rows, :].astype(o_ref.dtype)
        return carry

    lax.fori_loop(0, seq // blk, emit, 0)


def _attn_call(aq, ak, av):
    bsz, seq, w = aq.shape
    full, own = (jnp.asarray(x) for x in _attn_biases())
    blk = pl.BlockSpec((1, seq, 128), lambda b, hp: (b, 0, hp))
    return pl.pallas_call(
        _attn_kernel,
        grid=(bsz, w // 128),
        in_specs=[blk, blk, blk, _const_spec(full.shape), _const_spec(own.shape)],
        out_specs=blk,
        out_shape=jax.ShapeDtypeStruct((bsz, seq, w), BF16),
        scratch_shapes=[pltpu.VMEM((seq, 128), F32) for _ in range(6)],
        compiler_params=pltpu.CompilerParams(
            dimension_semantics=("arbitrary", "arbitrary"), vmem_limit_bytes=VMEM_LIMIT_BYTES),
        name="dilated_attn",
    )(aq, ak, av, full, own)


def _tail_kernel(x_ref, hg_ref, att_ref, g1_ref, sh2_ref, sc2_ref, g2_ref,
                 an_ref, n2_ref, fin_ref, wo_ref, wgu_ref, wd_ref, o_ref):
    d_ff = wd_ref.shape[0]
    att = att_ref[...].astype(F32)
    ms = jnp.mean(att * att, axis=-1, keepdims=True)
    att_n = (att * lax.rsqrt(ms + RMS_EPS) * an_ref[...]).astype(BF16)
    mix = jnp.dot(hg_ref[...], wo_ref[:HG_WIDTH, :], preferred_element_type=F32)
    mix = mix + jnp.dot(att_n, wo_ref[HG_WIDTH:, :], preferred_element_type=F32)
    x1 = x_ref[...] + g1_ref[0] * mix

    ms = jnp.mean(x1 * x1, axis=-1, keepdims=True)
    h = x1 * lax.rsqrt(ms + RMS_EPS) * n2_ref[...]
    hb = (h * (1.0 + sc2_ref[0]) + sh2_ref[0]).astype(BF16)
    ffn = jnp.zeros_like(x1)
    for j in range(d_ff // FF_CHUNK):
        lo = j * FF_CHUNK
        a = jnp.dot(hb, wgu_ref[:, lo:lo + FF_CHUNK], preferred_element_type=F32)
        u = jnp.dot(hb, wgu_ref[:, d_ff + lo:d_ff + lo + FF_CHUNK], preferred_element_type=F32)
        act = (_silu(a) * u).astype(BF16)
        ffn = ffn + jnp.dot(act, wd_ref[lo:lo + FF_CHUNK, :], preferred_element_type=F32)
    x2 = x1 + g2_ref[0] * ffn
    ms = jnp.mean(x2 * x2, axis=-1, keepdims=True)
    o_ref[...] = x2 * lax.rsqrt(ms + RMS_EPS) * fin_ref[...]


def _tail_call(x2, hg, att, mod3, att_g, norm2_g, final_g, wo_bf, wgu_bf, wd_bf, seq):
    t, d = x2.shape
    tm = ROW_TILE
    steps_per_batch = seq // tm
    row = lambda i: (i, 0)
    mod_col = lambda col: pl.BlockSpec((1, 1, d), lambda i: (i // steps_per_batch, 0, col))
    return pl.pallas_call(
        _tail_kernel,
        grid=(t // tm,),
        in_specs=[
            pl.BlockSpec((tm, d), row),
            pl.BlockSpec((tm, HG_WIDTH), row),
            pl.BlockSpec((tm, ATT_WIDTH), row),
            mod_col(2), mod_col(3), mod_col(4), mod_col(5),
            _const_spec((1, ATT_WIDTH)), _const_spec((1, d)), _const_spec((1, d)),
            _const_spec(wo_bf.shape), _const_spec(wgu_bf.shape), _const_spec(wd_bf.shape),
        ],
        out_specs=pl.BlockSpec((tm, d), row),
        out_shape=jax.ShapeDtypeStruct((t, d), F32),
        compiler_params=pltpu.CompilerParams(
            dimension_semantics=("arbitrary",), vmem_limit_bytes=VMEM_LIMIT_BYTES),
        name="outproj_ffn",
    )(x2, hg, att, mod3, mod3, mod3, mod3,
      att_g.reshape(1, ATT_WIDTH), norm2_g.reshape(1, d), final_g.reshape(1, d),
      wo_bf, wgu_bf, wd_bf)


def kernel(x, c, w_ada, b_ada, norm1_g, w_in, hg_lb_logits, hg_onorm_g, att_onorm_g,
           w_out, norm2_g, w_gate_up, w_down, final_g):
    bsz, seq, d = x.shape
    assert w_in.shape[0] == 1 and hg_lb_logits.shape[0] == 2, "single-layer block expected"
    assert seq % (DILATIONS[-1] * ATT_BLOCK) == 0 and seq % ROW_TILE == 0
    t = bsz * seq
    x2 = x.reshape(t, d)

    mod = _mod_call(c, w_ada[0], b_ada[0])
    mod3 = mod.reshape(bsz, 1, 6 * d)

    q, k, lf_hi, lf_lo, v, g, aq, ak, av, half_decay = _inproj_call(
        x2, mod3, norm1_g[0], w_in[0].astype(BF16), hg_lb_logits, hg_onorm_g[0], seq)
    safe = (jnp.min(half_decay[:, :, 0].reshape(-1, HG_GROUP * HG_CHUNK // HG_HALF), axis=1)
            >= HG_SAFE_LOG2_DECAY).astype(jnp.int32)

    as_seq = lambda a: a.reshape(bsz, seq, a.shape[-1])
    hg = _hgrn_call(as_seq(q), as_seq(k), as_seq(lf_hi), as_seq(lf_lo), as_seq(v), as_seq(g), safe)
    att = _attn_call(as_seq(aq), as_seq(ak), as_seq(av))

    out = _tail_call(
        x2, hg.reshape(t, HG_WIDTH), att.reshape(t, ATT_WIDTH), mod3,
        att_onorm_g[0], norm2_g[0], final_g,
        w_out[0].astype(BF16), w_gate_up[0].astype(BF16), w_down[0].astype(BF16), seq)
    return out.reshape(bsz, seq, d)
```

```python
import functools

import numpy as np
import jax
import jax.numpy as jnp
from jax import lax
from jax.experimental import pallas as pl
from jax.experimental.pallas import tpu as pltpu

F32 = jnp.float32
BF16 = jnp.bfloat16

RMS_EPS = 1e-6
HG_HEADS = 4
HG_DIM = 128
HG_WIDTH = HG_HEADS * HG_DIM
ATT_HEAD_DIM = 64
ATT_WIDTH = 512
ATT_BLOCK = 128
DILATIONS = (1, 4, 16)
ATT_SPAN = 128
NEG_BIG = -1e30
ATT_ITEMS = 32
LOG2_E = 1.4426950408889634
ATT_GROUP = 4
WCAST_PARTS = 16

HG_CHUNK = 128
HG_LEVELS = 7
HG_HALF = HG_CHUNK // 2
HG_SAFE_LOG2_DECAY = -85.0
HG_GROUP = 8
ROW_TILE = 512
FF_CHUNK = 256
PROJ_COLS = 512
VMEM_LIMIT_BYTES = 56 * 1024 * 1024


def _silu(x):
    return x * jax.nn.sigmoid(x)


def _const_spec(shape):
    nd = len(shape)
    return pl.BlockSpec(shape, lambda *_: (0,) * nd, pipeline_mode=pl.Buffered(1))


def _mod_kernel(c_ref, w_ref, b_ref, o_ref):
    ca = _silu(c_ref[...])
    o_ref[...] = jnp.dot(ca, w_ref[...], preferred_element_type=F32) + b_ref[...]


def _mod_call(c, w_ada, b_ada):
    bsz, d = c.shape
    n = w_ada.shape[1]
    tn = 1024
    return pl.pallas_call(
        _mod_kernel,
        grid=(n // tn,),
        in_specs=[
            pl.BlockSpec((bsz, d), lambda j: (0, 0)),
            pl.BlockSpec((d, tn), lambda j: (0, j)),
            pl.BlockSpec((1, tn), lambda j: (0, j)),
        ],
        out_specs=pl.BlockSpec((bsz, tn), lambda j: (0, j)),
        out_shape=jax.ShapeDtypeStruct((bsz, n), F32),
        name="adaln_mod",
    )(c, w_ada, b_ada.reshape(1, n))


def _inproj_kernel(x_ref, sh_ref, sc_ref, g_ref, w_ref, lbl_ref, on_ref,
                   q_o, k_o, lfh_o, lfl_o, v_o, g_o, aq_o, ak_o, av_o, dmin_o):
    x = x_ref[...]
    ms = jnp.mean(x * x, axis=-1, keepdims=True)
    y = x * lax.rsqrt(ms + RMS_EPS) * g_ref[...]
    h = y * (1.0 + sc_ref[0]) + sh_ref[0]
    hb = h.astype(BF16)
    tm = hb.shape[0]

    lbl = lbl_ref[...]
    e = jnp.exp(lbl - jnp.max(lbl, axis=0, keepdims=True))
    lb = e[0:1] / (e[0:1] + e[1:2])

    grp_id = lax.broadcasted_iota(jnp.int32, (tm // HG_HALF, tm), 0)
    row_id = lax.broadcasted_iota(jnp.int32, (tm // HG_HALF, tm), 1)
    member = (row_id // HG_HALF == grp_id).astype(BF16)

    dmin = None
    for lo in range(0, 512, PROJ_COLS):
        cols = slice(lo, lo + PROJ_COLS)

        def proj(j):
            return jnp.dot(hb, w_ref[:, j * 512 + lo:j * 512 + lo + PROJ_COLS],
                           preferred_element_type=F32)

        sg = jax.nn.sigmoid(proj(1))
        lbc = lb[:, cols]
        lf = jnp.log2(lbc + (1.0 - lbc) * sg)
        lf_hi = lf.astype(BF16)
        lfh_o[:, cols] = lf_hi
        lfl_o[:, cols] = (lf - lf_hi.astype(F32)).astype(BF16)
        k_o[:, cols] = ((1.0 - lbc) * (1.0 - sg)).astype(BF16)
        ak_o[:, cols] = proj(5)
        q_o[:, cols] = _silu(proj(0)).astype(BF16)
        av_o[:, cols] = proj(6)
        g_o[:, cols] = (_silu(proj(3)) * on_ref[:, cols]).astype(BF16)
        v_o[:, cols] = proj(2).astype(BF16)
        aq_o[:, cols] = proj(4) * (ATT_HEAD_DIM ** -0.5 * LOG2_E)
        totals = jnp.min(jnp.dot(member, lf_hi, preferred_element_type=F32),
                         axis=-1, keepdims=True)
        dmin = totals if dmin is None else jnp.minimum(dmin, totals)
    dmin_o[0] = jnp.broadcast_to(dmin, dmin_o.shape[1:])


def _inproj_call(x2, mod3, norm_g, w_in_bf, lb_logits, hg_gain, seq):
    t, d = x2.shape
    tm = ROW_TILE
    steps_per_batch = seq // tm
    row = lambda i: (i, 0)
    out_dtypes = (BF16, BF16, BF16, BF16, BF16, BF16, F32, F32, F32)
    return pl.pallas_call(
        _inproj_kernel,
        grid=(t // tm,),
        in_specs=[
            pl.BlockSpec((tm, d), row),
            pl.BlockSpec((1, 1, d), lambda i: (i // steps_per_batch, 0, 0)),
            pl.BlockSpec((1, 1, d), lambda i: (i // steps_per_batch, 0, 1)),
            _const_spec((1, d)),
            _const_spec(w_in_bf.shape),
            _const_spec(lb_logits.shape),
            _const_spec((1, HG_WIDTH)),
        ],
        out_specs=[pl.BlockSpec((tm, 512), row) for _ in out_dtypes]
        + [pl.BlockSpec((1, tm // HG_HALF, 128), lambda i: (i, 0, 0))],
        out_shape=[jax.ShapeDtypeStruct((t, 512), dt) for dt in out_dtypes]
        + [jax.ShapeDtypeStruct((t // tm, tm // HG_HALF, 128), F32)],
        compiler_params=pltpu.CompilerParams(
            dimension_semantics=("arbitrary",), vmem_limit_bytes=VMEM_LIMIT_BYTES),
        name="in_proj",
    )(x2, mod3, mod3, norm_g.reshape(1, d), w_in_bf, lb_logits,
      jnp.tile(hg_gain.reshape(1, HG_DIM), (1, HG_HEADS)))


def _decay_selectors():
    c = HG_CHUNK
    t = np.arange(c)[:, None]
    u = np.arange(c)[None, :]
    blocks = []
    for l in range(HG_LEVELS):
        lo = (t >> l) << l
        hi = lo + (1 << l) - 1
        second = ((t >> l) & 1) == 1
        blocks.append(np.where(second, (u >= lo) & (u <= t), (u > t) & (u <= hi)))
    blocks.append(u <= t)
    blocks.append(u > t)
    return np.concatenate(blocks, axis=0).astype(np.float32)


def _hgrn_kernel(safe_ref, q_ref, k_ref, lfh_ref, lfl_ref, v_ref, g_ref, p_ref, o_ref, st_ref):
    c = HG_CHUNK
    half = HG_HALF
    step = pl.program_id(0) * pl.num_programs(1) + pl.program_id(1)

    @pl.when(pl.program_id(1) == 0)
    def _():
        st_ref[...] = jnp.zeros_like(st_ref)

    def head(a, h):
        return a[:, h * HG_DIM:(h + 1) * HG_DIM]

    def gram(a, b):
        return lax.dot_general(a, b, (((1,), (1,)), ((), ())), preferred_element_type=F32)

    def log_decay(rows):
        return jnp.concatenate([lfh_ref[0, rows, :], lfl_ref[0, rows, :]], axis=0)

    def exponents(lf2, block):
        return jnp.dot(p_ref[block * c:(block + 1) * c, :], lf2, preferred_element_type=F32)

    def scaled(x_bf, log2_scale):
        return x_bf * jnp.exp2(log2_scale).astype(BF16)

    def normed(rows, o, h):
        ms = jnp.mean(o * o, axis=-1, keepdims=True)
        return (o * lax.rsqrt(ms + RMS_EPS)).astype(BF16) * head(g_ref[0, rows, :], h)

    safe = safe_ref[step] > 0

    @pl.when(safe)
    def _():
        top_mask = (lax.broadcasted_iota(jnp.int32, (half, half), 1)
                    <= lax.broadcasted_iota(jnp.int32, (half, half), 0))
        bot_mask = (lax.broadcasted_iota(jnp.int32, (half, c), 1)
                    <= lax.broadcasted_iota(jnp.int32, (half, c), 0) + half)
        heads = range(HG_HEADS)

        def decay(j):
            rows = slice(j * c, (j + 1) * c)
            return rows, exponents(log_decay(rows), HG_LEVELS)

        def scale(st):
            rows, b = st
            b_mid = b[half - 1:half, :]
            q = q_ref[0, rows, :]
            k = k_ref[0, rows, :]
            return dict(rows=rows, b_mid=b_mid, b_last=b[c - 1:c, :], qe=scaled(q, b),
                        k_top=scaled(k[:half], -b[:half]), k_mid=scaled(k, b_mid - b),
                        q_bot=scaled(q[half:], b[half:] - b_mid))

        def score(st):
            st["s_top"] = [jnp.where(top_mask, gram(head(st["qe"], h)[:half],
                                                    head(st["k_top"], h)), 0.0).astype(BF16)
                           for h in heads]
            st["s_bot"] = [jnp.where(bot_mask, gram(head(st["q_bot"], h),
                                                    head(st["k_mid"], h)), 0.0).astype(BF16)
                           for h in heads]
            return st

        def apply(st):
            v = v_ref[0, st["rows"], :]
            st["intra"] = [jnp.concatenate(
                [jnp.dot(st["s_top"][h], head(v, h)[:half], preferred_element_type=F32),
                 jnp.dot(st["s_bot"][h], head(v, h), preferred_element_type=F32)], axis=0)
                for h in heads]
            tail_decay = jnp.exp2(st["b_last"] - st["b_mid"])
            st["kv"] = [lax.dot_general(head(v, h), head(st["k_mid"], h),
                                        (((0,), (0,)), ((), ())), preferred_element_type=F32)
                        * head(tail_decay, h) for h in heads]
            return st

        states = [st_ref[h] for h in heads]

        def recur(st):
            chunk_decay = jnp.exp2(st["b_last"])
            outs = []
            for h in heads:
                o = st["intra"][h] + gram(head(st["qe"], h), states[h].astype(BF16))
                states[h] = states[h] * head(chunk_decay, h) + st["kv"][h]
                outs.append(normed(st["rows"], o, h))
            o_ref[0, st["rows"], :] = jnp.concatenate(outs, axis=1).astype(o_ref.dtype)
            return None

        stages = (scale, score, apply, recur)
        live = [None] * HG_GROUP
        for t in range(HG_GROUP + len(stages)):
            for kk in reversed(range(len(stages) + 1)):
                j = t - kk
                if 0 <= j < HG_GROUP:
                    live[j] = decay(j) if kk == 0 else stages[kk - 1](live[j])
        for h in heads:
            st_ref[h] = states[h]


    @pl.when(jnp.logical_not(safe))
    def _():
        row = lax.broadcasted_iota(jnp.int32, (c, HG_WIDTH), 0)
        ti = lax.broadcasted_iota(jnp.int32, (c, c), 0)
        si = lax.broadcasted_iota(jnp.int32, (c, c), 1)
        txs = ti ^ si

        def one_chunk(j, carry):
            rows = pl.ds(pl.multiple_of(j * c, c), c)
            q = q_ref[0, rows, :]
            k = k_ref[0, rows, :]
            v = v_ref[0, rows, :]
            lf = log_decay(rows)
            scores = [jnp.where(ti == si, gram(head(q, h), head(k, h)), 0.0)
                      for h in range(HG_HEADS)]
            for l in range(HG_LEVELS):
                second = ((row >> l) & 1) == 1
                xl = scaled(jnp.where(second, q, k), exponents(lf, l))
                mask = (ti > si) & ((txs >> l) == 1)
                for h in range(HG_HEADS):
                    xh = head(xl, h)
                    scores[h] = jnp.where(mask, gram(xh, xh), scores[h])
            b = exponents(lf, HG_LEVELS)
            qe = scaled(q, b)
            k_end = scaled(k, exponents(lf, HG_LEVELS + 1))
            chunk_decay = jnp.exp2(b[c - 1:c, :])
            outs = []
            for h in range(HG_HEADS):
                st = st_ref[h]
                o = jnp.dot(scores[h].astype(BF16), head(v, h), preferred_element_type=F32)
                o = o + gram(head(qe, h), st.astype(BF16))
                st_ref[h] = st * head(chunk_decay, h) + lax.dot_general(
                    head(v, h), head(k_end, h), (((0,), (0,)), ((), ())),
                    preferred_element_type=F32)
                outs.append(normed(rows, o, h))
            o_ref[0, rows, :] = jnp.concatenate(outs, axis=1).astype(o_ref.dtype)
            return carry

        lax.fori_loop(0, HG_GROUP, one_chunk, 0)


def _hgrn_call(q, k, lf_hi, lf_lo, v, g, safe):
    bsz, seq, w = q.shape
    rows = HG_GROUP * HG_CHUNK
    sel = _decay_selectors()
    sel = jnp.asarray(np.concatenate([sel, sel], axis=1), dtype=BF16)
    blk = pl.BlockSpec((1, rows, w), lambda b, i, safe_ref: (b, i, 0))
    return pl.pallas_call(
        _hgrn_kernel,
        grid_spec=pltpu.PrefetchScalarGridSpec(
            num_scalar_prefetch=1,
            grid=(bsz, seq // rows),
            in_specs=[blk, blk, blk, blk, blk, blk, _const_spec(sel.shape)],
            out_specs=blk,
            scratch_shapes=[pltpu.VMEM((HG_HEADS, HG_DIM, HG_DIM), F32)],
        ),
        out_shape=jax.ShapeDtypeStruct((bsz, seq, w), BF16),
        compiler_params=pltpu.CompilerParams(
            dimension_semantics=("arbitrary", "arbitrary"), vmem_limit_bytes=VMEM_LIMIT_BYTES),
        name="hgrn2",
    )(safe, q, k, lf_hi, lf_lo, v, g, sel)


def _attn_biases():
    blk, grp = ATT_BLOCK, ATT_GROUP
    slab = blk // grp
    a = np.arange(blk)
    unslab = grp * (a % slab) + a // slab
    full, own = [], []
    for pos in (a, unslab):
        kpos = np.concatenate([pos, pos + blk])
        dist = blk + pos[:, None] - kpos[None, :]
        keep = (dist >= 0) & (dist <= ATT_SPAN)
        full.append(np.where(np.concatenate([keep, keep], axis=0), 0.0, NEG_BIG))
        keep_own = keep[:, blk:]
        own.append(np.where(np.concatenate([keep_own, keep_own], axis=0), 0.0, NEG_BIG))
    return np.stack(full).astype(np.float32), np.stack(own).astype(np.float32)


def _attn_kernel(q_ref, k_ref, v_ref, full_ref, own_ref, wa_ref, wb_ref, wc_ref,
                 o_ref, wa_out, wb_out, wc_out,
                 qd_ref, kd_ref, vd_ref, acc_ref, m_ref, l_ref):
    seq = q_ref.shape[1]
    blk = ATT_BLOCK
    grp = ATT_GROUP
    quarter = seq // grp
    slab = blk // grp

    @pl.when(pl.program_id(0) * pl.num_programs(1) + pl.program_id(1) < WCAST_PARTS)
    def _():
        for src, dst in ((wa_ref, wa_out), (wb_ref, wb_out), (wc_ref, wc_out)):
            dst[...] = src[...].astype(dst.dtype)

    def deinterleave(i, carry):
        base = pl.multiple_of(i * blk, blk)
        for src, dst in ((q_ref, qd_ref), (k_ref, kd_ref), (v_ref, vd_ref)):
            for c in range(grp):
                dst[pl.ds(c * quarter + base, blk), :] = (
                    src[0, pl.ds(c + grp * base, blk, stride=grp), :])
        return carry

    lax.fori_loop(0, quarter // blk, deinterleave, 0)

    first_head = lax.broadcasted_iota(jnp.int32, (blk, 128), 1) < ATT_HEAD_DIM
    stats = (m_ref, l_ref, acc_ref)

    def gather(ref, slices):
        parts = [ref[sl, :] for sl in slices]
        return parts[0] if len(parts) == 1 else jnp.concatenate(parts, axis=0)

    def scatter(ref, slices, val):
        n = val.shape[0] // len(slices)
        for i, sl in enumerate(slices):
            ref[sl, :] = val[i * n:(i + 1) * n]

    def process(items, merge):
        def keys(ref, cur, prev):
            own = gather(ref, cur)
            if prev is not None:
                own = jnp.concatenate([gather(ref, prev), own], axis=0)
            return own.astype(BF16)

        def scores(item):
            cur, prev, bias = item
            qf = gather(qd_ref, cur)
            q2 = jnp.concatenate([jnp.where(first_head, qf, 0.0), jnp.where(first_head, 0.0, qf)],
                                 axis=0).astype(BF16)
            return lax.dot_general(q2, keys(kd_ref, cur, prev), (((1,), (1,)), ((), ())),
                                   preferred_element_type=F32) + bias

        def softmax(s):
            m2 = jnp.max(s, axis=-1, keepdims=True)
            p = jnp.exp2(s - m2)
            return m2, jnp.sum(p, axis=-1, keepdims=True), p.astype(BF16)

        def values(item, state):
            cur, prev, _ = item
            m2, l2, p = state
            return m2, l2, jnp.dot(p, keys(vd_ref, cur, prev), preferred_element_type=F32)

        def commit(item, state):
            cur = item[0]
            new = tuple(jnp.where(first_head, x[:blk], x[blk:]) for x in state)
            if merge:
                m_old, l_old, o_old = (gather(ref, cur) for ref in stats)
                m_new, l_new, o_new = new
                m_tot = jnp.maximum(m_old, m_new)
                a_old = jnp.exp2(m_old - m_tot)
                a_new = jnp.exp2(m_new - m_tot)
                new = (m_tot, l_old * a_old + l_new * a_new, o_old * a_old + o_new * a_new)
            for ref, val in zip(stats, new):
                scatter(ref, cur, val)

        stages = (lambda it, st: scores(it), lambda it, st: softmax(st), values, commit)
        state = [None] * len(items)
        for t in range(len(items) + len(stages) - 1):
            for k in reversed(range(len(stages))):
                i = t - k
                if 0 <= i < len(items):
                    state[i] = stages[k](items[i], state[i])

    def rows4(c, n):
        return [pl.ds(pl.multiple_of(c * quarter + n * blk, blk), blk)]

    per_trip = ATT_ITEMS // grp

    def items4(n0, first):
        return [(rows4(c, n0 + j), None, own_ref[0]) if first and j == 0
                else (rows4(c, n0 + j), rows4(c, n0 + j - 1), full_ref[0])
                for j in range(per_trip) for c in range(grp)]

    process(items4(0, True), merge=False)

    def body4(i, carry):
        process(items4(i * per_trip, False), merge=False)
        return carry

    lax.fori_loop(1, quarter // (blk * per_trip), body4, 0)

    def rows1(n):
        return [pl.ds(pl.multiple_of(c * quarter + n * slab, slab), slab) for c in range(grp)]

    process([(rows1(0), None, own_ref[1])]
            + [(rows1(n), rows1(n - 1), full_ref[1]) for n in range(1, ATT_ITEMS)], merge=True)

    def body1(i, carry):
        n0 = i * ATT_ITEMS
        process([(rows1(n0 + j), rows1(n0 + j - 1), full_ref[1]) for j in range(ATT_ITEMS)],
                merge=True)
        return carry

    lax.fori_loop(1, seq // (blk * ATT_ITEMS), body1, 0)

    sub_blocks = seq // (DILATIONS[-1] * blk)

    def rows16(c, e, n):
        return [pl.ds(c * quarter + e + n * grp * blk, blk, stride=grp)]

    res_per_trip = max(1, ATT_ITEMS // (grp * sub_blocks))

    def body16(i, carry):
        items = []
        for j in range(res_per_trip):
            c = i * res_per_trip + j
            for e in range(grp):
                items.append((rows16(c, e, 0), None, own_ref[0]))
                for n in range(1, sub_blocks):
                    items.append((rows16(c, e, n), rows16(c, e, n - 1), full_ref[0]))
        process(items, merge=True)
        return carry

    lax.fori_loop(0, grp // res_per_trip, body16, 0)

    def finish(i, carry):
        base = pl.multiple_of(i * blk, blk)
        for c in range(grp):
            rows = pl.ds(c * quarter + base, blk)
            m_ref[pl.ds(c + grp * base, blk, stride=grp), :] = acc_ref[rows, :] / l_ref[rows, :]
        return carry

    lax.fori_loop(0, quarter // blk, finish, 0)

    def emit(i, carry):
        rows = pl.ds(pl.multiple_of(i * blk, blk), blk)
        o_ref[0, rows, :] = m_ref[rows, :].astype(o_ref.dtype)
        return carry

    lax.fori_loop(0, seq // blk, emit, 0)


def _attn_call(aq, ak, av, tail_weights):
    bsz, seq, w = aq.shape
    pairs = w // 128
    assert bsz * pairs >= WCAST_PARTS
    full, own = (jnp.asarray(x) for x in _attn_biases())
    blk = pl.BlockSpec((1, seq, 128), lambda b, hp: (b, 0, hp))
    part = lambda b, hp: (jnp.minimum(b * pairs + hp, WCAST_PARTS - 1), 0)
    w_specs = [pl.BlockSpec((wt.shape[0] // WCAST_PARTS, wt.shape[1]), part) for wt in tail_weights]
    return pl.pallas_call(
        _attn_kernel,
        grid=(bsz, pairs),
        in_specs=[blk, blk, blk, _const_spec(full.shape), _const_spec(own.shape)] + w_specs,
        out_specs=[blk] + w_specs,
        out_shape=[jax.ShapeDtypeStruct((bsz, seq, w), BF16)]
        + [jax.ShapeDtypeStruct(wt.shape, BF16) for wt in tail_weights],
        scratch_shapes=[pltpu.VMEM((seq, 128), F32) for _ in range(6)],
        compiler_params=pltpu.CompilerParams(
            dimension_semantics=("arbitrary", "arbitrary"), vmem_limit_bytes=VMEM_LIMIT_BYTES),
        name="dilated_attn",
    )(aq, ak, av, full, own, *tail_weights)


def _tail_kernel(x_ref, hg_ref, att_ref, g1_ref, sh2_ref, sc2_ref, g2_ref,
                 an_ref, n2_ref, fin_ref, wo_ref, wgu_ref, wd_ref, o_ref):
    d_ff = wd_ref.shape[0]
    att = att_ref[...].astype(F32)
    ms = jnp.mean(att * att, axis=-1, keepdims=True)
    att_n = (att * lax.rsqrt(ms + RMS_EPS) * an_ref[...]).astype(BF16)
    mix = jnp.dot(hg_ref[...], wo_ref[:HG_WIDTH, :], preferred_element_type=F32)
    mix = mix + jnp.dot(att_n, wo_ref[HG_WIDTH:, :], preferred_element_type=F32)
    x1 = x_ref[...] + g1_ref[0] * mix

    ms = jnp.mean(x1 * x1, axis=-1, keepdims=True)
    h = x1 * lax.rsqrt(ms + RMS_EPS) * n2_ref[...]
    hb = (h * (1.0 + sc2_ref[0]) + sh2_ref[0]).astype(BF16)
    ffn = jnp.zeros_like(x1)
    for j in range(d_ff // FF_CHUNK):
        lo = j * FF_CHUNK
        a = jnp.dot(hb, wgu_ref[:, lo:lo + FF_CHUNK], preferred_element_type=F32)
        u = jnp.dot(hb, wgu_ref[:, d_ff + lo:d_ff + lo + FF_CHUNK], preferred_element_type=F32)
        act = (_silu(a) * u).astype(BF16)
        ffn = ffn + jnp.dot(act, wd_ref[lo:lo + FF_CHUNK, :], preferred_element_type=F32)
    x2 = x1 + g2_ref[0] * ffn
    ms = jnp.mean(x2 * x2, axis=-1, keepdims=True)
    o_ref[...] = x2 * lax.rsqrt(ms + RMS_EPS) * fin_ref[...]


def _tail_call(x2, hg, att, mod3, att_g, norm2_g, final_g, wo_bf, wgu_bf, wd_bf, seq):
    t, d = x2.shape
    tm = ROW_TILE
    steps_per_batch = seq // tm
    row = lambda i: (i, 0)
    mod_col = lambda col: pl.BlockSpec((1, 1, d), lambda i: (i // steps_per_batch, 0, col))
    return pl.pallas_call(
        _tail_kernel,
        grid=(t // tm,),
        in_specs=[
            pl.BlockSpec((tm, d), row),
            pl.BlockSpec((tm, HG_WIDTH), row),
            pl.BlockSpec((tm, ATT_WIDTH), row),
            mod_col(2), mod_col(3), mod_col(4), mod_col(5),
            _const_spec((1, ATT_WIDTH)), _const_spec((1, d)), _const_spec((1, d)),
            _const_spec(wo_bf.shape), _const_spec(wgu_bf.shape), _const_spec(wd_bf.shape),
        ],
        out_specs=pl.BlockSpec((tm, d), row),
        out_shape=jax.ShapeDtypeStruct((t, d), F32),
        compiler_params=pltpu.CompilerParams(
            dimension_semantics=("arbitrary",), vmem_limit_bytes=VMEM_LIMIT_BYTES),
        name="outproj_ffn",
    )(x2, hg, att, mod3, mod3, mod3, mod3,
      att_g.reshape(1, ATT_WIDTH), norm2_g.reshape(1, d), final_g.reshape(1, d),
      wo_bf, wgu_bf, wd_bf)


def kernel(x, c, w_ada, b_ada, norm1_g, w_in, hg_lb_logits, hg_onorm_g, att_onorm_g,
           w_out, norm2_g, w_gate_up, w_down, final_g):
    bsz, seq, d = x.shape
    assert w_in.shape[0] == 1 and hg_lb_logits.shape[0] == 2, "single-layer block expected"
    assert seq % (DILATIONS[-1] * ATT_BLOCK) == 0 and seq % ROW_TILE == 0
    t = bsz * seq
    x2 = x.reshape(t, d)

    mod = _mod_call(c, w_ada[0], b_ada[0])
    mod3 = mod.reshape(bsz, 1, 6 * d)

    q, k, lf_hi, lf_lo, v, g, aq, ak, av, half_decay = _inproj_call(
        x2, mod3, norm1_g[0], w_in[0].astype(BF16), hg_lb_logits, hg_onorm_g[0], seq)
    safe = (jnp.min(half_decay[:, :, 0].reshape(-1, HG_GROUP * HG_CHUNK // HG_HALF), axis=1)
            >= HG_SAFE_LOG2_DECAY).astype(jnp.int32)

    as_seq = lambda a: a.reshape(bsz, seq, a.shape[-1])
    hg = _hgrn_call(as_seq(q), as_seq(k), as_seq(lf_hi), as_seq(lf_lo), as_seq(v), as_seq(g), safe)
    att, wo_bf, wgu_bf, wd_bf = _attn_call(
        as_seq(aq), as_seq(ak), as_seq(av), (w_out[0], w_gate_up[0], w_down[0]))

    out = _tail_call(
        x2, hg.reshape(t, HG_WIDTH), att.reshape(t, ATT_WIDTH), mod3,
        att_onorm_g[0], norm2_g[0], final_g, wo_bf, wgu_bf, wd_bf, seq)
    return out.reshape(bsz, seq, d)
```

```python
import functools

import numpy as np
import jax
import jax.numpy as jnp
from jax import lax
from jax.experimental import pallas as pl
from jax.experimental.pallas import tpu as pltpu

F32 = jnp.float32
BF16 = jnp.bfloat16

RMS_EPS = 1e-6
HG_HEADS = 4
HG_DIM = 128
HG_WIDTH = HG_HEADS * HG_DIM
ATT_HEAD_DIM = 64
ATT_WIDTH = 512
ATT_BLOCK = 128
DILATIONS = (1, 4, 16)
ATT_SPAN = 128
NEG_BIG = -1e30
ATT_ITEMS = 32
LOG2_E = 1.4426950408889634
ATT_GROUP = 4
WCAST_PARTS = 16

HG_CHUNK = 128
HG_LEVELS = 7
HG_HALF = HG_CHUNK // 2
HG_SAFE_LOG2_DECAY = -85.0
HG_GROUP = 16
ROW_TILE = 512
FF_CHUNK = 256
PROJ_COLS = 512
VMEM_LIMIT_BYTES = 56 * 1024 * 1024


def _silu(x):
    return x * jax.nn.sigmoid(x)


def _const_spec(shape):
    nd = len(shape)
    return pl.BlockSpec(shape, lambda *_: (0,) * nd, pipeline_mode=pl.Buffered(1))


def _mod_kernel(c_ref, w_ref, b_ref, o_ref):
    ca = _silu(c_ref[...])
    o_ref[...] = jnp.dot(ca, w_ref[...], preferred_element_type=F32) + b_ref[...]


def _mod_call(c, w_ada, b_ada):
    bsz, d = c.shape
    n = w_ada.shape[1]
    tn = 1024
    return pl.pallas_call(
        _mod_kernel,
        grid=(n // tn,),
        in_specs=[
            pl.BlockSpec((bsz, d), lambda j: (0, 0)),
            pl.BlockSpec((d, tn), lambda j: (0, j)),
            pl.BlockSpec((1, tn), lambda j: (0, j)),
        ],
        out_specs=pl.BlockSpec((bsz, tn), lambda j: (0, j)),
        out_shape=jax.ShapeDtypeStruct((bsz, n), F32),
        name="adaln_mod",
    )(c, w_ada, b_ada.reshape(1, n))


def _inproj_kernel(x_ref, sh_ref, sc_ref, g_ref, wf_ref, lbl_ref, on_ref,
                   q_o, k_o, lfh_o, lfl_o, v_o, g_o, aq_o, ak_o, av_o, dmin_o, w_ref):
    @pl.when(pl.program_id(0) == 0)
    def _():
        for r in range(0, wf_ref.shape[0], 128):
            w_ref[r:r + 128, :] = wf_ref[r:r + 128, :].astype(BF16)

    x = x_ref[...]
    ms = jnp.mean(x * x, axis=-1, keepdims=True)
    y = x * lax.rsqrt(ms + RMS_EPS) * g_ref[...]
    h = y * (1.0 + sc_ref[0]) + sh_ref[0]
    hb = h.astype(BF16)
    tm = hb.shape[0]

    lbl = lbl_ref[...]
    e = jnp.exp(lbl - jnp.max(lbl, axis=0, keepdims=True))
    lb = e[0:1] / (e[0:1] + e[1:2])

    grp_id = lax.broadcasted_iota(jnp.int32, (tm // HG_HALF, tm), 0)
    row_id = lax.broadcasted_iota(jnp.int32, (tm // HG_HALF, tm), 1)
    member = (row_id // HG_HALF == grp_id).astype(BF16)

    dmin = None
    for lo in range(0, 512, PROJ_COLS):
        cols = slice(lo, lo + PROJ_COLS)

        def proj(j):
            return jnp.dot(hb, w_ref[:, j * 512 + lo:j * 512 + lo + PROJ_COLS],
                           preferred_element_type=F32)

        sg = jax.nn.sigmoid(proj(1))
        lbc = lb[:, cols]
        lf = jnp.log2(lbc + (1.0 - lbc) * sg)
        lf_hi = lf.astype(BF16)
        lfh_o[:, cols] = lf_hi
        lfl_o[:, cols] = (lf - lf_hi.astype(F32)).astype(BF16)
        k_o[:, cols] = ((1.0 - lbc) * (1.0 - sg)).astype(BF16)
        ak_o[:, cols] = proj(5)
        q_o[:, cols] = _silu(proj(0)).astype(BF16)
        av_o[:, cols] = proj(6)
        g_o[:, cols] = (_silu(proj(3)) * on_ref[:, cols]).astype(BF16)
        v_o[:, cols] = proj(2).astype(BF16)
        aq_o[:, cols] = proj(4) * (ATT_HEAD_DIM ** -0.5 * LOG2_E)
        totals = jnp.min(jnp.dot(member, lf_hi, preferred_element_type=F32),
                         axis=-1, keepdims=True)
        dmin = totals if dmin is None else jnp.minimum(dmin, totals)
    dmin_o[0] = jnp.broadcast_to(dmin, dmin_o.shape[1:])


def _inproj_call(x2, mod3, norm_g, w_in, lb_logits, hg_gain, seq):
    t, d = x2.shape
    tm = ROW_TILE
    steps_per_batch = seq // tm
    row = lambda i: (i, 0)
    out_dtypes = (BF16, BF16, BF16, BF16, BF16, BF16, F32, F32, F32)
    return pl.pallas_call(
        _inproj_kernel,
        grid=(t // tm,),
        in_specs=[
            pl.BlockSpec((tm, d), row),
            pl.BlockSpec((1, 1, d), lambda i: (i // steps_per_batch, 0, 0)),
            pl.BlockSpec((1, 1, d), lambda i: (i // steps_per_batch, 0, 1)),
            _const_spec((1, d)),
            _const_spec(w_in.shape),
            _const_spec(lb_logits.shape),
            _const_spec((1, HG_WIDTH)),
        ],
        out_specs=[pl.BlockSpec((tm, 512), row) for _ in out_dtypes]
        + [pl.BlockSpec((1, tm // HG_HALF, 128), lambda i: (i, 0, 0))],
        out_shape=[jax.ShapeDtypeStruct((t, 512), dt) for dt in out_dtypes]
        + [jax.ShapeDtypeStruct((t // tm, tm // HG_HALF, 128), F32)],
        scratch_shapes=[pltpu.VMEM(w_in.shape, BF16)],
        compiler_params=pltpu.CompilerParams(
            dimension_semantics=("arbitrary",), vmem_limit_bytes=VMEM_LIMIT_BYTES),
        name="in_proj",
    )(x2, mod3, mod3, norm_g.reshape(1, d), w_in, lb_logits,
      jnp.tile(hg_gain.reshape(1, HG_DIM), (1, HG_HEADS)))


def _decay_selectors():
    c = HG_CHUNK
    t = np.arange(c)[:, None]
    u = np.arange(c)[None, :]
    blocks = []
    for l in range(HG_LEVELS):
        lo = (t >> l) << l
        hi = lo + (1 << l) - 1
        second = ((t >> l) & 1) == 1
        blocks.append(np.where(second, (u >= lo) & (u <= t), (u > t) & (u <= hi)))
    blocks.append(u <= t)
    blocks.append(u > t)
    return np.concatenate(blocks, axis=0).astype(np.float32)


def _hgrn_kernel(safe_ref, q_ref, k_ref, lfh_ref, lfl_ref, v_ref, g_ref, p_ref, o_ref, st_ref):
    c = HG_CHUNK
    half = HG_HALF
    step = pl.program_id(0) * pl.num_programs(1) + pl.program_id(1)

    @pl.when(pl.program_id(1) == 0)
    def _():
        st_ref[...] = jnp.zeros_like(st_ref)

    def head(a, h):
        return a[:, h * HG_DIM:(h + 1) * HG_DIM]

    def gram(a, b):
        return lax.dot_general(a, b, (((1,), (1,)), ((), ())), preferred_element_type=F32)

    def log_decay(rows):
        return jnp.concatenate([lfh_ref[0, rows, :], lfl_ref[0, rows, :]], axis=0)

    def exponents(lf2, block):
        return jnp.dot(p_ref[block * c:(block + 1) * c, :], lf2, preferred_element_type=F32)

    def scaled(x_bf, log2_scale):
        return x_bf * jnp.exp2(log2_scale).astype(BF16)

    def normed(rows, o, h):
        ms = jnp.mean(o * o, axis=-1, keepdims=True)
        return (o * lax.rsqrt(ms + RMS_EPS)).astype(BF16) * head(g_ref[0, rows, :], h)

    safe = safe_ref[step] > 0

    @pl.when(safe)
    def _():
        top_mask = (lax.broadcasted_iota(jnp.int32, (half, half), 1)
                    <= lax.broadcasted_iota(jnp.int32, (half, half), 0))
        bot_mask = (lax.broadcasted_iota(jnp.int32, (half, c), 1)
                    <= lax.broadcasted_iota(jnp.int32, (half, c), 0) + half)
        heads = range(HG_HEADS)

        def decay(j):
            rows = slice(j * c, (j + 1) * c)
            return rows, exponents(log_decay(rows), HG_LEVELS)

        def scale(st):
            rows, b = st
            b_mid = b[half - 1:half, :]
            q = q_ref[0, rows, :]
            k = k_ref[0, rows, :]
            return dict(rows=rows, b_mid=b_mid, b_last=b[c - 1:c, :], qe=scaled(q, b),
                        k_top=scaled(k[:half], -b[:half]), k_mid=scaled(k, b_mid - b),
                        q_bot=scaled(q[half:], b[half:] - b_mid))

        def score(st):
            st["s_top"] = [jnp.where(top_mask, gram(head(st["qe"], h)[:half],
                                                    head(st["k_top"], h)), 0.0).astype(BF16)
                           for h in heads]
            st["s_bot"] = [jnp.where(bot_mask, gram(head(st["q_bot"], h),
                                                    head(st["k_mid"], h)), 0.0).astype(BF16)
                           for h in heads]
            return st

        def apply(st):
            v = v_ref[0, st["rows"], :]
            st["intra"] = [jnp.concatenate(
                [jnp.dot(st["s_top"][h], head(v, h)[:half], preferred_element_type=F32),
                 jnp.dot(st["s_bot"][h], head(v, h), preferred_element_type=F32)], axis=0)
                for h in heads]
            tail_decay = jnp.exp2(st["b_last"] - st["b_mid"])
            st["kv"] = [lax.dot_general(head(v, h), head(st["k_mid"], h),
                                        (((0,), (0,)), ((), ())), preferred_element_type=F32)
                        * head(tail_decay, h) for h in heads]
            return st

        states = [st_ref[h] for h in heads]

        def recur(st):
            chunk_decay = jnp.exp2(st["b_last"])
            outs = []
            for h in heads:
                o = st["intra"][h] + gram(head(st["qe"], h), states[h].astype(BF16))
                states[h] = states[h] * head(chunk_decay, h) + st["kv"][h]
                outs.append(normed(st["rows"], o, h))
            o_ref[0, st["rows"], :] = jnp.concatenate(outs, axis=1).astype(o_ref.dtype)
            return None

        stages = (scale, score, apply, recur)
        live = [None] * HG_GROUP
        for t in range(HG_GROUP + len(stages)):
            for kk in reversed(range(len(stages) + 1)):
                j = t - kk
                if 0 <= j < HG_GROUP:
                    live[j] = decay(j) if kk == 0 else stages[kk - 1](live[j])
        for h in heads:
            st_ref[h] = states[h]


    @pl.when(jnp.logical_not(safe))
    def _():
        row = lax.broadcasted_iota(jnp.int32, (c, HG_WIDTH), 0)
        ti = lax.broadcasted_iota(jnp.int32, (c, c), 0)
        si = lax.broadcasted_iota(jnp.int32, (c, c), 1)
        txs = ti ^ si

        def one_chunk(j, carry):
            rows = pl.ds(pl.multiple_of(j * c, c), c)
            q = q_ref[0, rows, :]
            k = k_ref[0, rows, :]
            v = v_ref[0, rows, :]
            lf = log_decay(rows)
            scores = [jnp.where(ti == si, gram(head(q, h), head(k, h)), 0.0)
                      for h in range(HG_HEADS)]
            for l in range(HG_LEVELS):
                second = ((row >> l) & 1) == 1
                xl = scaled(jnp.where(second, q, k), exponents(lf, l))
                mask = (ti > si) & ((txs >> l) == 1)
                for h in range(HG_HEADS):
                    xh = head(xl, h)
                    scores[h] = jnp.where(mask, gram(xh, xh), scores[h])
            b = exponents(lf, HG_LEVELS)
            qe = scaled(q, b)
            k_end = scaled(k, exponents(lf, HG_LEVELS + 1))
            chunk_decay = jnp.exp2(b[c - 1:c, :])
            outs = []
            for h in range(HG_HEADS):
                st = st_ref[h]
                o = jnp.dot(scores[h].astype(BF16), head(v, h), preferred_element_type=F32)
                o = o + gram(head(qe, h), st.astype(BF16))
                st_ref[h] = st * head(chunk_decay, h) + lax.dot_general(
                    head(v, h), head(k_end, h), (((0,), (0,)), ((), ())),
                    preferred_element_type=F32)
                outs.append(normed(rows, o, h))
            o_ref[0, rows, :] = jnp.concatenate(outs, axis=1).astype(o_ref.dtype)
            return carry

        lax.fori_loop(0, HG_GROUP, one_chunk, 0)


def _hgrn_call(q, k, lf_hi, lf_lo, v, g, safe):
    bsz, seq, w = q.shape
    rows = HG_GROUP * HG_CHUNK
    sel = _decay_selectors()
    sel = jnp.asarray(np.concatenate([sel, sel], axis=1), dtype=BF16)
    blk = pl.BlockSpec((1, rows, w), lambda b, i, safe_ref: (b, i, 0))
    return pl.pallas_call(
        _hgrn_kernel,
        grid_spec=pltpu.PrefetchScalarGridSpec(
            num_scalar_prefetch=1,
            grid=(bsz, seq // rows),
            in_specs=[blk, blk, blk, blk, blk, blk, _const_spec(sel.shape)],
            out_specs=blk,
            scratch_shapes=[pltpu.VMEM((HG_HEADS, HG_DIM, HG_DIM), F32)],
        ),
        out_shape=jax.ShapeDtypeStruct((bsz, seq, w), BF16),
        compiler_params=pltpu.CompilerParams(
            dimension_semantics=("arbitrary", "arbitrary"), vmem_limit_bytes=VMEM_LIMIT_BYTES),
        name="hgrn2",
    )(safe, q, k, lf_hi, lf_lo, v, g, sel)


def _attn_biases():
    blk, grp = ATT_BLOCK, ATT_GROUP
    slab = blk // grp
    a = np.arange(blk)
    unslab = grp * (a % slab) + a // slab
    full, own = [], []
    for pos in (a, unslab):
        kpos = np.concatenate([pos, pos + blk])
        dist = blk + pos[:, None] - kpos[None, :]
        keep = (dist >= 0) & (dist <= ATT_SPAN)
        full.append(np.where(np.concatenate([keep, keep], axis=0), 0.0, NEG_BIG))
        keep_own = keep[:, blk:]
        own.append(np.where(np.concatenate([keep_own, keep_own], axis=0), 0.0, NEG_BIG))
    return np.stack(full).astype(np.float32), np.stack(own).astype(np.float32)


def _attn_kernel(q_ref, k_ref, v_ref, full_ref, own_ref, wa_ref, wb_ref, wc_ref,
                 o_ref, wa_out, wb_out, wc_out,
                 qd_ref, kd_ref, vd_ref, acc_ref, m_ref, l_ref):
    seq = q_ref.shape[1]
    blk = ATT_BLOCK
    grp = ATT_GROUP
    quarter = seq // grp
    slab = blk // grp

    @pl.when(pl.program_id(0) * pl.num_programs(1) + pl.program_id(1) < WCAST_PARTS)
    def _():
        for src, dst in ((wa_ref, wa_out), (wb_ref, wb_out), (wc_ref, wc_out)):
            dst[...] = src[...].astype(dst.dtype)

    def deinterleave(i, carry):
        base = pl.multiple_of(i * blk, blk)
        for src, dst in ((q_ref, qd_ref), (k_ref, kd_ref), (v_ref, vd_ref)):
            for c in range(grp):
                dst[pl.ds(c * quarter + base, blk), :] = (
                    src[0, pl.ds(c + grp * base, blk, stride=grp), :])
        return carry

    lax.fori_loop(0, quarter // blk, deinterleave, 0)

    first_head = lax.broadcasted_iota(jnp.int32, (blk, 128), 1) < ATT_HEAD_DIM
    stats = (m_ref, l_ref, acc_ref)

    def gather(ref, slices):
        parts = [ref[sl, :] for sl in slices]
        return parts[0] if len(parts) == 1 else jnp.concatenate(parts, axis=0)

    def scatter(ref, slices, val):
        n = val.shape[0] // len(slices)
        for i, sl in enumerate(slices):
            ref[sl, :] = val[i * n:(i + 1) * n]

    def process(items, merge):
        def keys(ref, cur, prev):
            own = gather(ref, cur)
            if prev is not None:
                own = jnp.concatenate([gather(ref, prev), own], axis=0)
            return own.astype(BF16)

        def scores(item):
            cur, prev, bias = item
            qf = gather(qd_ref, cur)
            q2 = jnp.concatenate([jnp.where(first_head, qf, 0.0), jnp.where(first_head, 0.0, qf)],
                                 axis=0).astype(BF16)
            return lax.dot_general(q2, keys(kd_ref, cur, prev), (((1,), (1,)), ((), ())),
                                   preferred_element_type=F32) + bias

        def softmax(s):
            m2 = jnp.max(s, axis=-1, keepdims=True)
            p = jnp.exp2(s - m2)
            return m2, jnp.sum(p, axis=-1, keepdims=True), p.astype(BF16)

        def values(item, state):
            cur, prev, _ = item
            m2, l2, p = state
            return m2, l2, jnp.dot(p, keys(vd_ref, cur, prev), preferred_element_type=F32)

        def commit(item, state):
            cur = item[0]
            new = tuple(jnp.where(first_head, x[:blk], x[blk:]) for x in state)
            if merge:
                m_old, l_old, o_old = (gather(ref, cur) for ref in stats)
                m_new, l_new, o_new = new
                m_tot = jnp.maximum(m_old, m_new)
                a_old = jnp.exp2(m_old - m_tot)
                a_new = jnp.exp2(m_new - m_tot)
                new = (m_tot, l_old * a_old + l_new * a_new, o_old * a_old + o_new * a_new)
            for ref, val in zip(stats, new):
                scatter(ref, cur, val)

        stages = (lambda it, st: scores(it), lambda it, st: softmax(st), values, commit)
        state = [None] * len(items)
        for t in range(len(items) + len(stages) - 1):
            for k in reversed(range(len(stages))):
                i = t - k
                if 0 <= i < len(items):
                    state[i] = stages[k](items[i], state[i])

    def rows4(c, n):
        return [pl.ds(pl.multiple_of(c * quarter + n * blk, blk), blk)]

    per_trip = ATT_ITEMS // grp

    def items4(n0, first):
        return [(rows4(c, n0 + j), None, own_ref[0]) if first and j == 0
                else (rows4(c, n0 + j), rows4(c, n0 + j - 1), full_ref[0])
                for j in range(per_trip) for c in range(grp)]

    process(items4(0, True), merge=False)

    def body4(i, carry):
        process(items4(i * per_trip, False), merge=False)
        return carry

    lax.fori_loop(1, quarter // (blk * per_trip), body4, 0)

    def rows1(n):
        return [pl.ds(pl.multiple_of(c * quarter + n * slab, slab), slab) for c in range(grp)]

    process([(rows1(0), None, own_ref[1])]
            + [(rows1(n), rows1(n - 1), full_ref[1]) for n in range(1, ATT_ITEMS)], merge=True)

    def body1(i, carry):
        n0 = i * ATT_ITEMS
        process([(rows1(n0 + j), rows1(n0 + j - 1), full_ref[1]) for j in range(ATT_ITEMS)],
                merge=True)
        return carry

    lax.fori_loop(1, seq // (blk * ATT_ITEMS), body1, 0)

    sub_blocks = seq // (DILATIONS[-1] * blk)

    def rows16(c, e, n):
        return [pl.ds(c * quarter + e + n * grp * blk, blk, stride=grp)]

    res_per_trip = max(1, ATT_ITEMS // (grp * sub_blocks))

    def body16(i, carry):
        items = []
        for j in range(res_per_trip):
            c = i * res_per_trip + j
            for e in range(grp):
                items.append((rows16(c, e, 0), None, own_ref[0]))
                for n in range(1, sub_blocks):
                    items.append((rows16(c, e, n), rows16(c, e, n - 1), full_ref[0]))
        process(items, merge=True)
        return carry

    lax.fori_loop(0, grp // res_per_trip, body16, 0)

    def finish(i, carry):
        base = pl.multiple_of(i * blk, blk)
        for c in range(grp):
            rows = pl.ds(c * quarter + base, blk)
            m_ref[pl.ds(c + grp * base, blk, stride=grp), :] = acc_ref[rows, :] / l_ref[rows, :]
        return carry

    lax.fori_loop(0, quarter // blk, finish, 0)

    def emit(i, carry):
        rows = pl.ds(pl.multiple_of(i * blk, blk), blk)
        o_ref[0, rows, :] = m_ref[rows, :].astype(o_ref.dtype)
        return carry

    lax.fori_loop(0, seq // blk, emit, 0)


def _attn_call(aq, ak, av, tail_weights):
    bsz, seq, w = aq.shape
    pairs = w // 128
    assert bsz * pairs >= WCAST_PARTS
    full, own = (jnp.asarray(x) for x in _attn_biases())
    blk = pl.BlockSpec((1, seq, 128), lambda b, hp: (b, 0, hp))
    part = lambda b, hp: (jnp.minimum(b * pairs + hp, WCAST_PARTS - 1), 0)
    w_specs = [pl.BlockSpec((wt.shape[0] // WCAST_PARTS, wt.shape[1]), part) for wt in tail_weights]
    return pl.pallas_call(
        _attn_kernel,
        grid=(bsz, pairs),
        in_specs=[blk, blk, blk, _const_spec(full.shape), _const_spec(own.shape)] + w_specs,
        out_specs=[blk] + w_specs,
        out_shape=[jax.ShapeDtypeStruct((bsz, seq, w), BF16)]
        + [jax.ShapeDtypeStruct(wt.shape, BF16) for wt in tail_weights],
        scratch_shapes=[pltpu.VMEM((seq, 128), F32) for _ in range(6)],
        compiler_params=pltpu.CompilerParams(
            dimension_semantics=("arbitrary", "arbitrary"), vmem_limit_bytes=VMEM_LIMIT_BYTES),
        name="dilated_attn",
    )(aq, ak, av, full, own, *tail_weights)


def _tail_kernel(x_ref, hg_ref, att_ref, g1_ref, sh2_ref, sc2_ref, g2_ref,
                 an_ref, n2_ref, fin_ref, wo_ref, wgu_ref, wd_ref, o_ref):
    d_ff = wd_ref.shape[0]
    att = att_ref[...].astype(F32)
    ms = jnp.mean(att * att, axis=-1, keepdims=True)
    att_n = (att * lax.rsqrt(ms + RMS_EPS) * an_ref[...]).astype(BF16)
    mix = jnp.dot(hg_ref[...], wo_ref[:HG_WIDTH, :], preferred_element_type=F32)
    mix = mix + jnp.dot(att_n, wo_ref[HG_WIDTH:, :], preferred_element_type=F32)
    x1 = x_ref[...] + g1_ref[0] * mix

    ms = jnp.mean(x1 * x1, axis=-1, keepdims=True)
    h = x1 * lax.rsqrt(ms + RMS_EPS) * n2_ref[...]
    hb = (h * (1.0 + sc2_ref[0]) + sh2_ref[0]).astype(BF16)
    ffn = jnp.zeros_like(x1)
    for j in range(d_ff // FF_CHUNK):
        lo = j * FF_CHUNK
        a = jnp.dot(hb, wgu_ref[:, lo:lo + FF_CHUNK], preferred_element_type=F32)
        u = jnp.dot(hb, wgu_ref[:, d_ff + lo:d_ff + lo + FF_CHUNK], preferred_element_type=F32)
        act = (_silu(a) * u).astype(BF16)
        ffn = ffn + jnp.dot(act, wd_ref[lo:lo + FF_CHUNK, :], preferred_element_type=F32)
    x2 = x1 + g2_ref[0] * ffn
    ms = jnp.mean(x2 * x2, axis=-1, keepdims=True)
    o_ref[...] = x2 * lax.rsqrt(ms + RMS_EPS) * fin_ref[...]


def _tail_call(x2, hg, att, mod3, att_g, norm2_g, final_g, wo_bf, wgu_bf, wd_bf, seq):
    t, d = x2.shape
    tm = ROW_TILE
    steps_per_batch = seq // tm
    row = lambda i: (i, 0)
    mod_col = lambda col: pl.BlockSpec((1, 1, d), lambda i: (i // steps_per_batch, 0, col))
    return pl.pallas_call(
        _tail_kernel,
        grid=(t // tm,),
        in_specs=[
            pl.BlockSpec((tm, d), row),
            pl.BlockSpec((tm, HG_WIDTH), row),
            pl.BlockSpec((tm, ATT_WIDTH), row),
            mod_col(2), mod_col(3), mod_col(4), mod_col(5),
            _const_spec((1, ATT_WIDTH)), _const_spec((1, d)), _const_spec((1, d)),
            _const_spec(wo_bf.shape), _const_spec(wgu_bf.shape), _const_spec(wd_bf.shape),
        ],
        out_specs=pl.BlockSpec((tm, d), row),
        out_shape=jax.ShapeDtypeStruct((t, d), F32),
        compiler_params=pltpu.CompilerParams(
            dimension_semantics=("arbitrary",), vmem_limit_bytes=VMEM_LIMIT_BYTES),
        name="outproj_ffn",
    )(x2, hg, att, mod3, mod3, mod3, mod3,
      att_g.reshape(1, ATT_WIDTH), norm2_g.reshape(1, d), final_g.reshape(1, d),
      wo_bf, wgu_bf, wd_bf)


def kernel(x, c, w_ada, b_ada, norm1_g, w_in, hg_lb_logits, hg_onorm_g, att_onorm_g,
           w_out, norm2_g, w_gate_up, w_down, final_g):
    bsz, seq, d = x.shape
    assert w_in.shape[0] == 1 and hg_lb_logits.shape[0] == 2, "single-layer block expected"
    assert seq % (DILATIONS[-1] * ATT_BLOCK) == 0 and seq % ROW_TILE == 0
    t = bsz * seq
    x2 = x.reshape(t, d)

    mod = _mod_call(c, w_ada[0], b_ada[0])
    mod3 = mod.reshape(bsz, 1, 6 * d)

    q, k, lf_hi, lf_lo, v, g, aq, ak, av, half_decay = _inproj_call(
        x2, mod3, norm1_g[0], w_in[0], hg_lb_logits, hg_onorm_g[0], seq)
    safe = (jnp.min(half_decay[:, :, 0].reshape(-1, HG_GROUP * HG_CHUNK // HG_HALF), axis=1)
            >= HG_SAFE_LOG2_DECAY).astype(jnp.int32)

    as_seq = lambda a: a.reshape(bsz, seq, a.shape[-1])
    hg = _hgrn_call(as_seq(q), as_seq(k), as_seq(lf_hi), as_seq(lf_lo), as_seq(v), as_seq(g), safe)
    att, wo_bf, wgu_bf, wd_bf = _attn_call(
        as_seq(aq), as_seq(ak), as_seq(av), (w_out[0], w_gate_up[0], w_down[0]))

    out = _tail_call(
        x2, hg.reshape(t, HG_WIDTH), att.reshape(t, ATT_WIDTH), mod3,
        att_onorm_g[0], norm2_g[0], final_g, wo_bf, wgu_bf, wd_bf, seq)
    return out.reshape(bsz, seq, d)
```

```python
import numpy as np
import jax
import jax.numpy as jnp
from jax import lax
from jax.experimental import pallas as pl
from jax.experimental.pallas import tpu as pltpu

F32 = jnp.float32
BF16 = jnp.bfloat16

LANES = 128
RMS_EPS = 1e-6
LOG2_E = 1.4426950408889634
PROJ_WIDTH = 512

HG_HEADS = 4
HG_DIM = 128
HG_WIDTH = HG_HEADS * HG_DIM
HG_CHUNK = 128
HG_LEVELS = 7
HG_HALF = HG_CHUNK // 2
HG_SAFE_LOG2_DECAY = -85.0
HG_GROUP = 16

ATT_HEAD_DIM = 64
ATT_WIDTH = 512
ATT_BLOCK = 128
DILATIONS = (1, 4, 16)
ATT_SPAN = 128
NEG_BIG = -1e30
ATT_ITEMS = 32
ATT_GROUP = 4
WCAST_PARTS = 16

ROW_TILE = 512
FF_CHUNK = 256
WCAST_ROWS = 128
VMEM_LIMIT_BYTES = 56 * 1024 * 1024


def _silu(x):
    return x * jax.nn.sigmoid(x)


def _const_spec(shape):
    nd = len(shape)
    return pl.BlockSpec(shape, lambda *_: (0,) * nd, pipeline_mode=pl.Buffered(1))


def _mod_kernel(c_ref, w_ref, b_ref, o_ref):
    ca = _silu(c_ref[...])
    o_ref[...] = jnp.dot(ca, w_ref[...], preferred_element_type=F32) + b_ref[...]


def _mod_call(c, w_ada, b_ada):
    bsz, d = c.shape
    n = w_ada.shape[1]
    tn = d
    return pl.pallas_call(
        _mod_kernel,
        grid=(n // tn,),
        in_specs=[
            pl.BlockSpec((bsz, d), lambda j: (0, 0)),
            pl.BlockSpec((d, tn), lambda j: (0, j)),
            pl.BlockSpec((1, tn), lambda j: (0, j)),
        ],
        out_specs=pl.BlockSpec((bsz, tn), lambda j: (0, j)),
        out_shape=jax.ShapeDtypeStruct((bsz, n), F32),
        name="adaln_mod",
    )(c, w_ada, b_ada.reshape(1, n))


def _inproj_kernel(x_ref, sh_ref, sc_ref, g_ref, wf_ref, lbl_ref, on_ref,
                   q_o, k_o, lfh_o, lfl_o, v_o, g_o, aq_o, ak_o, av_o, dmin_o, w_ref):
    @pl.when(pl.program_id(0) == 0)
    def _():
        for r in range(0, wf_ref.shape[0], WCAST_ROWS):
            w_ref[r:r + WCAST_ROWS, :] = wf_ref[r:r + WCAST_ROWS, :].astype(BF16)

    x = x_ref[...]
    ms = jnp.mean(x * x, axis=-1, keepdims=True)
    y = x * lax.rsqrt(ms + RMS_EPS) * g_ref[...]
    h = y * (1.0 + sc_ref[0]) + sh_ref[0]
    hb = h.astype(BF16)
    tm = hb.shape[0]

    def proj(j):
        return jnp.dot(hb, w_ref[:, j * PROJ_WIDTH:(j + 1) * PROJ_WIDTH],
                       preferred_element_type=F32)

    lbl = lbl_ref[...]
    e = jnp.exp(lbl - jnp.max(lbl, axis=0, keepdims=True))
    lb = e[0:1] / (e[0:1] + e[1:2])

    sg = jax.nn.sigmoid(proj(1))
    lf = jnp.log2(lb + (1.0 - lb) * sg)
    lf_hi = lf.astype(BF16)
    lfh_o[...] = lf_hi
    lfl_o[...] = (lf - lf_hi.astype(F32)).astype(BF16)
    k_o[...] = ((1.0 - lb) * (1.0 - sg)).astype(BF16)
    ak_o[...] = proj(5)
    q_o[...] = _silu(proj(0)).astype(BF16)
    av_o[...] = proj(6)
    g_o[...] = (_silu(proj(3)) * on_ref[...]).astype(BF16)
    v_o[...] = proj(2).astype(BF16)
    aq_o[...] = proj(4) * (ATT_HEAD_DIM ** -0.5 * LOG2_E)
    grp_id = lax.broadcasted_iota(jnp.int32, (tm // HG_HALF, tm), 0)
    row_id = lax.broadcasted_iota(jnp.int32, (tm // HG_HALF, tm), 1)
    member = (row_id // HG_HALF == grp_id).astype(BF16)
    totals = jnp.dot(member, lf_hi, preferred_element_type=F32)
    dmin_o[0] = jnp.broadcast_to(jnp.min(totals, axis=-1, keepdims=True), dmin_o.shape[1:])


def _inproj_call(x2, mod3, norm_g, w_in, lb_logits, hg_gain, seq):
    t, d = x2.shape
    tm = ROW_TILE
    steps_per_batch = seq // tm
    row = lambda i: (i, 0)
    out_dtypes = (BF16, BF16, BF16, BF16, BF16, BF16, F32, F32, F32)
    return pl.pallas_call(
        _inproj_kernel,
        grid=(t // tm,),
        in_specs=[
            pl.BlockSpec((tm, d), row),
            pl.BlockSpec((1, 1, d), lambda i: (i // steps_per_batch, 0, 0)),
            pl.BlockSpec((1, 1, d), lambda i: (i // steps_per_batch, 0, 1)),
            _const_spec((1, d)),
            _const_spec(w_in.shape),
            _const_spec(lb_logits.shape),
            _const_spec((1, HG_WIDTH)),
        ],
        out_specs=[pl.BlockSpec((tm, PROJ_WIDTH), row) for _ in out_dtypes]
        + [pl.BlockSpec((1, tm // HG_HALF, LANES), lambda i: (i, 0, 0))],
        out_shape=[jax.ShapeDtypeStruct((t, PROJ_WIDTH), dt) for dt in out_dtypes]
        + [jax.ShapeDtypeStruct((t // tm, tm // HG_HALF, LANES), F32)],
        scratch_shapes=[pltpu.VMEM(w_in.shape, BF16)],
        compiler_params=pltpu.CompilerParams(
            dimension_semantics=("arbitrary",), vmem_limit_bytes=VMEM_LIMIT_BYTES),
        name="in_proj",
    )(x2, mod3, mod3, norm_g.reshape(1, d), w_in, lb_logits,
      jnp.tile(hg_gain.reshape(1, HG_DIM), (1, HG_HEADS)))


def _decay_selectors():
    c = HG_CHUNK
    t = np.arange(c)[:, None]
    u = np.arange(c)[None, :]
    blocks = []
    for l in range(HG_LEVELS):
        lo = (t >> l) << l
        hi = lo + (1 << l) - 1
        second = ((t >> l) & 1) == 1
        blocks.append(np.where(second, (u >= lo) & (u <= t), (u > t) & (u <= hi)))
    blocks.append(u <= t)
    blocks.append(u > t)
    return np.concatenate(blocks, axis=0).astype(np.float32)


def _hgrn_kernel(safe_ref, q_ref, k_ref, lfh_ref, lfl_ref, v_ref, g_ref, p_ref, o_ref, st_ref):
    c = HG_CHUNK
    half = HG_HALF
    step = pl.program_id(0) * pl.num_programs(1) + pl.program_id(1)

    @pl.when(pl.program_id(1) == 0)
    def _():
        st_ref[...] = jnp.zeros_like(st_ref)

    def head(a, h):
        return a[:, h * HG_DIM:(h + 1) * HG_DIM]

    def gram(a, b):
        return lax.dot_general(a, b, (((1,), (1,)), ((), ())), preferred_element_type=F32)

    def log_decay(rows):
        return jnp.concatenate([lfh_ref[0, rows, :], lfl_ref[0, rows, :]], axis=0)

    def exponents(lf2, block):
        return jnp.dot(p_ref[block * c:(block + 1) * c, :], lf2, preferred_element_type=F32)

    def scaled(x_bf, log2_scale):
        return x_bf * jnp.exp2(log2_scale).astype(BF16)

    def normed(rows, o, h):
        ms = jnp.mean(o * o, axis=-1, keepdims=True)
        return (o * lax.rsqrt(ms + RMS_EPS)).astype(BF16) * head(g_ref[0, rows, :], h)

    safe = safe_ref[step] > 0

    @pl.when(safe)
    def _():
        top_mask = (lax.broadcasted_iota(jnp.int32, (half, half), 1)
                    <= lax.broadcasted_iota(jnp.int32, (half, half), 0))
        bot_mask = (lax.broadcasted_iota(jnp.int32, (half, c), 1)
                    <= lax.broadcasted_iota(jnp.int32, (half, c), 0) + half)
        heads = range(HG_HEADS)

        def decay(j):
            rows = slice(j * c, (j + 1) * c)
            return rows, exponents(log_decay(rows), HG_LEVELS)

        def scale(st):
            rows, b = st
            b_mid = b[half - 1:half, :]
            q = q_ref[0, rows, :]
            k = k_ref[0, rows, :]
            return dict(rows=rows, b_mid=b_mid, b_last=b[c - 1:c, :], qe=scaled(q, b),
                        k_top=scaled(k[:half], -b[:half]), k_mid=scaled(k, b_mid - b),
                        q_bot=scaled(q[half:], b[half:] - b_mid))

        def score(st):
            st["s_top"] = [jnp.where(top_mask, gram(head(st["qe"], h)[:half],
                                                    head(st["k_top"], h)), 0.0).astype(BF16)
                           for h in heads]
            st["s_bot"] = [jnp.where(bot_mask, gram(head(st["q_bot"], h),
                                                    head(st["k_mid"], h)), 0.0).astype(BF16)
                           for h in heads]
            return st

        def apply(st):
            v = v_ref[0, st["rows"], :]
            st["intra"] = [jnp.concatenate(
                [jnp.dot(st["s_top"][h], head(v, h)[:half], preferred_element_type=F32),
                 jnp.dot(st["s_bot"][h], head(v, h), preferred_element_type=F32)], axis=0)
                for h in heads]
            tail_decay = jnp.exp2(st["b_last"] - st["b_mid"])
            st["kv"] = [lax.dot_general(head(v, h), head(st["k_mid"], h),
                                        (((0,), (0,)), ((), ())), preferred_element_type=F32)
                        * head(tail_decay, h) for h in heads]
            return st

        states = [st_ref[h] for h in heads]

        def recur(st):
            chunk_decay = jnp.exp2(st["b_last"])
            outs = []
            for h in heads:
                o = st["intra"][h] + gram(head(st["qe"], h), states[h].astype(BF16))
                states[h] = states[h] * head(chunk_decay, h) + st["kv"][h]
                outs.append(normed(st["rows"], o, h))
            o_ref[0, st["rows"], :] = jnp.concatenate(outs, axis=1).astype(o_ref.dtype)
            return None

        stages = (scale, score, apply, recur)
        live = [None] * HG_GROUP
        for t in range(HG_GROUP + len(stages)):
            for kk in reversed(range(len(stages) + 1)):
                j = t - kk
                if 0 <= j < HG_GROUP:
                    live[j] = decay(j) if kk == 0 else stages[kk - 1](live[j])
        for h in heads:
            st_ref[h] = states[h]


    @pl.when(jnp.logical_not(safe))
    def _():
        row = lax.broadcasted_iota(jnp.int32, (c, HG_WIDTH), 0)
        ti = lax.broadcasted_iota(jnp.int32, (c, c), 0)
        si = lax.broadcasted_iota(jnp.int32, (c, c), 1)
        txs = ti ^ si

        def one_chunk(j, carry):
            rows = pl.ds(pl.multiple_of(j * c, c), c)
            q = q_ref[0, rows, :]
            k = k_ref[0, rows, :]
            v = v_ref[0, rows, :]
            lf = log_decay(rows)
            scores = [jnp.where(ti == si, gram(head(q, h), head(k, h)), 0.0)
                      for h in range(HG_HEADS)]
            for l in range(HG_LEVELS):
                second = ((row >> l) & 1) == 1
                xl = scaled(jnp.where(second, q, k), exponents(lf, l))
                mask = (ti > si) & ((txs >> l) == 1)
                for h in range(HG_HEADS):
                    xh = head(xl, h)
                    scores[h] = jnp.where(mask, gram(xh, xh), scores[h])
            b = exponents(lf, HG_LEVELS)
            qe = scaled(q, b)
            k_end = scaled(k, exponents(lf, HG_LEVELS + 1))
            chunk_decay = jnp.exp2(b[c - 1:c, :])
            outs = []
            for h in range(HG_HEADS):
                st = st_ref[h]
                o = jnp.dot(scores[h].astype(BF16), head(v, h), preferred_element_type=F32)
                o = o + gram(head(qe, h), st.astype(BF16))
                st_ref[h] = st * head(chunk_decay, h) + lax.dot_general(
                    head(v, h), head(k_end, h), (((0,), (0,)), ((), ())),
                    preferred_element_type=F32)
                outs.append(normed(rows, o, h))
            o_ref[0, rows, :] = jnp.concatenate(outs, axis=1).astype(o_ref.dtype)
            return carry

        lax.fori_loop(0, HG_GROUP, one_chunk, 0)


def _hgrn_call(q, k, lf_hi, lf_lo, v, g, safe):
    bsz, seq, w = q.shape
    rows = HG_GROUP * HG_CHUNK
    sel = _decay_selectors()
    sel = jnp.asarray(np.concatenate([sel, sel], axis=1), dtype=BF16)
    blk = pl.BlockSpec((1, rows, w), lambda b, i, safe_ref: (b, i, 0))
    return pl.pallas_call(
        _hgrn_kernel,
        grid_spec=pltpu.PrefetchScalarGridSpec(
            num_scalar_prefetch=1,
            grid=(bsz, seq // rows),
            in_specs=[blk, blk, blk, blk, blk, blk, _const_spec(sel.shape)],
            out_specs=blk,
            scratch_shapes=[pltpu.VMEM((HG_HEADS, HG_DIM, HG_DIM), F32)],
        ),
        out_shape=jax.ShapeDtypeStruct((bsz, seq, w), BF16),
        compiler_params=pltpu.CompilerParams(
            dimension_semantics=("arbitrary", "arbitrary"), vmem_limit_bytes=VMEM_LIMIT_BYTES),
        name="hgrn2",
    )(safe, q, k, lf_hi, lf_lo, v, g, sel)


def _attn_biases():
    blk, grp = ATT_BLOCK, ATT_GROUP
    slab = blk // grp
    a = np.arange(blk)
    unslab = grp * (a % slab) + a // slab
    full, own = [], []
    for pos in (a, unslab):
        kpos = np.concatenate([pos, pos + blk])
        dist = blk + pos[:, None] - kpos[None, :]
        keep = (dist >= 0) & (dist <= ATT_SPAN)
        full.append(np.where(np.concatenate([keep, keep], axis=0), 0.0, NEG_BIG))
        keep_own = keep[:, blk:]
        own.append(np.where(np.concatenate([keep_own, keep_own], axis=0), 0.0, NEG_BIG))
    return np.stack(full).astype(np.float32), np.stack(own).astype(np.float32)


def _attn_kernel(q_ref, k_ref, v_ref, full_ref, own_ref, wa_ref, wb_ref, wc_ref,
                 o_ref, wa_out, wb_out, wc_out,
                 qd_ref, kd_ref, vd_ref, acc_ref, m_ref, l_ref):
    seq = q_ref.shape[1]
    blk = ATT_BLOCK
    grp = ATT_GROUP
    quarter = seq // grp
    slab = blk // grp

    @pl.when(pl.program_id(0) * pl.num_programs(1) + pl.program_id(1) < WCAST_PARTS)
    def _():
        for src, dst in ((wa_ref, wa_out), (wb_ref, wb_out), (wc_ref, wc_out)):
            dst[...] = src[...].astype(dst.dtype)

    def deinterleave(i, carry):
        base = pl.multiple_of(i * blk, blk)
        for src, dst in ((q_ref, qd_ref), (k_ref, kd_ref), (v_ref, vd_ref)):
            for c in range(grp):
                dst[pl.ds(c * quarter + base, blk), :] = (
                    src[0, pl.ds(c + grp * base, blk, stride=grp), :])
        return carry

    lax.fori_loop(0, quarter // blk, deinterleave, 0)

    first_head = lax.broadcasted_iota(jnp.int32, (blk, LANES), 1) < ATT_HEAD_DIM
    stats = (m_ref, l_ref, acc_ref)

    def gather(ref, slices):
        parts = [ref[sl, :] for sl in slices]
        return parts[0] if len(parts) == 1 else jnp.concatenate(parts, axis=0)

    def scatter(ref, slices, val):
        n = val.shape[0] // len(slices)
        for i, sl in enumerate(slices):
            ref[sl, :] = val[i * n:(i + 1) * n]

    def process(items, merge):
        def keys(ref, cur, prev):
            own = gather(ref, cur)
            if prev is not None:
                own = jnp.concatenate([gather(ref, prev), own], axis=0)
            return own.astype(BF16)

        def scores(item):
            cur, prev, bias = item
            qf = gather(qd_ref, cur)
            q2 = jnp.concatenate([jnp.where(first_head, qf, 0.0), jnp.where(first_head, 0.0, qf)],
                                 axis=0).astype(BF16)
            return lax.dot_general(q2, keys(kd_ref, cur, prev), (((1,), (1,)), ((), ())),
                                   preferred_element_type=F32) + bias

        def softmax(s):
            m2 = jnp.max(s, axis=-1, keepdims=True)
            p = jnp.exp2(s - m2)
            return m2, jnp.sum(p, axis=-1, keepdims=True), p.astype(BF16)

        def values(item, state):
            cur, prev, _ = item
            m2, l2, p = state
            return m2, l2, jnp.dot(p, keys(vd_ref, cur, prev), preferred_element_type=F32)

        def commit(item, state):
            cur = item[0]
            new = tuple(jnp.where(first_head, x[:blk], x[blk:]) for x in state)
            if merge:
                m_old, l_old, o_old = (gather(ref, cur) for ref in stats)
                m_new, l_new, o_new = new
                m_tot = jnp.maximum(m_old, m_new)
                a_old = jnp.exp2(m_old - m_tot)
                a_new = jnp.exp2(m_new - m_tot)
                new = (m_tot, l_old * a_old + l_new * a_new, o_old * a_old + o_new * a_new)
            for ref, val in zip(stats, new):
                scatter(ref, cur, val)

        stages = (lambda it, st: scores(it), lambda it, st: softmax(st), values, commit)
        state = [None] * len(items)
        for t in range(len(items) + len(stages) - 1):
            for k in reversed(range(len(stages))):
                i = t - k
                if 0 <= i < len(items):
                    state[i] = stages[k](items[i], state[i])

    def rows4(c, n):
        return [pl.ds(pl.multiple_of(c * quarter + n * blk, blk), blk)]

    per_trip = ATT_ITEMS // grp

    def items4(n0, first):
        return [(rows4(c, n0 + j), None, own_ref[0]) if first and j == 0
                else (rows4(c, n0 + j), rows4(c, n0 + j - 1), full_ref[0])
                for j in range(per_trip) for c in range(grp)]

    process(items4(0, True), merge=False)

    def body4(i, carry):
        process(items4(i * per_trip, False), merge=False)
        return carry

    lax.fori_loop(1, quarter // (blk * per_trip), body4, 0)

    def rows1(n):
        return [pl.ds(pl.multiple_of(c * quarter + n * slab, slab), slab) for c in range(grp)]

    process([(rows1(0), None, own_ref[1])]
            + [(rows1(n), rows1(n - 1), full_ref[1]) for n in range(1, ATT_ITEMS)], merge=True)

    def body1(i, carry):
        n0 = i * ATT_ITEMS
        process([(rows1(n0 + j), rows1(n0 + j - 1), full_ref[1]) for j in range(ATT_ITEMS)],
                merge=True)
        return carry

    lax.fori_loop(1, seq // (blk * ATT_ITEMS), body1, 0)

    sub_blocks = seq // (DILATIONS[-1] * blk)

    def rows16(c, e, n):
        return [pl.ds(c * quarter + e + n * grp * blk, blk, stride=grp)]

    res_per_trip = max(1, ATT_ITEMS // (grp * sub_blocks))

    def body16(i, carry):
        items = []
        for j in range(res_per_trip):
            c = i * res_per_trip + j
            for e in range(grp):
                items.append((rows16(c, e, 0), None, own_ref[0]))
                for n in range(1, sub_blocks):
                    items.append((rows16(c, e, n), rows16(c, e, n - 1), full_ref[0]))
        process(items, merge=True)
        return carry

    lax.fori_loop(0, grp // res_per_trip, body16, 0)

    def finish(i, carry):
        base = pl.multiple_of(i * blk, blk)
        for c in range(grp):
            rows = pl.ds(c * quarter + base, blk)
            m_ref[pl.ds(c + grp * base, blk, stride=grp), :] = acc_ref[rows, :] / l_ref[rows, :]
        return carry

    lax.fori_loop(0, quarter // blk, finish, 0)

    def emit(i, carry):
        rows = pl.ds(pl.multiple_of(i * blk, blk), blk)
        o_ref[0, rows, :] = m_ref[rows, :].astype(o_ref.dtype)
        return carry

    lax.fori_loop(0, seq // blk, emit, 0)


def _attn_call(aq, ak, av, tail_weights):
    bsz, seq, w = aq.shape
    assert LANES == 2 * ATT_HEAD_DIM, "one grid step handles the two heads of a lane slab"
    pairs = w // LANES
    assert bsz * pairs >= WCAST_PARTS
    full, own = (jnp.asarray(x) for x in _attn_biases())
    blk = pl.BlockSpec((1, seq, LANES), lambda b, hp: (b, 0, hp))
    part = lambda b, hp: (jnp.minimum(b * pairs + hp, WCAST_PARTS - 1), 0)
    w_specs = [pl.BlockSpec((wt.shape[0] // WCAST_PARTS, wt.shape[1]), part) for wt in tail_weights]
    return pl.pallas_call(
        _attn_kernel,
        grid=(bsz, pairs),
        in_specs=[blk, blk, blk, _const_spec(full.shape), _const_spec(own.shape)] + w_specs,
        out_specs=[blk] + w_specs,
        out_shape=[jax.ShapeDtypeStruct((bsz, seq, w), BF16)]
        + [jax.ShapeDtypeStruct(wt.shape, BF16) for wt in tail_weights],
        scratch_shapes=[pltpu.VMEM((seq, LANES), F32) for _ in range(6)],
        compiler_params=pltpu.CompilerParams(
            dimension_semantics=("arbitrary", "arbitrary"), vmem_limit_bytes=VMEM_LIMIT_BYTES),
        name="dilated_attn",
    )(aq, ak, av, full, own, *tail_weights)


def _tail_kernel(x_ref, hg_ref, att_ref, g1_ref, sh2_ref, sc2_ref, g2_ref,
                 an_ref, n2_ref, fin_ref, wo_ref, wgu_ref, wd_ref, o_ref):
    d_ff = wd_ref.shape[0]
    att = att_ref[...].astype(F32)
    ms = jnp.mean(att * att, axis=-1, keepdims=True)
    att_n = (att * lax.rsqrt(ms + RMS_EPS) * an_ref[...]).astype(BF16)
    mix = jnp.dot(hg_ref[...], wo_ref[:HG_WIDTH, :], preferred_element_type=F32)
    mix = mix + jnp.dot(att_n, wo_ref[HG_WIDTH:, :], preferred_element_type=F32)
    x1 = x_ref[...] + g1_ref[0] * mix

    ms = jnp.mean(x1 * x1, axis=-1, keepdims=True)
    h = x1 * lax.rsqrt(ms + RMS_EPS) * n2_ref[...]
    hb = (h * (1.0 + sc2_ref[0]) + sh2_ref[0]).astype(BF16)
    ffn = jnp.zeros_like(x1)
    for j in range(d_ff // FF_CHUNK):
        lo = j * FF_CHUNK
        a = jnp.dot(hb, wgu_ref[:, lo:lo + FF_CHUNK], preferred_element_type=F32)
        u = jnp.dot(hb, wgu_ref[:, d_ff + lo:d_ff + lo + FF_CHUNK], preferred_element_type=F32)
        act = (_silu(a) * u).astype(BF16)
        ffn = ffn + jnp.dot(act, wd_ref[lo:lo + FF_CHUNK, :], preferred_element_type=F32)
    x2 = x1 + g2_ref[0] * ffn
    ms = jnp.mean(x2 * x2, axis=-1, keepdims=True)
    o_ref[...] = x2 * lax.rsqrt(ms + RMS_EPS) * fin_ref[...]


def _tail_call(x2, hg, att, mod3, att_g, norm2_g, final_g, wo_bf, wgu_bf, wd_bf, seq):
    t, d = x2.shape
    tm = ROW_TILE
    steps_per_batch = seq // tm
    row = lambda i: (i, 0)
    mod_col = lambda col: pl.BlockSpec((1, 1, d), lambda i: (i // steps_per_batch, 0, col))
    return pl.pallas_call(
        _tail_kernel,
        grid=(t // tm,),
        in_specs=[
            pl.BlockSpec((tm, d), row),
            pl.BlockSpec((tm, HG_WIDTH), row),
            pl.BlockSpec((tm, ATT_WIDTH), row),
            mod_col(2), mod_col(3), mod_col(4), mod_col(5),
            _const_spec((1, ATT_WIDTH)), _const_spec((1, d)), _const_spec((1, d)),
            _const_spec(wo_bf.shape), _const_spec(wgu_bf.shape), _const_spec(wd_bf.shape),
        ],
        out_specs=pl.BlockSpec((tm, d), row),
        out_shape=jax.ShapeDtypeStruct((t, d), F32),
        compiler_params=pltpu.CompilerParams(
            dimension_semantics=("arbitrary",), vmem_limit_bytes=VMEM_LIMIT_BYTES),
        name="outproj_ffn",
    )(x2, hg, att, mod3, mod3, mod3, mod3,
      att_g.reshape(1, ATT_WIDTH), norm2_g.reshape(1, d), final_g.reshape(1, d),
      wo_bf, wgu_bf, wd_bf)


def kernel(x, c, w_ada, b_ada, norm1_g, w_in, hg_lb_logits, hg_onorm_g, att_onorm_g,
           w_out, norm2_g, w_gate_up, w_down, final_g):
    bsz, seq, d = x.shape
    assert w_in.shape[0] == 1 and hg_lb_logits.shape[0] == 2, "single-layer block expected"
    assert seq % (DILATIONS[-1] * ATT_BLOCK) == 0 and seq % ROW_TILE == 0
    t = bsz * seq
    x2 = x.reshape(t, d)

    mod = _mod_call(c, w_ada[0], b_ada[0])
    mod3 = mod.reshape(bsz, 1, 6 * d)

    q, k, lf_hi, lf_lo, v, g, aq, ak, av, half_decay = _inproj_call(
        x2, mod3, norm1_g[0], w_in[0], hg_lb_logits, hg_onorm_g[0], seq)
    safe = (jnp.min(half_decay[:, :, 0].reshape(-1, HG_GROUP * HG_CHUNK // HG_HALF), axis=1)
            >= HG_SAFE_LOG2_DECAY).astype(jnp.int32)

    as_seq = lambda a: a.reshape(bsz, seq, a.shape[-1])
    hg = _hgrn_call(as_seq(q), as_seq(k), as_seq(lf_hi), as_seq(lf_lo), as_seq(v), as_seq(g), safe)
    att, wo_bf, wgu_bf, wd_bf = _attn_call(
        as_seq(aq), as_seq(ak), as_seq(av), (w_out[0], w_gate_up[0], w_down[0]))

    out = _tail_call(
        x2, hg.reshape(t, HG_WIDTH), att.reshape(t, ATT_WIDTH), mod3,
        att_onorm_g[0], norm2_g[0], final_g, wo_bf, wgu_bf, wd_bf, seq)
    return out.reshape(bsz, seq, d)
```

```python
import numpy as np
import jax
import jax.numpy as jnp
from jax import lax
from jax.experimental import pallas as pl
from jax.experimental.pallas import tpu as pltpu

F32 = jnp.float32
BF16 = jnp.bfloat16

LANES = 128
RMS_EPS = 1e-6
LOG2_E = 1.4426950408889634
PROJ_WIDTH = 512

HG_HEADS = 4
HG_DIM = 128
HG_WIDTH = HG_HEADS * HG_DIM
HG_CHUNK = 128
HG_LEVELS = 7
HG_HALF = HG_CHUNK // 2
HG_SAFE_LOG2_DECAY = -85.0
HG_GROUP = 16

ATT_HEAD_DIM = 64
ATT_WIDTH = 512
ATT_BLOCK = 128
DILATIONS = (1, 4, 16)
ATT_SPAN = 128
NEG_BIG = -1e30
ATT_GROUP = 4
WCAST_PARTS = 16

ROW_TILE = 512
FF_CHUNK = 256
WCAST_ROWS = 128
VMEM_LIMIT_BYTES = 56 * 1024 * 1024


def _silu(x):
    return x * jax.nn.sigmoid(x)


def _const_spec(shape):
    nd = len(shape)
    return pl.BlockSpec(shape, lambda *_: (0,) * nd, pipeline_mode=pl.Buffered(1))


def _mod_kernel(c_ref, w_ref, b_ref, o_ref):
    ca = _silu(c_ref[...])
    o_ref[...] = jnp.dot(ca, w_ref[...], preferred_element_type=F32) + b_ref[...]


def _mod_call(c, w_ada, b_ada):
    bsz, d = c.shape
    n = w_ada.shape[1]
    tn = d
    return pl.pallas_call(
        _mod_kernel,
        grid=(n // tn,),
        in_specs=[
            pl.BlockSpec((bsz, d), lambda j: (0, 0)),
            pl.BlockSpec((d, tn), lambda j: (0, j)),
            pl.BlockSpec((1, tn), lambda j: (0, j)),
        ],
        out_specs=pl.BlockSpec((bsz, tn), lambda j: (0, j)),
        out_shape=jax.ShapeDtypeStruct((bsz, n), F32),
        name="adaln_mod",
    )(c, w_ada, b_ada.reshape(1, n))


def _inproj_kernel(x_ref, sh_ref, sc_ref, g_ref, wf_ref, lbl_ref, on_ref,
                   q_o, k_o, lfh_o, lfl_o, v_o, g_o, aq_o, ak_o, av_o, dmin_o, w_ref):
    @pl.when(pl.program_id(0) == 0)
    def _():
        for r in range(0, wf_ref.shape[0], WCAST_ROWS):
            w_ref[r:r + WCAST_ROWS, :] = wf_ref[r:r + WCAST_ROWS, :].astype(BF16)

    x = x_ref[...]
    ms = jnp.mean(x * x, axis=-1, keepdims=True)
    y = x * lax.rsqrt(ms + RMS_EPS) * g_ref[...]
    h = y * (1.0 + sc_ref[0]) + sh_ref[0]
    hb = h.astype(BF16)
    tm = hb.shape[0]

    def proj(j):
        return jnp.dot(hb, w_ref[:, j * PROJ_WIDTH:(j + 1) * PROJ_WIDTH],
                       preferred_element_type=F32)

    lbl = lbl_ref[...]
    e = jnp.exp(lbl - jnp.max(lbl, axis=0, keepdims=True))
    lb = e[0:1] / (e[0:1] + e[1:2])

    sg = jax.nn.sigmoid(proj(1))
    lf = jnp.log2(lb + (1.0 - lb) * sg)
    lf_hi = lf.astype(BF16)
    lfh_o[...] = lf_hi
    lfl_o[...] = (lf - lf_hi.astype(F32)).astype(BF16)
    k_o[...] = ((1.0 - lb) * (1.0 - sg)).astype(BF16)
    ak_o[...] = proj(5)
    q_o[...] = _silu(proj(0)).astype(BF16)
    av_o[...] = proj(6)
    g_o[...] = (_silu(proj(3)) * on_ref[...]).astype(BF16)
    v_o[...] = proj(2).astype(BF16)
    aq_o[...] = proj(4) * (ATT_HEAD_DIM ** -0.5 * LOG2_E)
    grp_id = lax.broadcasted_iota(jnp.int32, (tm // HG_HALF, tm), 0)
    row_id = lax.broadcasted_iota(jnp.int32, (tm // HG_HALF, tm), 1)
    member = (row_id // HG_HALF == grp_id).astype(BF16)
    totals = jnp.dot(member, lf_hi, preferred_element_type=F32)
    dmin_o[0] = jnp.broadcast_to(jnp.min(totals, axis=-1, keepdims=True), dmin_o.shape[1:])


def _inproj_call(x2, mod3, norm_g, w_in, lb_logits, hg_gain, seq):
    t, d = x2.shape
    tm = ROW_TILE
    steps_per_batch = seq // tm
    row = lambda i: (i, 0)
    out_dtypes = (BF16, BF16, BF16, BF16, BF16, BF16, F32, F32, F32)
    return pl.pallas_call(
        _inproj_kernel,
        grid=(t // tm,),
        in_specs=[
            pl.BlockSpec((tm, d), row),
            pl.BlockSpec((1, 1, d), lambda i: (i // steps_per_batch, 0, 0)),
            pl.BlockSpec((1, 1, d), lambda i: (i // steps_per_batch, 0, 1)),
            _const_spec((1, d)),
            _const_spec(w_in.shape),
            _const_spec(lb_logits.shape),
            _const_spec((1, HG_WIDTH)),
        ],
        out_specs=[pl.BlockSpec((tm, PROJ_WIDTH), row) for _ in out_dtypes]
        + [pl.BlockSpec((1, tm // HG_HALF, LANES), lambda i: (i, 0, 0))],
        out_shape=[jax.ShapeDtypeStruct((t, PROJ_WIDTH), dt) for dt in out_dtypes]
        + [jax.ShapeDtypeStruct((t // tm, tm // HG_HALF, LANES), F32)],
        scratch_shapes=[pltpu.VMEM(w_in.shape, BF16)],
        compiler_params=pltpu.CompilerParams(
            dimension_semantics=("arbitrary",), vmem_limit_bytes=VMEM_LIMIT_BYTES),
        name="in_proj",
    )(x2, mod3, mod3, norm_g.reshape(1, d), w_in, lb_logits,
      jnp.tile(hg_gain.reshape(1, HG_DIM), (1, HG_HEADS)))


def _decay_selectors():
    c = HG_CHUNK
    t = np.arange(c)[:, None]
    u = np.arange(c)[None, :]
    blocks = []
    for l in range(HG_LEVELS):
        lo = (t >> l) << l
        hi = lo + (1 << l) - 1
        second = ((t >> l) & 1) == 1
        blocks.append(np.where(second, (u >= lo) & (u <= t), (u > t) & (u <= hi)))
    blocks.append(u <= t)
    blocks.append(u > t)
    return np.concatenate(blocks, axis=0).astype(np.float32)


def _hgrn_kernel(safe_ref, q_ref, k_ref, lfh_ref, lfl_ref, v_ref, g_ref, p_ref, o_ref, st_ref):
    c = HG_CHUNK
    half = HG_HALF
    step = pl.program_id(0) * pl.num_programs(1) + pl.program_id(1)

    @pl.when(pl.program_id(1) == 0)
    def _():
        st_ref[...] = jnp.zeros_like(st_ref)

    def head(a, h):
        return a[:, h * HG_DIM:(h + 1) * HG_DIM]

    def gram(a, b):
        return lax.dot_general(a, b, (((1,), (1,)), ((), ())), preferred_element_type=F32)

    def log_decay(rows):
        return jnp.concatenate([lfh_ref[0, rows, :], lfl_ref[0, rows, :]], axis=0)

    def exponents(lf2, block):
        return jnp.dot(p_ref[block * c:(block + 1) * c, :], lf2, preferred_element_type=F32)

    def scaled(x_bf, log2_scale):
        return x_bf * jnp.exp2(log2_scale).astype(BF16)

    def normed(rows, o, h):
        ms = jnp.mean(o * o, axis=-1, keepdims=True)
        return (o * lax.rsqrt(ms + RMS_EPS)).astype(BF16) * head(g_ref[0, rows, :], h)

    safe = safe_ref[step] > 0

    @pl.when(safe)
    def _():
        top_mask = (lax.broadcasted_iota(jnp.int32, (half, half), 1)
                    <= lax.broadcasted_iota(jnp.int32, (half, half), 0))
        bot_mask = (lax.broadcasted_iota(jnp.int32, (half, c), 1)
                    <= lax.broadcasted_iota(jnp.int32, (half, c), 0) + half)
        heads = range(HG_HEADS)

        def decay(j):
            rows = slice(j * c, (j + 1) * c)
            return rows, exponents(log_decay(rows), HG_LEVELS)

        def scale(st):
            rows, b = st
            b_mid = b[half - 1:half, :]
            q = q_ref[0, rows, :]
            k = k_ref[0, rows, :]
            return dict(rows=rows, b_mid=b_mid, b_last=b[c - 1:c, :], qe=scaled(q, b),
                        k_top=scaled(k[:half], -b[:half]), k_mid=scaled(k, b_mid - b),
                        q_bot=scaled(q[half:], b[half:] - b_mid))

        def score(st):
            st["s_top"] = [jnp.where(top_mask, gram(head(st["qe"], h)[:half],
                                                    head(st["k_top"], h)), 0.0).astype(BF16)
                           for h in heads]
            st["s_bot"] = [jnp.where(bot_mask, gram(head(st["q_bot"], h),
                                                    head(st["k_mid"], h)), 0.0).astype(BF16)
                           for h in heads]
            return st

        def apply(st):
            v = v_ref[0, st["rows"], :]
            st["intra"] = [jnp.concatenate(
                [jnp.dot(st["s_top"][h], head(v, h)[:half], preferred_element_type=F32),
                 jnp.dot(st["s_bot"][h], head(v, h), preferred_element_type=F32)], axis=0)
                for h in heads]
            tail_decay = jnp.exp2(st["b_last"] - st["b_mid"])
            st["kv"] = [lax.dot_general(head(v, h), head(st["k_mid"], h),
                                        (((0,), (0,)), ((), ())), preferred_element_type=F32)
                        * head(tail_decay, h) for h in heads]
            return st

        states = [st_ref[h] for h in heads]

        def recur(st):
            chunk_decay = jnp.exp2(st["b_last"])
            outs = []
            for h in heads:
                o = st["intra"][h] + gram(head(st["qe"], h), states[h].astype(BF16))
                states[h] = states[h] * head(chunk_decay, h) + st["kv"][h]
                outs.append(normed(st["rows"], o, h))
            o_ref[0, st["rows"], :] = jnp.concatenate(outs, axis=1).astype(o_ref.dtype)
            return None

        stages = (scale, score, apply, recur)
        live = [None] * HG_GROUP
        for t in range(HG_GROUP + len(stages)):
            for kk in reversed(range(len(stages) + 1)):
                j = t - kk
                if 0 <= j < HG_GROUP:
                    live[j] = decay(j) if kk == 0 else stages[kk - 1](live[j])
        for h in heads:
            st_ref[h] = states[h]


    @pl.when(jnp.logical_not(safe))
    def _():
        row = lax.broadcasted_iota(jnp.int32, (c, HG_WIDTH), 0)
        ti = lax.broadcasted_iota(jnp.int32, (c, c), 0)
        si = lax.broadcasted_iota(jnp.int32, (c, c), 1)
        txs = ti ^ si

        def one_chunk(j, carry):
            rows = pl.ds(pl.multiple_of(j * c, c), c)
            q = q_ref[0, rows, :]
            k = k_ref[0, rows, :]
            v = v_ref[0, rows, :]
            lf = log_decay(rows)
            scores = [jnp.where(ti == si, gram(head(q, h), head(k, h)), 0.0)
                      for h in range(HG_HEADS)]
            for l in range(HG_LEVELS):
                second = ((row >> l) & 1) == 1
                xl = scaled(jnp.where(second, q, k), exponents(lf, l))
                mask = (ti > si) & ((txs >> l) == 1)
                for h in range(HG_HEADS):
                    xh = head(xl, h)
                    scores[h] = jnp.where(mask, gram(xh, xh), scores[h])
            b = exponents(lf, HG_LEVELS)
            qe = scaled(q, b)
            k_end = scaled(k, exponents(lf, HG_LEVELS + 1))
            chunk_decay = jnp.exp2(b[c - 1:c, :])
            outs = []
            for h in range(HG_HEADS):
                st = st_ref[h]
                o = jnp.dot(scores[h].astype(BF16), head(v, h), preferred_element_type=F32)
                o = o + gram(head(qe, h), st.astype(BF16))
                st_ref[h] = st * head(chunk_decay, h) + lax.dot_general(
                    head(v, h), head(k_end, h), (((0,), (0,)), ((), ())),
                    preferred_element_type=F32)
                outs.append(normed(rows, o, h))
            o_ref[0, rows, :] = jnp.concatenate(outs, axis=1).astype(o_ref.dtype)
            return carry

        lax.fori_loop(0, HG_GROUP, one_chunk, 0)


def _hgrn_call(q, k, lf_hi, lf_lo, v, g, safe):
    bsz, seq, w = q.shape
    rows = HG_GROUP * HG_CHUNK
    sel = _decay_selectors()
    sel = jnp.asarray(np.concatenate([sel, sel], axis=1), dtype=BF16)
    blk = pl.BlockSpec((1, rows, w), lambda b, i, safe_ref: (b, i, 0))
    return pl.pallas_call(
        _hgrn_kernel,
        grid_spec=pltpu.PrefetchScalarGridSpec(
            num_scalar_prefetch=1,
            grid=(bsz, seq // rows),
            in_specs=[blk, blk, blk, blk, blk, blk, _const_spec(sel.shape)],
            out_specs=blk,
            scratch_shapes=[pltpu.VMEM((HG_HEADS, HG_DIM, HG_DIM), F32)],
        ),
        out_shape=jax.ShapeDtypeStruct((bsz, seq, w), BF16),
        compiler_params=pltpu.CompilerParams(
            dimension_semantics=("arbitrary", "arbitrary"), vmem_limit_bytes=VMEM_LIMIT_BYTES),
        name="hgrn2",
    )(safe, q, k, lf_hi, lf_lo, v, g, sel)


def _attn_biases():
    blk, grp = ATT_BLOCK, ATT_GROUP
    slab = blk // grp
    a = np.arange(blk)
    unslab = grp * (a % slab) + a // slab
    full, own = [], []
    for pos in (a, unslab):
        kpos = np.concatenate([pos, pos + blk])
        dist = blk + pos[:, None] - kpos[None, :]
        keep = (dist >= 0) & (dist <= ATT_SPAN)
        full.append(np.where(np.concatenate([keep, keep], axis=0), 0.0, NEG_BIG))
        keep_own = keep[:, blk:]
        own.append(np.where(np.concatenate([keep_own, keep_own], axis=0), 0.0, NEG_BIG))
    return np.stack(full).astype(np.float32), np.stack(own).astype(np.float32)


def _attn_kernel(q_ref, k_ref, v_ref, full_ref, own_ref, wa_ref, wb_ref, wc_ref,
                 o_ref, wa_out, wb_out, wc_out,
                 qd_ref, kd_ref, vd_ref, acc_ref, m_ref, l_ref, nat_ref):
    seq = q_ref.shape[1]
    blk = ATT_BLOCK
    grp = ATT_GROUP
    quarter = seq // grp
    slab = blk // grp
    blocks = seq // blk
    per_res = quarter // blk

    @pl.when(pl.program_id(0) * pl.num_programs(1) + pl.program_id(1) < WCAST_PARTS)
    def _():
        for src, dst in ((wa_ref, wa_out), (wb_ref, wb_out), (wc_ref, wc_out)):
            dst[...] = src[...].astype(dst.dtype)

    def deinterleave(i):
        for src, dst in ((q_ref, qd_ref), (k_ref, kd_ref), (v_ref, vd_ref)):
            for c in range(grp):
                dst[pl.ds(c * quarter + i * blk, blk), :] = (
                    src[0, pl.ds(c + grp * i * blk, blk, stride=grp), :])

    def finish(c):
        for i in range(per_res):
            rows = pl.ds(c * quarter + i * blk, blk)
            nat_ref[pl.ds(c + grp * i * blk, blk, stride=grp), :] = acc_ref[rows, :] / l_ref[rows, :]

    first_head = lax.broadcasted_iota(jnp.int32, (blk, LANES), 1) < ATT_HEAD_DIM
    stats = (m_ref, l_ref, acc_ref)

    def gather(ref, slices):
        parts = [ref[sl, :] for sl in slices]
        return parts[0] if len(parts) == 1 else jnp.concatenate(parts, axis=0)

    def scatter(ref, slices, val):
        n = val.shape[0] // len(slices)
        for i, sl in enumerate(slices):
            ref[sl, :] = val[i * n:(i + 1) * n]

    def process(items, merge, before=None, after=None):
        def keys(ref, cur, prev):
            own = gather(ref, cur)
            if prev is not None:
                own = jnp.concatenate([gather(ref, prev), own], axis=0)
            return own.astype(BF16)

        def scores(item):
            cur, prev, bias = item
            qf = gather(qd_ref, cur)
            q2 = jnp.concatenate([jnp.where(first_head, qf, 0.0), jnp.where(first_head, 0.0, qf)],
                                 axis=0).astype(BF16)
            return lax.dot_general(q2, keys(kd_ref, cur, prev), (((1,), (1,)), ((), ())),
                                   preferred_element_type=F32) + bias

        def softmax(s):
            m2 = jnp.max(s, axis=-1, keepdims=True)
            p = jnp.exp2(s - m2)
            return m2, jnp.sum(p, axis=-1, keepdims=True), p.astype(BF16)

        def values(item, state):
            cur, prev, _ = item
            m2, l2, p = state
            return m2, l2, jnp.dot(p, keys(vd_ref, cur, prev), preferred_element_type=F32)

        def commit(item, state):
            cur = item[0]
            new = tuple(jnp.where(first_head, x[:blk], x[blk:]) for x in state)
            if merge:
                m_old, l_old, o_old = (gather(ref, cur) for ref in stats)
                m_new, l_new, o_new = new
                m_tot = jnp.maximum(m_old, m_new)
                a_old = jnp.exp2(m_old - m_tot)
                a_new = jnp.exp2(m_new - m_tot)
                new = (m_tot, l_old * a_old + l_new * a_new, o_old * a_old + o_new * a_new)
            for ref, val in zip(stats, new):
                scatter(ref, cur, val)

        stages = (lambda it, st: scores(it), lambda it, st: softmax(st), values, commit)
        state = [None] * len(items)
        for t in range(len(items) + len(stages) - 1):
            for k in reversed(range(len(stages))):
                i = t - k
                if 0 <= i < len(items):
                    if k == 0 and before and i in before:
                        before[i]()
                    state[i] = stages[k](items[i], state[i])
                    if k == len(stages) - 1 and after and i in after:
                        after[i]()

    def rows4(c, n):
        return [pl.ds(c * quarter + n * blk, blk)]

    deinterleave(0)
    process([(rows4(c, n), None, own_ref[0]) if n == 0
             else (rows4(c, n), rows4(c, n - 1), full_ref[0])
             for n in range(per_res) for c in range(grp)], merge=False,
            before={grp * (n - 1): (lambda n=n: deinterleave(n)) for n in range(1, per_res)})

    def rows1(n):
        return [pl.ds(c * quarter + n * slab, slab) for c in range(grp)]

    process([(rows1(0), None, own_ref[1])]
            + [(rows1(n), rows1(n - 1), full_ref[1]) for n in range(1, blocks)], merge=True)

    sub_blocks = seq // (DILATIONS[-1] * blk)

    def rows16(c, e, n):
        return [pl.ds(c * quarter + e + n * grp * blk, blk, stride=grp)]

    items = []
    for c in range(grp):
        for e in range(grp):
            items.append((rows16(c, e, 0), None, own_ref[0]))
            for n in range(1, sub_blocks):
                items.append((rows16(c, e, n), rows16(c, e, n - 1), full_ref[0]))
    per_quarter = grp * sub_blocks
    process(items, merge=True,
            after={(c + 1) * per_quarter - 1: (lambda c=c: finish(c)) for c in range(grp)})

    def emit(i, carry):
        rows = pl.ds(pl.multiple_of(i * blk, blk), blk)
        o_ref[0, rows, :] = nat_ref[rows, :].astype(o_ref.dtype)
        return carry

    lax.fori_loop(0, blocks, emit, 0)


def _attn_call(aq, ak, av, tail_weights):
    bsz, seq, w = aq.shape
    assert LANES == 2 * ATT_HEAD_DIM, "one grid step handles the two heads of a lane slab"
    pairs = w // LANES
    assert bsz * pairs >= WCAST_PARTS
    full, own = (jnp.asarray(x) for x in _attn_biases())
    blk = pl.BlockSpec((1, seq, LANES), lambda b, hp: (b, 0, hp))
    part = lambda b, hp: (jnp.minimum(b * pairs + hp, WCAST_PARTS - 1), 0)
    w_specs = [pl.BlockSpec((wt.shape[0] // WCAST_PARTS, wt.shape[1]), part) for wt in tail_weights]
    return pl.pallas_call(
        _attn_kernel,
        grid=(bsz, pairs),
        in_specs=[blk, blk, blk, _const_spec(full.shape), _const_spec(own.shape)] + w_specs,
        out_specs=[blk] + w_specs,
        out_shape=[jax.ShapeDtypeStruct((bsz, seq, w), BF16)]
        + [jax.ShapeDtypeStruct(wt.shape, BF16) for wt in tail_weights],
        scratch_shapes=[pltpu.VMEM((seq, LANES), F32) for _ in range(7)],
        compiler_params=pltpu.CompilerParams(
            dimension_semantics=("arbitrary", "arbitrary"), vmem_limit_bytes=VMEM_LIMIT_BYTES),
        name="dilated_attn",
    )(aq, ak, av, full, own, *tail_weights)


def _tail_kernel(x_ref, hg_ref, att_ref, g1_ref, sh2_ref, sc2_ref, g2_ref,
                 an_ref, n2_ref, fin_ref, wo_ref, wgu_ref, wd_ref, o_ref):
    d_ff = wd_ref.shape[0]
    att = att_ref[...].astype(F32)
    ms = jnp.mean(att * att, axis=-1, keepdims=True)
    att_n = (att * lax.rsqrt(ms + RMS_EPS) * an_ref[...]).astype(BF16)
    mix = jnp.dot(hg_ref[...], wo_ref[:HG_WIDTH, :], preferred_element_type=F32)
    mix = mix + jnp.dot(att_n, wo_ref[HG_WIDTH:, :], preferred_element_type=F32)
    x1 = x_ref[...] + g1_ref[0] * mix

    ms = jnp.mean(x1 * x1, axis=-1, keepdims=True)
    h = x1 * lax.rsqrt(ms + RMS_EPS) * n2_ref[...]
    hb = (h * (1.0 + sc2_ref[0]) + sh2_ref[0]).astype(BF16)
    ffn = jnp.zeros_like(x1)
    for j in range(d_ff // FF_CHUNK):
        lo = j * FF_CHUNK
        a = jnp.dot(hb, wgu_ref[:, lo:lo + FF_CHUNK], preferred_element_type=F32)
        u = jnp.dot(hb, wgu_ref[:, d_ff + lo:d_ff + lo + FF_CHUNK], preferred_element_type=F32)
        act = (_silu(a) * u).astype(BF16)
        ffn = ffn + jnp.dot(act, wd_ref[lo:lo + FF_CHUNK, :], preferred_element_type=F32)
    x2 = x1 + g2_ref[0] * ffn
    ms = jnp.mean(x2 * x2, axis=-1, keepdims=True)
    o_ref[...] = x2 * lax.rsqrt(ms + RMS_EPS) * fin_ref[...]


def _tail_call(x2, hg, att, mod3, att_g, norm2_g, final_g, wo_bf, wgu_bf, wd_bf, seq):
    t, d = x2.shape
    tm = ROW_TILE
    steps_per_batch = seq // tm
    row = lambda i: (i, 0)
    mod_col = lambda col: pl.BlockSpec((1, 1, d), lambda i: (i // steps_per_batch, 0, col))
    return pl.pallas_call(
        _tail_kernel,
        grid=(t // tm,),
        in_specs=[
            pl.BlockSpec((tm, d), row),
            pl.BlockSpec((tm, HG_WIDTH), row),
            pl.BlockSpec((tm, ATT_WIDTH), row),
            mod_col(2), mod_col(3), mod_col(4), mod_col(5),
            _const_spec((1, ATT_WIDTH)), _const_spec((1, d)), _const_spec((1, d)),
            _const_spec(wo_bf.shape), _const_spec(wgu_bf.shape), _const_spec(wd_bf.shape),
        ],
        out_specs=pl.BlockSpec((tm, d), row),
        out_shape=jax.ShapeDtypeStruct((t, d), F32),
        compiler_params=pltpu.CompilerParams(
            dimension_semantics=("arbitrary",), vmem_limit_bytes=VMEM_LIMIT_BYTES),
        name="outproj_ffn",
    )(x2, hg, att, mod3, mod3, mod3, mod3,
      att_g.reshape(1, ATT_WIDTH), norm2_g.reshape(1, d), final_g.reshape(1, d),
      wo_bf, wgu_bf, wd_bf)


def kernel(x, c, w_ada, b_ada, norm1_g, w_in, hg_lb_logits, hg_onorm_g, att_onorm_g,
           w_out, norm2_g, w_gate_up, w_down, final_g):
    bsz, seq, d = x.shape
    assert w_in.shape[0] == 1 and hg_lb_logits.shape[0] == 2, "single-layer block expected"
    assert seq % (DILATIONS[-1] * ATT_BLOCK) == 0 and seq % ROW_TILE == 0
    t = bsz * seq
    x2 = x.reshape(t, d)

    mod = _mod_call(c, w_ada[0], b_ada[0])
    mod3 = mod.reshape(bsz, 1, 6 * d)

    q, k, lf_hi, lf_lo, v, g, aq, ak, av, half_decay = _inproj_call(
        x2, mod3, norm1_g[0], w_in[0], hg_lb_logits, hg_onorm_g[0], seq)
    safe = (jnp.min(half_decay[:, :, 0].reshape(-1, HG_GROUP * HG_CHUNK // HG_HALF), axis=1)
            >= HG_SAFE_LOG2_DECAY).astype(jnp.int32)

    as_seq = lambda a: a.reshape(bsz, seq, a.shape[-1])
    hg = _hgrn_call(as_seq(q), as_seq(k), as_seq(lf_hi), as_seq(lf_lo), as_seq(v), as_seq(g), safe)
    att, wo_bf, wgu_bf, wd_bf = _attn_call(
        as_seq(aq), as_seq(ak), as_seq(av), (w_out[0], w_gate_up[0], w_down[0]))

    out = _tail_call(
        x2, hg.reshape(t, HG_WIDTH), att.reshape(t, ATT_WIDTH), mod3,
        att_onorm_g[0], norm2_g[0], final_g, wo_bf, wgu_bf, wd_bf, seq)
    return out.reshape(bsz, seq, d)
```

```python
import numpy as np
import jax
import jax.numpy as jnp
from jax import lax
from jax.experimental import pallas as pl
from jax.experimental.pallas import tpu as pltpu

F32 = jnp.float32
BF16 = jnp.bfloat16

LANES = 128
RMS_EPS = 1e-6
LOG2_E = 1.4426950408889634
PROJ_WIDTH = 512

HG_HEADS = 4
HG_DIM = 128
HG_WIDTH = HG_HEADS * HG_DIM
HG_CHUNK = 128
HG_LEVELS = 7
HG_HALF = HG_CHUNK // 2
HG_SAFE_LOG2_DECAY = -85.0
HG_GROUP = 16

ATT_HEAD_DIM = 64
ATT_WIDTH = 512
ATT_BLOCK = 128
DILATIONS = (1, 4, 16)
ATT_SPAN = 128
NEG_BIG = -1e30
ATT_GROUP = 4
WCAST_PARTS = 16

ROW_TILE = 512
FF_CHUNK = 256
WCAST_ROWS = 128
VMEM_LIMIT_BYTES = 56 * 1024 * 1024


def _silu(x):
    return x * jax.nn.sigmoid(x)


def _const_spec(shape):
    nd = len(shape)
    return pl.BlockSpec(shape, lambda *_: (0,) * nd, pipeline_mode=pl.Buffered(1))


def _mod_kernel(c_ref, w_ref, b_ref, o_ref):
    ca = _silu(c_ref[...])
    o_ref[...] = jnp.dot(ca, w_ref[...], preferred_element_type=F32) + b_ref[...]


def _mod_call(c, w_ada, b_ada):
    bsz, d = c.shape
    n = w_ada.shape[1]
    tn = d
    return pl.pallas_call(
        _mod_kernel,
        grid=(n // tn,),
        in_specs=[
            pl.BlockSpec((bsz, d), lambda j: (0, 0)),
            pl.BlockSpec((d, tn), lambda j: (0, j)),
            pl.BlockSpec((1, tn), lambda j: (0, j)),
        ],
        out_specs=pl.BlockSpec((bsz, tn), lambda j: (0, j)),
        out_shape=jax.ShapeDtypeStruct((bsz, n), F32),
        name="adaln_mod",
    )(c, w_ada, b_ada.reshape(1, n))


def _inproj_kernel(x_ref, sh_ref, sc_ref, g_ref, wf_ref, lbl_ref, on_ref,
                   q_o, k_o, lfh_o, lfl_o, v_o, g_o, aq_o, ak_o, av_o, dmin_o, w_ref):
    @pl.when(pl.program_id(0) == 0)
    def _():
        for r in range(0, wf_ref.shape[0], WCAST_ROWS):
            w_ref[r:r + WCAST_ROWS, :] = wf_ref[r:r + WCAST_ROWS, :].astype(BF16)

    x = x_ref[...]
    ms = jnp.mean(x * x, axis=-1, keepdims=True)
    gain = g_ref[...] * (1.0 + sc_ref[0])
    hb = (x * lax.rsqrt(ms + RMS_EPS) * gain + sh_ref[0]).astype(BF16)
    tm = hb.shape[0]

    def proj(j):
        return jnp.dot(hb, w_ref[:, j * PROJ_WIDTH:(j + 1) * PROJ_WIDTH],
                       preferred_element_type=F32)

    lbl = lbl_ref[...]
    e = jnp.exp(lbl - jnp.max(lbl, axis=0, keepdims=True))
    lb = e[0:1] / (e[0:1] + e[1:2])

    sg = jax.nn.sigmoid(proj(1))
    lf = jnp.log2(lb + (1.0 - lb) * sg)
    lf_hi = lf.astype(BF16)
    lfh_o[...] = lf_hi
    lfl_o[...] = (lf - lf_hi.astype(F32)).astype(BF16)
    k_o[...] = ((1.0 - lb) * (1.0 - sg)).astype(BF16)
    ak_o[...] = proj(5)
    q_o[...] = _silu(proj(0)).astype(BF16)
    av_o[...] = proj(6)
    g_o[...] = (_silu(proj(3)) * on_ref[...]).astype(BF16)
    v_o[...] = proj(2).astype(BF16)
    aq_o[...] = proj(4) * (ATT_HEAD_DIM ** -0.5 * LOG2_E)
    grp_id = lax.broadcasted_iota(jnp.int32, (tm // HG_HALF, tm), 0)
    row_id = lax.broadcasted_iota(jnp.int32, (tm // HG_HALF, tm), 1)
    member = (row_id // HG_HALF == grp_id).astype(BF16)
    totals = jnp.dot(member, lf_hi, preferred_element_type=F32)
    dmin_o[0] = jnp.broadcast_to(jnp.min(totals, axis=-1, keepdims=True), dmin_o.shape[1:])


def _inproj_call(x2, mod3, norm_g, w_in, lb_logits, hg_gain, seq):
    t, d = x2.shape
    tm = ROW_TILE
    steps_per_batch = seq // tm
    row = lambda i: (i, 0)
    out_dtypes = (BF16, BF16, BF16, BF16, BF16, BF16, F32, F32, F32)
    return pl.pallas_call(
        _inproj_kernel,
        grid=(t // tm,),
        in_specs=[
            pl.BlockSpec((tm, d), row),
            pl.BlockSpec((1, 1, d), lambda i: (i // steps_per_batch, 0, 0)),
            pl.BlockSpec((1, 1, d), lambda i: (i // steps_per_batch, 0, 1)),
            _const_spec((1, d)),
            _const_spec(w_in.shape),
            _const_spec(lb_logits.shape),
            _const_spec((1, HG_WIDTH)),
        ],
        out_specs=[pl.BlockSpec((tm, PROJ_WIDTH), row) for _ in out_dtypes]
        + [pl.BlockSpec((1, tm // HG_HALF, LANES), lambda i: (i, 0, 0))],
        out_shape=[jax.ShapeDtypeStruct((t, PROJ_WIDTH), dt) for dt in out_dtypes]
        + [jax.ShapeDtypeStruct((t // tm, tm // HG_HALF, LANES), F32)],
        scratch_shapes=[pltpu.VMEM(w_in.shape, BF16)],
        compiler_params=pltpu.CompilerParams(
            dimension_semantics=("arbitrary",), vmem_limit_bytes=VMEM_LIMIT_BYTES),
        name="in_proj",
    )(x2, mod3, mod3, norm_g.reshape(1, d), w_in, lb_logits,
      jnp.tile(hg_gain.reshape(1, HG_DIM), (1, HG_HEADS)))


def _decay_selectors():
    c = HG_CHUNK
    t = np.arange(c)[:, None]
    u = np.arange(c)[None, :]
    blocks = []
    for l in range(HG_LEVELS):
        lo = (t >> l) << l
        hi = lo + (1 << l) - 1
        second = ((t >> l) & 1) == 1
        blocks.append(np.where(second, (u >= lo) & (u <= t), (u > t) & (u <= hi)))
    blocks.append(u <= t)
    blocks.append(u > t)
    return np.concatenate(blocks, axis=0).astype(np.float32)


def _hgrn_kernel(safe_ref, q_ref, k_ref, lfh_ref, lfl_ref, v_ref, g_ref, p_ref, o_ref, st_ref):
    c = HG_CHUNK
    half = HG_HALF
    step = pl.program_id(0) * pl.num_programs(1) + pl.program_id(1)

    @pl.when(pl.program_id(1) == 0)
    def _():
        st_ref[...] = jnp.zeros_like(st_ref)

    def head(a, h):
        return a[:, h * HG_DIM:(h + 1) * HG_DIM]

    def gram(a, b):
        return lax.dot_general(a, b, (((1,), (1,)), ((), ())), preferred_element_type=F32)

    def log_decay(rows):
        return jnp.concatenate([lfh_ref[0, rows, :], lfl_ref[0, rows, :]], axis=0)

    def exponents(lf2, block):
        return jnp.dot(p_ref[block * c:(block + 1) * c, :], lf2, preferred_element_type=F32)

    def scaled(x_bf, log2_scale):
        return x_bf * jnp.exp2(log2_scale).astype(BF16)

    def normed(rows, o, h):
        ms = jnp.mean(o * o, axis=-1, keepdims=True)
        return (o * lax.rsqrt(ms + RMS_EPS)).astype(BF16) * head(g_ref[0, rows, :], h)

    safe = safe_ref[step] > 0

    @pl.when(safe)
    def _():
        top_mask = (lax.broadcasted_iota(jnp.int32, (half, half), 1)
                    <= lax.broadcasted_iota(jnp.int32, (half, half), 0))
        bot_mask = (lax.broadcasted_iota(jnp.int32, (half, c), 1)
                    <= lax.broadcasted_iota(jnp.int32, (half, c), 0) + half)
        heads = range(HG_HEADS)

        def decay(j):
            rows = slice(j * c, (j + 1) * c)
            return rows, exponents(log_decay(rows), HG_LEVELS)

        def scale(st):
            rows, b = st
            b_mid = b[half - 1:half, :]
            q = q_ref[0, rows, :]
            k = k_ref[0, rows, :]
            return dict(rows=rows, b_mid=b_mid, b_last=b[c - 1:c, :], qe=scaled(q, b),
                        k_top=scaled(k[:half], -b[:half]), k_mid=scaled(k, b_mid - b),
                        q_bot=scaled(q[half:], b[half:] - b_mid))

        def score(st):
            st["s_top"] = [jnp.where(top_mask, gram(head(st["qe"], h)[:half],
                                                    head(st["k_top"], h)), 0.0).astype(BF16)
                           for h in heads]
            st["s_bot"] = [jnp.where(bot_mask, gram(head(st["q_bot"], h),
                                                    head(st["k_mid"], h)), 0.0).astype(BF16)
                           for h in heads]
            return st

        def apply(st):
            v = v_ref[0, st["rows"], :]
            st["intra"] = [jnp.concatenate(
                [jnp.dot(st["s_top"][h], head(v, h)[:half], preferred_element_type=F32),
                 jnp.dot(st["s_bot"][h], head(v, h), preferred_element_type=F32)], axis=0)
                for h in heads]
            tail_decay = jnp.exp2(st["b_last"] - st["b_mid"])
            st["kv"] = [lax.dot_general(head(v, h), head(st["k_mid"], h),
                                        (((0,), (0,)), ((), ())), preferred_element_type=F32)
                        * head(tail_decay, h) for h in heads]
            return st

        states = [st_ref[h] for h in heads]

        def recur(st):
            chunk_decay = jnp.exp2(st["b_last"])
            outs = []
            for h in heads:
                o = st["intra"][h] + gram(head(st["qe"], h), states[h].astype(BF16))
                states[h] = states[h] * head(chunk_decay, h) + st["kv"][h]
                outs.append(normed(st["rows"], o, h))
            o_ref[0, st["rows"], :] = jnp.concatenate(outs, axis=1).astype(o_ref.dtype)
            return None

        stages = (scale, score, apply, recur)
        live = [None] * HG_GROUP
        for t in range(HG_GROUP + len(stages)):
            for kk in reversed(range(len(stages) + 1)):
                j = t - kk
                if 0 <= j < HG_GROUP:
                    live[j] = decay(j) if kk == 0 else stages[kk - 1](live[j])
        for h in heads:
            st_ref[h] = states[h]


    @pl.when(jnp.logical_not(safe))
    def _():
        row = lax.broadcasted_iota(jnp.int32, (c, HG_WIDTH), 0)
        ti = lax.broadcasted_iota(jnp.int32, (c, c), 0)
        si = lax.broadcasted_iota(jnp.int32, (c, c), 1)
        txs = ti ^ si

        def one_chunk(j, carry):
            rows = pl.ds(pl.multiple_of(j * c, c), c)
            q = q_ref[0, rows, :]
            k = k_ref[0, rows, :]
            v = v_ref[0, rows, :]
            lf = log_decay(rows)
            scores = [jnp.where(ti == si, gram(head(q, h), head(k, h)), 0.0)
                      for h in range(HG_HEADS)]
            for l in range(HG_LEVELS):
                second = ((row >> l) & 1) == 1
                xl = scaled(jnp.where(second, q, k), exponents(lf, l))
                mask = (ti > si) & ((txs >> l) == 1)
                for h in range(HG_HEADS):
                    xh = head(xl, h)
                    scores[h] = jnp.where(mask, gram(xh, xh), scores[h])
            b = exponents(lf, HG_LEVELS)
            qe = scaled(q, b)
            k_end = scaled(k, exponents(lf, HG_LEVELS + 1))
            chunk_decay = jnp.exp2(b[c - 1:c, :])
            outs = []
            for h in range(HG_HEADS):
                st = st_ref[h]
                o = jnp.dot(scores[h].astype(BF16), head(v, h), preferred_element_type=F32)
                o = o + gram(head(qe, h), st.astype(BF16))
                st_ref[h] = st * head(chunk_decay, h) + lax.dot_general(
                    head(v, h), head(k_end, h), (((0,), (0,)), ((), ())),
                    preferred_element_type=F32)
                outs.append(normed(rows, o, h))
            o_ref[0, rows, :] = jnp.concatenate(outs, axis=1).astype(o_ref.dtype)
            return carry

        lax.fori_loop(0, HG_GROUP, one_chunk, 0)


def _hgrn_call(q, k, lf_hi, lf_lo, v, g, safe):
    bsz, seq, w = q.shape
    rows = HG_GROUP * HG_CHUNK
    sel = _decay_selectors()
    sel = jnp.asarray(np.concatenate([sel, sel], axis=1), dtype=BF16)
    blk = pl.BlockSpec((1, rows, w), lambda b, i, safe_ref: (b, i, 0))
    return pl.pallas_call(
        _hgrn_kernel,
        grid_spec=pltpu.PrefetchScalarGridSpec(
            num_scalar_prefetch=1,
            grid=(bsz, seq // rows),
            in_specs=[blk, blk, blk, blk, blk, blk, _const_spec(sel.shape)],
            out_specs=blk,
            scratch_shapes=[pltpu.VMEM((HG_HEADS, HG_DIM, HG_DIM), F32)],
        ),
        out_shape=jax.ShapeDtypeStruct((bsz, seq, w), BF16),
        compiler_params=pltpu.CompilerParams(
            dimension_semantics=("arbitrary", "arbitrary"), vmem_limit_bytes=VMEM_LIMIT_BYTES),
        name="hgrn2",
    )(safe, q, k, lf_hi, lf_lo, v, g, sel)


def _attn_biases():
    blk, grp = ATT_BLOCK, ATT_GROUP
    slab = blk // grp
    a = np.arange(blk)
    unslab = grp * (a % slab) + a // slab
    full, own = [], []
    for pos in (a, unslab):
        kpos = np.concatenate([pos, pos + blk])
        dist = blk + pos[:, None] - kpos[None, :]
        keep = (dist >= 0) & (dist <= ATT_SPAN)
        full.append(np.where(np.concatenate([keep, keep], axis=0), 0.0, NEG_BIG))
        keep_own = keep[:, blk:]
        own.append(np.where(np.concatenate([keep_own, keep_own], axis=0), 0.0, NEG_BIG))
    return np.stack(full).astype(np.float32), np.stack(own).astype(np.float32)


def _attn_kernel(q_ref, k_ref, v_ref, full_ref, own_ref, wa_ref, wb_ref, wc_ref,
                 o_ref, wa_out, wb_out, wc_out,
                 qd_ref, kd_ref, vd_ref, acc_ref, m_ref, l_ref):
    seq = q_ref.shape[1]
    blk = ATT_BLOCK
    grp = ATT_GROUP
    quarter = seq // grp
    slab = blk // grp
    blocks = seq // blk
    per_res = quarter // blk

    @pl.when(pl.program_id(0) * pl.num_programs(1) + pl.program_id(1) < WCAST_PARTS)
    def _():
        for src, dst in ((wa_ref, wa_out), (wb_ref, wb_out), (wc_ref, wc_out)):
            dst[...] = src[...].astype(dst.dtype)

    def deinterleave(i):
        for src, dst in ((q_ref, qd_ref), (k_ref, kd_ref), (v_ref, vd_ref)):
            for c in range(grp):
                dst[pl.ds(c * quarter + i * blk, blk), :] = (
                    src[0, pl.ds(c + grp * i * blk, blk, stride=grp), :])

    def finish(c):
        for i in range(per_res):
            rows = pl.ds(c * quarter + i * blk, blk)
            o_ref[0, pl.ds(c + grp * i * blk, blk, stride=grp), :] = acc_ref[rows, :] / l_ref[rows, :]

    first_head = lax.broadcasted_iota(jnp.int32, (blk, LANES), 1) < ATT_HEAD_DIM
    stats = (m_ref, l_ref, acc_ref)

    def gather(ref, slices):
        parts = [ref[sl, :] for sl in slices]
        return parts[0] if len(parts) == 1 else jnp.concatenate(parts, axis=0)

    def scatter(ref, slices, val):
        n = val.shape[0] // len(slices)
        for i, sl in enumerate(slices):
            ref[sl, :] = val[i * n:(i + 1) * n]

    def process(items, merge, before=None, after=None):
        def keys(ref, cur, prev):
            own = gather(ref, cur)
            if prev is not None:
                own = jnp.concatenate([gather(ref, prev), own], axis=0)
            return own.astype(BF16)

        def scores(item):
            cur, prev, bias = item
            qf = gather(qd_ref, cur)
            q2 = jnp.concatenate([jnp.where(first_head, qf, 0.0), jnp.where(first_head, 0.0, qf)],
                                 axis=0).astype(BF16)
            return lax.dot_general(q2, keys(kd_ref, cur, prev), (((1,), (1,)), ((), ())),
                                   preferred_element_type=F32) + bias

        def softmax(s):
            m2 = jnp.max(s, axis=-1, keepdims=True)
            p = jnp.exp2(s - m2)
            return m2, jnp.sum(p, axis=-1, keepdims=True), p.astype(BF16)

        def values(item, state):
            cur, prev, _ = item
            m2, l2, p = state
            return m2, l2, jnp.dot(p, keys(vd_ref, cur, prev), preferred_element_type=F32)

        def commit(item, state):
            cur = item[0]
            new = tuple(jnp.where(first_head, x[:blk], x[blk:]) for x in state)
            if merge:
                m_old, l_old, o_old = (gather(ref, cur) for ref in stats)
                m_new, l_new, o_new = new
                m_tot = jnp.maximum(m_old, m_new)
                a_old = jnp.exp2(m_old - m_tot)
                a_new = jnp.exp2(m_new - m_tot)
                new = (m_tot, l_old * a_old + l_new * a_new, o_old * a_old + o_new * a_new)
            for ref, val in zip(stats, new):
                scatter(ref, cur, val)

        stages = (lambda it, st: scores(it), lambda it, st: softmax(st), values, commit)
        state = [None] * len(items)
        for t in range(len(items) + len(stages) - 1):
            for k in reversed(range(len(stages))):
                i = t - k
                if 0 <= i < len(items):
                    if k == 0 and before and i in before:
                        before[i]()
                    state[i] = stages[k](items[i], state[i])
                    if k == len(stages) - 1 and after and i in after:
                        after[i]()

    def rows4(c, n):
        return [pl.ds(c * quarter + n * blk, blk)]

    deinterleave(0)
    process([(rows4(c, n), None, own_ref[0]) if n == 0
             else (rows4(c, n), rows4(c, n - 1), full_ref[0])
             for n in range(per_res) for c in range(grp)], merge=False,
            before={grp * (n - 1): (lambda n=n: deinterleave(n)) for n in range(1, per_res)})

    def rows1(n):
        return [pl.ds(c * quarter + n * slab, slab) for c in range(grp)]

    process([(rows1(0), None, own_ref[1])]
            + [(rows1(n), rows1(n - 1), full_ref[1]) for n in range(1, blocks)], merge=True)

    sub_blocks = seq // (DILATIONS[-1] * blk)

    def rows16(c, e, n):
        return [pl.ds(c * quarter + e + n * grp * blk, blk, stride=grp)]

    items = []
    for c in range(grp):
        for e in range(grp):
            items.append((rows16(c, e, 0), None, own_ref[0]))
            for n in range(1, sub_blocks):
                items.append((rows16(c, e, n), rows16(c, e, n - 1), full_ref[0]))
    per_quarter = grp * sub_blocks
    process(items, merge=True,
            after={(c + 1) * per_quarter - 1: (lambda c=c: finish(c)) for c in range(grp)})


def _attn_call(aq, ak, av, tail_weights):
    bsz, seq, w = aq.shape
    assert LANES == 2 * ATT_HEAD_DIM, "one grid step handles the two heads of a lane slab"
    pairs = w // LANES
    assert bsz * pairs >= WCAST_PARTS
    full, own = (jnp.asarray(x) for x in _attn_biases())
    blk = pl.BlockSpec((1, seq, LANES), lambda b, hp: (b, 0, hp))
    part = lambda b, hp: (jnp.minimum(b * pairs + hp, WCAST_PARTS - 1), 0)
    w_specs = [pl.BlockSpec((wt.shape[0] // WCAST_PARTS, wt.shape[1]), part) for wt in tail_weights]
    return pl.pallas_call(
        _attn_kernel,
        grid=(bsz, pairs),
        in_specs=[blk, blk, blk, _const_spec(full.shape), _const_spec(own.shape)] + w_specs,
        out_specs=[blk] + w_specs,
        out_shape=[jax.ShapeDtypeStruct((bsz, seq, w), F32)]
        + [jax.ShapeDtypeStruct(wt.shape, BF16) for wt in tail_weights],
        scratch_shapes=[pltpu.VMEM((seq, LANES), F32) for _ in range(6)],
        compiler_params=pltpu.CompilerParams(
            dimension_semantics=("arbitrary", "arbitrary"), vmem_limit_bytes=VMEM_LIMIT_BYTES),
        name="dilated_attn",
    )(aq, ak, av, full, own, *tail_weights)


def _tail_kernel(x_ref, hg_ref, att_ref, g1_ref, sh2_ref, sc2_ref, g2_ref,
                 an_ref, n2_ref, fin_ref, wo_ref, wgu_ref, wd_ref, o_ref):
    d_ff = wd_ref.shape[0]
    att = att_ref[...]
    ms = jnp.mean(att * att, axis=-1, keepdims=True)
    att_n = (att * lax.rsqrt(ms + RMS_EPS) * an_ref[...]).astype(BF16)
    mix = jnp.dot(hg_ref[...], wo_ref[:HG_WIDTH, :], preferred_element_type=F32)
    mix = mix + jnp.dot(att_n, wo_ref[HG_WIDTH:, :], preferred_element_type=F32)
    x1 = x_ref[...] + g1_ref[0] * mix

    ms = jnp.mean(x1 * x1, axis=-1, keepdims=True)
    gain = n2_ref[...] * (1.0 + sc2_ref[0])
    hb = (x1 * lax.rsqrt(ms + RMS_EPS) * gain + sh2_ref[0]).astype(BF16)
    ffn = jnp.zeros_like(x1)
    for j in range(d_ff // FF_CHUNK):
        lo = j * FF_CHUNK
        a = jnp.dot(hb, wgu_ref[:, lo:lo + FF_CHUNK], preferred_element_type=F32)
        u = jnp.dot(hb, wgu_ref[:, d_ff + lo:d_ff + lo + FF_CHUNK], preferred_element_type=F32)
        act = (_silu(a) * u).astype(BF16)
        ffn = ffn + jnp.dot(act, wd_ref[lo:lo + FF_CHUNK, :], preferred_element_type=F32)
    x2 = x1 + g2_ref[0] * ffn
    ms = jnp.mean(x2 * x2, axis=-1, keepdims=True)
    o_ref[...] = x2 * lax.rsqrt(ms + RMS_EPS) * fin_ref[...]


def _tail_call(x2, hg, att, mod3, att_g, norm2_g, final_g, wo_bf, wgu_bf, wd_bf, seq):
    t, d = x2.shape
    tm = ROW_TILE
    steps_per_batch = seq // tm
    row = lambda i: (i, 0)
    mod_col = lambda col: pl.BlockSpec((1, 1, d), lambda i: (i // steps_per_batch, 0, col))
    return pl.pallas_call(
        _tail_kernel,
        grid=(t // tm,),
        in_specs=[
            pl.BlockSpec((tm, d), row),
            pl.BlockSpec((tm, HG_WIDTH), row),
            pl.BlockSpec((tm, ATT_WIDTH), row),
            mod_col(2), mod_col(3), mod_col(4), mod_col(5),
            _const_spec((1, ATT_WIDTH)), _const_spec((1, d)), _const_spec((1, d)),
            _const_spec(wo_bf.shape), _const_spec(wgu_bf.shape), _const_spec(wd_bf.shape),
        ],
        out_specs=pl.BlockSpec((tm, d), row),
        out_shape=jax.ShapeDtypeStruct((t, d), F32),
        compiler_params=pltpu.CompilerParams(
            dimension_semantics=("arbitrary",), vmem_limit_bytes=VMEM_LIMIT_BYTES),
        name="outproj_ffn",
    )(x2, hg, att, mod3, mod3, mod3, mod3,
      att_g.reshape(1, ATT_WIDTH), norm2_g.reshape(1, d), final_g.reshape(1, d),
      wo_bf, wgu_bf, wd_bf)


def kernel(x, c, w_ada, b_ada, norm1_g, w_in, hg_lb_logits, hg_onorm_g, att_onorm_g,
           w_out, norm2_g, w_gate_up, w_down, final_g):
    bsz, seq, d = x.shape
    assert w_in.shape[0] == 1 and hg_lb_logits.shape[0] == 2, "single-layer block expected"
    assert seq % (DILATIONS[-1] * ATT_BLOCK) == 0 and seq % ROW_TILE == 0
    t = bsz * seq
    x2 = x.reshape(t, d)

    mod = _mod_call(c, w_ada[0], b_ada[0])
    mod3 = mod.reshape(bsz, 1, 6 * d)

    q, k, lf_hi, lf_lo, v, g, aq, ak, av, half_decay = _inproj_call(
        x2, mod3, norm1_g[0], w_in[0], hg_lb_logits, hg_onorm_g[0], seq)
    safe = (jnp.min(half_decay[:, :, 0].reshape(-1, HG_GROUP * HG_CHUNK // HG_HALF), axis=1)
            >= HG_SAFE_LOG2_DECAY).astype(jnp.int32)

    as_seq = lambda a: a.reshape(bsz, seq, a.shape[-1])
    hg = _hgrn_call(as_seq(q), as_seq(k), as_seq(lf_hi), as_seq(lf_lo), as_seq(v), as_seq(g), safe)
    att, wo_bf, wgu_bf, wd_bf = _attn_call(
        as_seq(aq), as_seq(ak), as_seq(av), (w_out[0], w_gate_up[0], w_down[0]))

    out = _tail_call(
        x2, hg.reshape(t, HG_WIDTH), att.reshape(t, ATT_WIDTH), mod3,
        att_onorm_g[0], norm2_g[0], final_g, wo_bf, wgu_bf, wd_bf, seq)
    return out.reshape(bsz, seq, d)
```

```python
import numpy as np
import jax
import jax.numpy as jnp
from jax import lax
from jax.experimental import pallas as pl
from jax.experimental.pallas import tpu as pltpu

F32 = jnp.float32
BF16 = jnp.bfloat16

LANES = 128
RMS_EPS = 1e-6
LOG2_E = 1.4426950408889634
PROJ_WIDTH = 512

HG_HEADS = 4
HG_DIM = 128
HG_WIDTH = HG_HEADS * HG_DIM
HG_CHUNK = 128
HG_LEVELS = 7
HG_HALF = HG_CHUNK // 2
HG_SAFE_LOG2_DECAY = -85.0
HG_GROUP = 16

ATT_HEAD_DIM = 64
ATT_WIDTH = 512
ATT_BLOCK = 128
DILATIONS = (1, 4, 16)
ATT_SPAN = 128
NEG_BIG = -1e30
ATT_GROUP = 4
WCAST_PARTS = 16

ROW_TILE = 512
FF_CHUNK = 256
WCAST_ROWS = 128
VMEM_LIMIT_BYTES = 56 * 1024 * 1024


def _silu(x):
    return x * jax.nn.sigmoid(x)


def _const_spec(shape):
    nd = len(shape)
    return pl.BlockSpec(shape, lambda *_: (0,) * nd, pipeline_mode=pl.Buffered(1))


def _mod_kernel(c_ref, w_ref, b_ref, o_ref):
    ca = _silu(c_ref[...])
    o_ref[...] = jnp.dot(ca, w_ref[...], preferred_element_type=F32) + b_ref[...]


def _mod_call(c, w_ada, b_ada):
    bsz, d = c.shape
    n = w_ada.shape[1]
    tn = d
    return pl.pallas_call(
        _mod_kernel,
        grid=(n // tn,),
        in_specs=[
            pl.BlockSpec((bsz, d), lambda j: (0, 0)),
            pl.BlockSpec((d, tn), lambda j: (0, j)),
            pl.BlockSpec((1, tn), lambda j: (0, j)),
        ],
        out_specs=pl.BlockSpec((bsz, tn), lambda j: (0, j)),
        out_shape=jax.ShapeDtypeStruct((bsz, n), F32),
        name="adaln_mod",
    )(c, w_ada, b_ada.reshape(1, n))


def _inproj_kernel(x_ref, sh_ref, sc_ref, g_ref, wf_ref, lbl_ref, on_ref,
                   q_o, k_o, lfh_o, lfl_o, v_o, g_o, aq_o, ak_o, av_o, dmin_o, w_ref):
    @pl.when(pl.program_id(0) == 0)
    def _():
        for r in range(0, wf_ref.shape[0], WCAST_ROWS):
            w_ref[r:r + WCAST_ROWS, :] = wf_ref[r:r + WCAST_ROWS, :].astype(BF16)

    x = x_ref[...]
    ms = jnp.mean(x * x, axis=-1, keepdims=True)
    gain = g_ref[...] * (1.0 + sc_ref[0])
    hb = (x * lax.rsqrt(ms + RMS_EPS) * gain + sh_ref[0]).astype(BF16)
    tm = hb.shape[0]

    def proj(j):
        return jnp.dot(hb, w_ref[:, j * PROJ_WIDTH:(j + 1) * PROJ_WIDTH],
                       preferred_element_type=F32)

    lbl = lbl_ref[...]
    e = jnp.exp(lbl - jnp.max(lbl, axis=0, keepdims=True))
    lb = e[0:1] / (e[0:1] + e[1:2])

    sg = jax.nn.sigmoid(proj(1))
    lf = jnp.log2(lb + (1.0 - lb) * sg)
    lf_hi = lf.astype(BF16)
    lfh_o[...] = lf_hi
    lfl_o[...] = (lf - lf_hi.astype(F32)).astype(BF16)
    k_o[...] = ((1.0 - lb) * (1.0 - sg)).astype(BF16)
    ak_o[...] = proj(5)
    q_o[...] = _silu(proj(0)).astype(BF16)
    av_o[...] = proj(6)
    g_o[...] = (_silu(proj(3)) * on_ref[...]).astype(BF16)
    grp_id = lax.broadcasted_iota(jnp.int32, (tm // HG_HALF, tm), 0)
    row_id = lax.broadcasted_iota(jnp.int32, (tm // HG_HALF, tm), 1)
    member = (row_id // HG_HALF == grp_id).astype(BF16)
    totals = jnp.dot(member, lf_hi, preferred_element_type=F32)
    dmin_o[0] = jnp.broadcast_to(jnp.min(totals, axis=-1, keepdims=True), dmin_o.shape[1:])
    v_o[...] = proj(2).astype(BF16)
    aq_o[...] = proj(4) * (ATT_HEAD_DIM ** -0.5 * LOG2_E)


def _inproj_call(x2, mod3, norm_g, w_in, lb_logits, hg_gain, seq):
    t, d = x2.shape
    tm = ROW_TILE
    steps_per_batch = seq // tm
    row = lambda i: (i, 0)
    out_dtypes = (BF16, BF16, BF16, BF16, BF16, BF16, F32, F32, F32)
    return pl.pallas_call(
        _inproj_kernel,
        grid=(t // tm,),
        in_specs=[
            pl.BlockSpec((tm, d), row),
            pl.BlockSpec((1, 1, d), lambda i: (i // steps_per_batch, 0, 0)),
            pl.BlockSpec((1, 1, d), lambda i: (i // steps_per_batch, 0, 1)),
            _const_spec((1, d)),
            _const_spec(w_in.shape),
            _const_spec(lb_logits.shape),
            _const_spec((1, HG_WIDTH)),
        ],
        out_specs=[pl.BlockSpec((tm, PROJ_WIDTH), row) for _ in out_dtypes]
        + [pl.BlockSpec((1, tm // HG_HALF, LANES), lambda i: (i, 0, 0))],
        out_shape=[jax.ShapeDtypeStruct((t, PROJ_WIDTH), dt) for dt in out_dtypes]
        + [jax.ShapeDtypeStruct((t // tm, tm // HG_HALF, LANES), F32)],
        scratch_shapes=[pltpu.VMEM(w_in.shape, BF16)],
        compiler_params=pltpu.CompilerParams(
            dimension_semantics=("arbitrary",), vmem_limit_bytes=VMEM_LIMIT_BYTES),
        name="in_proj",
    )(x2, mod3, mod3, norm_g.reshape(1, d), w_in, lb_logits,
      jnp.tile(hg_gain.reshape(1, HG_DIM), (1, HG_HEADS)))


def _decay_selectors():
    c = HG_CHUNK
    t = np.arange(c)[:, None]
    u = np.arange(c)[None, :]
    blocks = []
    for l in range(HG_LEVELS):
        lo = (t >> l) << l
        hi = lo + (1 << l) - 1
        second = ((t >> l) & 1) == 1
        blocks.append(np.where(second, (u >= lo) & (u <= t), (u > t) & (u <= hi)))
    blocks.append(u <= t)
    blocks.append(u > t)
    return np.concatenate(blocks, axis=0).astype(np.float32)


def _hgrn_kernel(safe_ref, q_ref, k_ref, lfh_ref, lfl_ref, v_ref, g_ref, p_ref, o_ref, st_ref):
    c = HG_CHUNK
    half = HG_HALF
    step = pl.program_id(0) * pl.num_programs(1) + pl.program_id(1)

    @pl.when(pl.program_id(1) == 0)
    def _():
        st_ref[...] = jnp.zeros_like(st_ref)

    def head(a, h):
        return a[:, h * HG_DIM:(h + 1) * HG_DIM]

    def gram(a, b):
        return lax.dot_general(a, b, (((1,), (1,)), ((), ())), preferred_element_type=F32)

    def log_decay(rows):
        return jnp.concatenate([lfh_ref[0, rows, :], lfl_ref[0, rows, :]], axis=0)

    def exponents(lf2, block):
        return jnp.dot(p_ref[block * c:(block + 1) * c, :], lf2, preferred_element_type=F32)

    def scaled(x_bf, log2_scale):
        return x_bf * jnp.exp2(log2_scale).astype(BF16)

    def normed(rows, o, h):
        ms = jnp.mean(o * o, axis=-1, keepdims=True)
        return (o * lax.rsqrt(ms + RMS_EPS)).astype(BF16) * head(g_ref[0, rows, :], h)

    safe = safe_ref[step] > 0

    @pl.when(safe)
    def _():
        top_mask = (lax.broadcasted_iota(jnp.int32, (half, half), 1)
                    <= lax.broadcasted_iota(jnp.int32, (half, half), 0))
        bot_mask = (lax.broadcasted_iota(jnp.int32, (half, c), 1)
                    <= lax.broadcasted_iota(jnp.int32, (half, c), 0) + half)
        heads = range(HG_HEADS)

        def decay(j):
            rows = slice(j * c, (j + 1) * c)
            return rows, exponents(log_decay(rows), HG_LEVELS)

        def scale(st):
            rows, b = st
            b_mid = b[half - 1:half, :]
            q = q_ref[0, rows, :]
            k = k_ref[0, rows, :]
            return dict(rows=rows, b_mid=b_mid, b_last=b[c - 1:c, :], qe=scaled(q, b),
                        k_top=scaled(k[:half], -b[:half]), k_mid=scaled(k, b_mid - b),
                        q_bot=scaled(q[half:], b[half:] - b_mid))

        def score(st):
            st["s_top"] = [jnp.where(top_mask, gram(head(st["qe"], h)[:half],
                                                    head(st["k_top"], h)), 0.0).astype(BF16)
                           for h in heads]
            st["s_bot"] = [jnp.where(bot_mask, gram(head(st["q_bot"], h),
                                                    head(st["k_mid"], h)), 0.0).astype(BF16)
                           for h in heads]
            return st

        def apply(st):
            v = v_ref[0, st["rows"], :]
            st["intra"] = [jnp.concatenate(
                [jnp.dot(st["s_top"][h], head(v, h)[:half], preferred_element_type=F32),
                 jnp.dot(st["s_bot"][h], head(v, h), preferred_element_type=F32)], axis=0)
                for h in heads]
            tail_decay = jnp.exp2(st["b_last"] - st["b_mid"])
            st["kv"] = [lax.dot_general(head(v, h), head(st["k_mid"], h),
                                        (((0,), (0,)), ((), ())), preferred_element_type=F32)
                        * head(tail_decay, h) for h in heads]
            return st

        states = [st_ref[h] for h in heads]

        def recur(st):
            chunk_decay = jnp.exp2(st["b_last"])
            outs = []
            for h in heads:
                o = st["intra"][h] + gram(head(st["qe"], h), states[h].astype(BF16))
                states[h] = states[h] * head(chunk_decay, h) + st["kv"][h]
                outs.append(normed(st["rows"], o, h))
            o_ref[0, st["rows"], :] = jnp.concatenate(outs, axis=1).astype(o_ref.dtype)
            return None

        stages = (scale, score, apply, recur)
        live = [None] * HG_GROUP
        for t in range(HG_GROUP + len(stages)):
            for kk in reversed(range(len(stages) + 1)):
                j = t - kk
                if 0 <= j < HG_GROUP:
                    live[j] = decay(j) if kk == 0 else stages[kk - 1](live[j])
        for h in heads:
            st_ref[h] = states[h]


    @pl.when(jnp.logical_not(safe))
    def _():
        row = lax.broadcasted_iota(jnp.int32, (c, HG_WIDTH), 0)
        ti = lax.broadcasted_iota(jnp.int32, (c, c), 0)
        si = lax.broadcasted_iota(jnp.int32, (c, c), 1)
        txs = ti ^ si

        def one_chunk(j, carry):
            rows = pl.ds(pl.multiple_of(j * c, c), c)
            q = q_ref[0, rows, :]
            k = k_ref[0, rows, :]
            v = v_ref[0, rows, :]
            lf = log_decay(rows)
            scores = [jnp.where(ti == si, gram(head(q, h), head(k, h)), 0.0)
                      for h in range(HG_HEADS)]
            for l in range(HG_LEVELS):
                second = ((row >> l) & 1) == 1
                xl = scaled(jnp.where(second, q, k), exponents(lf, l))
                mask = (ti > si) & ((txs >> l) == 1)
                for h in range(HG_HEADS):
                    xh = head(xl, h)
                    scores[h] = jnp.where(mask, gram(xh, xh), scores[h])
            b = exponents(lf, HG_LEVELS)
            qe = scaled(q, b)
            k_end = scaled(k, exponents(lf, HG_LEVELS + 1))
            chunk_decay = jnp.exp2(b[c - 1:c, :])
            outs = []
            for h in range(HG_HEADS):
                st = st_ref[h]
                o = jnp.dot(scores[h].astype(BF16), head(v, h), preferred_element_type=F32)
                o = o + gram(head(qe, h), st.astype(BF16))
                st_ref[h] = st * head(chunk_decay, h) + lax.dot_general(
                    head(v, h), head(k_end, h), (((0,), (0,)), ((), ())),
                    preferred_element_type=F32)
                outs.append(normed(rows, o, h))
            o_ref[0, rows, :] = jnp.concatenate(outs, axis=1).astype(o_ref.dtype)
            return carry

        lax.fori_loop(0, HG_GROUP, one_chunk, 0)


def _hgrn_call(q, k, lf_hi, lf_lo, v, g, safe):
    bsz, seq, w = q.shape
    rows = HG_GROUP * HG_CHUNK
    sel = _decay_selectors()
    sel = jnp.asarray(np.concatenate([sel, sel], axis=1), dtype=BF16)
    blk = pl.BlockSpec((1, rows, w), lambda b, i, safe_ref: (b, i, 0))
    return pl.pallas_call(
        _hgrn_kernel,
        grid_spec=pltpu.PrefetchScalarGridSpec(
            num_scalar_prefetch=1,
            grid=(bsz, seq // rows),
            in_specs=[blk, blk, blk, blk, blk, blk, _const_spec(sel.shape)],
            out_specs=blk,
            scratch_shapes=[pltpu.VMEM((HG_HEADS, HG_DIM, HG_DIM), F32)],
        ),
        out_shape=jax.ShapeDtypeStruct((bsz, seq, w), BF16),
        compiler_params=pltpu.CompilerParams(
            dimension_semantics=("arbitrary", "arbitrary"), vmem_limit_bytes=VMEM_LIMIT_BYTES),
        name="hgrn2",
    )(safe, q, k, lf_hi, lf_lo, v, g, sel)


def _attn_biases():
    blk, grp = ATT_BLOCK, ATT_GROUP
    slab = blk // grp
    a = np.arange(blk)
    unslab = grp * (a % slab) + a // slab
    full, own = [], []
    for pos in (a, unslab):
        kpos = np.concatenate([pos, pos + blk])
        dist = blk + pos[:, None] - kpos[None, :]
        keep = (dist >= 0) & (dist <= ATT_SPAN)
        full.append(np.where(np.concatenate([keep, keep], axis=0), 0.0, NEG_BIG))
        keep_own = keep[:, blk:]
        own.append(np.where(np.concatenate([keep_own, keep_own], axis=0), 0.0, NEG_BIG))
    return np.stack(full).astype(np.float32), np.stack(own).astype(np.float32)


def _attn_kernel(q_ref, k_ref, v_ref, full_ref, own_ref, wa_ref, wb_ref, wc_ref,
                 o_ref, wa_out, wb_out, wc_out,
                 qd_ref, kd_ref, vd_ref, acc_ref, m_ref, l_ref):
    seq = q_ref.shape[1]
    blk = ATT_BLOCK
    grp = ATT_GROUP
    quarter = seq // grp
    slab = blk // grp
    blocks = seq // blk
    per_res = quarter // blk

    @pl.when(pl.program_id(0) * pl.num_programs(1) + pl.program_id(1) < WCAST_PARTS)
    def _():
        for src, dst in ((wa_ref, wa_out), (wb_ref, wb_out), (wc_ref, wc_out)):
            dst[...] = src[...].astype(dst.dtype)

    def deinterleave(i):
        for src, dst in ((q_ref, qd_ref), (k_ref, kd_ref), (v_ref, vd_ref)):
            for c in range(grp):
                dst[pl.ds(c * quarter + i * blk, blk), :] = (
                    src[0, pl.ds(c + grp * i * blk, blk, stride=grp), :])

    def finish(c):
        for i in range(per_res):
            rows = pl.ds(c * quarter + i * blk, blk)
            o_ref[0, pl.ds(c + grp * i * blk, blk, stride=grp), :] = acc_ref[rows, :] / l_ref[rows, :]

    first_head = lax.broadcasted_iota(jnp.int32, (blk, LANES), 1) < ATT_HEAD_DIM
    stats = (m_ref, l_ref, acc_ref)

    def gather(ref, slices):
        parts = [ref[sl, :] for sl in slices]
        return parts[0] if len(parts) == 1 else jnp.concatenate(parts, axis=0)

    def scatter(ref, slices, val):
        n = val.shape[0] // len(slices)
        for i, sl in enumerate(slices):
            ref[sl, :] = val[i * n:(i + 1) * n]

    def process(items, merge, before=None, after=None):
        def keys(ref, cur, prev):
            own = gather(ref, cur)
            if prev is not None:
                own = jnp.concatenate([gather(ref, prev), own], axis=0)
            return own.astype(BF16)

        def scores(item):
            cur, prev, bias = item
            qf = gather(qd_ref, cur)
            q2 = jnp.concatenate([jnp.where(first_head, qf, 0.0), jnp.where(first_head, 0.0, qf)],
                                 axis=0).astype(BF16)
            return lax.dot_general(q2, keys(kd_ref, cur, prev), (((1,), (1,)), ((), ())),
                                   preferred_element_type=F32) + bias

        def softmax(s):
            m2 = jnp.max(s, axis=-1, keepdims=True)
            p = jnp.exp2(s - m2)
            return m2, jnp.sum(p, axis=-1, keepdims=True), p.astype(BF16)

        def values(item, state):
            cur, prev, _ = item
            m2, l2, p = state
            return m2, l2, jnp.dot(p, keys(vd_ref, cur, prev), preferred_element_type=F32)

        def commit(item, state):
            cur = item[0]
            new = tuple(jnp.where(first_head, x[:blk], x[blk:]) for x in state)
            if merge:
                m_old, l_old, o_old = (gather(ref, cur) for ref in stats)
                m_new, l_new, o_new = new
                m_tot = jnp.maximum(m_old, m_new)
                a_old = jnp.exp2(m_old - m_tot)
                a_new = jnp.exp2(m_new - m_tot)
                new = (m_tot, l_old * a_old + l_new * a_new, o_old * a_old + o_new * a_new)
            for ref, val in zip(stats, new):
                scatter(ref, cur, val)

        stages = (lambda it, st: scores(it), lambda it, st: softmax(st), values, commit)
        state = [None] * len(items)
        for t in range(len(items) + len(stages) - 1):
            for k in reversed(range(len(stages))):
                i = t - k
                if 0 <= i < len(items):
                    if k == 0 and before and i in before:
                        before[i]()
                    state[i] = stages[k](items[i], state[i])
                    if k == len(stages) - 1 and after and i in after:
                        after[i]()

    def rows4(c, n):
        return [pl.ds(c * quarter + n * blk, blk)]

    deinterleave(0)
    process([(rows4(c, n), None, own_ref[0]) if n == 0
             else (rows4(c, n), rows4(c, n - 1), full_ref[0])
             for n in range(per_res) for c in range(grp)], merge=False,
            before={grp * (n - 1): (lambda n=n: deinterleave(n)) for n in range(1, per_res)})

    def rows1(n):
        return [pl.ds(c * quarter + n * slab, slab) for c in range(grp)]

    process([(rows1(0), None, own_ref[1])]
            + [(rows1(n), rows1(n - 1), full_ref[1]) for n in range(1, blocks)], merge=True)

    sub_blocks = seq // (DILATIONS[-1] * blk)

    def rows16(c, e, n):
        return [pl.ds(c * quarter + e + n * grp * blk, blk, stride=grp)]

    items = []
    for c in range(grp):
        for e in range(grp):
            items.append((rows16(c, e, 0), None, own_ref[0]))
            for n in range(1, sub_blocks):
                items.append((rows16(c, e, n), rows16(c, e, n - 1), full_ref[0]))
    per_quarter = grp * sub_blocks
    process(items, merge=True,
            after={(c + 1) * per_quarter - 1: (lambda c=c: finish(c)) for c in range(grp)})


def _attn_call(aq, ak, av, tail_weights):
    bsz, seq, w = aq.shape
    assert LANES == 2 * ATT_HEAD_DIM, "one grid step handles the two heads of a lane slab"
    pairs = w // LANES
    assert bsz * pairs >= WCAST_PARTS
    full, own = (jnp.asarray(x) for x in _attn_biases())
    blk = pl.BlockSpec((1, seq, LANES), lambda b, hp: (b, 0, hp))
    part = lambda b, hp: (jnp.minimum(b * pairs + hp, WCAST_PARTS - 1), 0)
    w_specs = [pl.BlockSpec((wt.shape[0] // WCAST_PARTS, wt.shape[1]), part) for wt in tail_weights]
    return pl.pallas_call(
        _attn_kernel,
        grid=(bsz, pairs),
        in_specs=[blk, blk, blk, _const_spec(full.shape), _const_spec(own.shape)] + w_specs,
        out_specs=[blk] + w_specs,
        out_shape=[jax.ShapeDtypeStruct((bsz, seq, w), F32)]
        + [jax.ShapeDtypeStruct(wt.shape, BF16) for wt in tail_weights],
        scratch_shapes=[pltpu.VMEM((seq, LANES), F32) for _ in range(6)],
        compiler_params=pltpu.CompilerParams(
            dimension_semantics=("arbitrary", "arbitrary"), vmem_limit_bytes=VMEM_LIMIT_BYTES),
        name="dilated_attn",
    )(aq, ak, av, full, own, *tail_weights)


def _tail_kernel(x_ref, hg_ref, att_ref, g1_ref, sh2_ref, sc2_ref, g2_ref,
                 an_ref, n2_ref, fin_ref, wo_ref, wgu_ref, wd_ref, o_ref):
    d_ff = wd_ref.shape[0]
    att = att_ref[...]
    ms = jnp.mean(att * att, axis=-1, keepdims=True)
    att_n = (att * lax.rsqrt(ms + RMS_EPS) * an_ref[...]).astype(BF16)
    mix = jnp.dot(hg_ref[...], wo_ref[:HG_WIDTH, :], preferred_element_type=F32)
    mix = mix + jnp.dot(att_n, wo_ref[HG_WIDTH:, :], preferred_element_type=F32)
    x1 = x_ref[...] + g1_ref[0] * mix

    ms = jnp.mean(x1 * x1, axis=-1, keepdims=True)
    gain = n2_ref[...] * (1.0 + sc2_ref[0])
    hb = (x1 * lax.rsqrt(ms + RMS_EPS) * gain + sh2_ref[0]).astype(BF16)
    ffn = jnp.zeros_like(x1)
    for j in range(d_ff // FF_CHUNK):
        lo = j * FF_CHUNK
        a = jnp.dot(hb, wgu_ref[:, lo:lo + FF_CHUNK], preferred_element_type=F32)
        u = jnp.dot(hb, wgu_ref[:, d_ff + lo:d_ff + lo + FF_CHUNK], preferred_element_type=F32)
        act = (_silu(a) * u).astype(BF16)
        ffn = ffn + jnp.dot(act, wd_ref[lo:lo + FF_CHUNK, :], preferred_element_type=F32)
    x2 = x1 + g2_ref[0] * ffn
    ms = jnp.mean(x2 * x2, axis=-1, keepdims=True)
    o_ref[...] = x2 * lax.rsqrt(ms + RMS_EPS) * fin_ref[...]


def _tail_call(x2, hg, att, mod3, att_g, norm2_g, final_g, wo_bf, wgu_bf, wd_bf, seq):
    t, d = x2.shape
    tm = ROW_TILE
    steps_per_batch = seq // tm
    row = lambda i: (i, 0)
    mod_col = lambda col: pl.BlockSpec((1, 1, d), lambda i: (i // steps_per_batch, 0, col))
    return pl.pallas_call(
        _tail_kernel,
        grid=(t // tm,),
        in_specs=[
            pl.BlockSpec((tm, d), row),
            pl.BlockSpec((tm, HG_WIDTH), row),
            pl.BlockSpec((tm, ATT_WIDTH), row),
            mod_col(2), mod_col(3), mod_col(4), mod_col(5),
            _const_spec((1, ATT_WIDTH)), _const_spec((1, d)), _const_spec((1, d)),
            _const_spec(wo_bf.shape), _const_spec(wgu_bf.shape), _const_spec(wd_bf.shape),
        ],
        out_specs=pl.BlockSpec((tm, d), row),
        out_shape=jax.ShapeDtypeStruct((t, d), F32),
        compiler_params=pltpu.CompilerParams(
            dimension_semantics=("arbitrary",), vmem_limit_bytes=VMEM_LIMIT_BYTES),
        name="outproj_ffn",
    )(x2, hg, att, mod3, mod3, mod3, mod3,
      att_g.reshape(1, ATT_WIDTH), norm2_g.reshape(1, d), final_g.reshape(1, d),
      wo_bf, wgu_bf, wd_bf)


def kernel(x, c, w_ada, b_ada, norm1_g, w_in, hg_lb_logits, hg_onorm_g, att_onorm_g,
           w_out, norm2_g, w_gate_up, w_down, final_g):
    bsz, seq, d = x.shape
    assert w_in.shape[0] == 1 and hg_lb_logits.shape[0] == 2, "single-layer block expected"
    assert seq % (DILATIONS[-1] * ATT_BLOCK) == 0 and seq % ROW_TILE == 0
    t = bsz * seq
    x2 = x.reshape(t, d)

    mod = _mod_call(c, w_ada[0], b_ada[0])
    mod3 = mod.reshape(bsz, 1, 6 * d)

    q, k, lf_hi, lf_lo, v, g, aq, ak, av, half_decay = _inproj_call(
        x2, mod3, norm1_g[0], w_in[0], hg_lb_logits, hg_onorm_g[0], seq)
    safe = (jnp.min(half_decay[:, :, 0].reshape(-1, HG_GROUP * HG_CHUNK // HG_HALF), axis=1)
            >= HG_SAFE_LOG2_DECAY).astype(jnp.int32)

    as_seq = lambda a: a.reshape(bsz, seq, a.shape[-1])
    hg = _hgrn_call(as_seq(q), as_seq(k), as_seq(lf_hi), as_seq(lf_lo), as_seq(v), as_seq(g), safe)
    att, wo_bf, wgu_bf, wd_bf = _attn_call(
        as_seq(aq), as_seq(ak), as_seq(av), (w_out[0], w_gate_up[0], w_down[0]))

    out = _tail_call(
        x2, hg.reshape(t, HG_WIDTH), att.reshape(t, ATT_WIDTH), mod3,
        att_onorm_g[0], norm2_g[0], final_g, wo_bf, wgu_bf, wd_bf, seq)
    return out.reshape(bsz, seq, d)
```

```python
import numpy as np
import jax
import jax.numpy as jnp
from jax import lax
from jax.experimental import pallas as pl
from jax.experimental.pallas import tpu as pltpu

F32 = jnp.float32
BF16 = jnp.bfloat16

LANES = 128
RMS_EPS = 1e-6
LOG2_E = 1.4426950408889634
PROJ_WIDTH = 512

HG_HEADS = 4
HG_DIM = 128
HG_WIDTH = HG_HEADS * HG_DIM
HG_CHUNK = 128
HG_LEVELS = 7
HG_HALF = HG_CHUNK // 2
HG_SAFE_LOG2_DECAY = -85.0
HG_GROUP = 16

ATT_HEAD_DIM = 64
ATT_WIDTH = 512
ATT_BLOCK = 128
DILATIONS = (1, 4, 16)
ATT_SPAN = 128
NEG_BIG = -1e30
ATT_GROUP = 4
WCAST_PARTS = 16

ROW_TILE = 512
FF_CHUNK = 256
WCAST_ROWS = 128
VMEM_LIMIT_BYTES = 56 * 1024 * 1024


def _silu(x):
    return x * jax.nn.sigmoid(x)


def _const_spec(shape):
    nd = len(shape)
    return pl.BlockSpec(shape, lambda *_: (0,) * nd, pipeline_mode=pl.Buffered(1))


def _mod_kernel(c_ref, w_ref, b_ref, o_ref):
    ca = _silu(c_ref[...])
    o_ref[...] = jnp.dot(ca, w_ref[...], preferred_element_type=F32) + b_ref[...]


def _mod_call(c, w_ada, b_ada):
    bsz, d = c.shape
    n = w_ada.shape[1]
    tn = d
    return pl.pallas_call(
        _mod_kernel,
        grid=(n // tn,),
        in_specs=[
            pl.BlockSpec((bsz, d), lambda j: (0, 0)),
            pl.BlockSpec((d, tn), lambda j: (0, j)),
            pl.BlockSpec((1, tn), lambda j: (0, j)),
        ],
        out_specs=pl.BlockSpec((bsz, tn), lambda j: (0, j)),
        out_shape=jax.ShapeDtypeStruct((bsz, n), F32),
        name="adaln_mod",
    )(c, w_ada, b_ada.reshape(1, n))


def _inproj_kernel(x_ref, sh_ref, sc_ref, g_ref, wf_ref, lbl_ref, on_ref,
                   q_o, k_o, lfh_o, lfl_o, v_o, g_o, aq_o, ak_o, av_o, dmin_o, w_ref):
    @pl.when(pl.program_id(0) == 0)
    def _():
        for r in range(0, wf_ref.shape[0], WCAST_ROWS):
            w_ref[r:r + WCAST_ROWS, :] = wf_ref[r:r + WCAST_ROWS, :].astype(BF16)

    x = x_ref[...]
    ms = jnp.mean(x * x, axis=-1, keepdims=True)
    gain = g_ref[...] * (1.0 + sc_ref[0])
    hb = (x * lax.rsqrt(ms + RMS_EPS) * gain + sh_ref[0]).astype(BF16)
    tm = hb.shape[0]

    def proj(j):
        return jnp.dot(hb, w_ref[:, j * PROJ_WIDTH:(j + 1) * PROJ_WIDTH],
                       preferred_element_type=F32)

    lbl = lbl_ref[...]
    e = jnp.exp(lbl - jnp.max(lbl, axis=0, keepdims=True))
    lb = e[0:1] / (e[0:1] + e[1:2])

    sg = jax.nn.sigmoid(proj(1))
    lf = jnp.log2(lb + (1.0 - lb) * sg)
    lf_hi = lf.astype(BF16)
    lfh_o[...] = lf_hi
    lfl_o[...] = (lf - lf_hi.astype(F32)).astype(BF16)
    k_o[...] = ((1.0 - lb) * (1.0 - sg)).astype(BF16)
    ak_o[...] = proj(5)
    q_o[...] = _silu(proj(0)).astype(BF16)
    av_o[...] = proj(6)
    g_o[...] = (_silu(proj(3)) * on_ref[...]).astype(BF16)
    grp_id = lax.broadcasted_iota(jnp.int32, (tm // HG_HALF, tm), 0)
    row_id = lax.broadcasted_iota(jnp.int32, (tm // HG_HALF, tm), 1)
    member = (row_id // HG_HALF == grp_id).astype(BF16)
    totals = jnp.dot(member, lf_hi, preferred_element_type=F32)
    dmin_o[0] = jnp.broadcast_to(jnp.min(totals, axis=-1, keepdims=True), dmin_o.shape[1:])
    v_o[...] = proj(2).astype(BF16)
    aq_o[...] = proj(4) * (ATT_HEAD_DIM ** -0.5 * LOG2_E)


def _inproj_call(x2, mod3, norm_g, w_in, lb_logits, hg_gain, seq):
    t, d = x2.shape
    tm = ROW_TILE
    steps_per_batch = seq // tm
    row = lambda i: (i, 0)
    out_dtypes = (BF16, BF16, BF16, BF16, BF16, BF16, F32, F32, F32)
    return pl.pallas_call(
        _inproj_kernel,
        grid=(t // tm,),
        in_specs=[
            pl.BlockSpec((tm, d), row),
            pl.BlockSpec((1, 1, d), lambda i: (i // steps_per_batch, 0, 0)),
            pl.BlockSpec((1, 1, d), lambda i: (i // steps_per_batch, 0, 1)),
            _const_spec((1, d)),
            _const_spec(w_in.shape),
            _const_spec(lb_logits.shape),
            _const_spec((1, HG_WIDTH)),
        ],
        out_specs=[pl.BlockSpec((tm, PROJ_WIDTH), row) for _ in out_dtypes]
        + [pl.BlockSpec((1, tm // HG_HALF, LANES), lambda i: (i, 0, 0))],
        out_shape=[jax.ShapeDtypeStruct((t, PROJ_WIDTH), dt) for dt in out_dtypes]
        + [jax.ShapeDtypeStruct((t // tm, tm // HG_HALF, LANES), F32)],
        scratch_shapes=[pltpu.VMEM(w_in.shape, BF16)],
        compiler_params=pltpu.CompilerParams(
            dimension_semantics=("arbitrary",), vmem_limit_bytes=VMEM_LIMIT_BYTES),
        name="in_proj",
    )(x2, mod3, mod3, norm_g.reshape(1, d), w_in, lb_logits,
      jnp.tile(hg_gain.reshape(1, HG_DIM), (1, HG_HEADS)))


def _decay_selectors():
    c = HG_CHUNK
    t = np.arange(c)[:, None]
    u = np.arange(c)[None, :]
    blocks = []
    for l in range(HG_LEVELS):
        lo = (t >> l) << l
        hi = lo + (1 << l) - 1
        second = ((t >> l) & 1) == 1
        blocks.append(np.where(second, (u >= lo) & (u <= t), (u > t) & (u <= hi)))
    blocks.append(u <= t)
    blocks.append(u > t)
    return np.concatenate(blocks, axis=0).astype(np.float32)


def _hgrn_kernel(safe_ref, q_ref, k_ref, lfh_ref, lfl_ref, v_ref, g_ref, p_ref, o_ref, st_ref):
    c = HG_CHUNK
    half = HG_HALF
    step = pl.program_id(0) * pl.num_programs(1) + pl.program_id(1)

    @pl.when(pl.program_id(1) == 0)
    def _():
        st_ref[...] = jnp.zeros_like(st_ref)

    def head(a, h):
        return a[:, h * HG_DIM:(h + 1) * HG_DIM]

    def gram(a, b):
        return lax.dot_general(a, b, (((1,), (1,)), ((), ())), preferred_element_type=F32)

    def log_decay(rows):
        return jnp.concatenate([lfh_ref[0, rows, :], lfl_ref[0, rows, :]], axis=0)

    def exponents(lf2, block):
        return jnp.dot(p_ref[block * c:(block + 1) * c, :], lf2, preferred_element_type=F32)

    def scaled(x_bf, log2_scale):
        return x_bf * jnp.exp2(log2_scale).astype(BF16)

    def normed(rows, o, h):
        ms = jnp.mean(o * o, axis=-1, keepdims=True)
        return (o * lax.rsqrt(ms + RMS_EPS)).astype(BF16) * head(g_ref[0, rows, :], h)

    safe = safe_ref[step] > 0

    @pl.when(safe)
    def _():
        causal = (lax.broadcasted_iota(jnp.int32, (c, c), 1)
                  <= lax.broadcasted_iota(jnp.int32, (c, c), 0))
        heads = range(HG_HEADS)

        def decay(j):
            rows = slice(j * c, (j + 1) * c)
            return rows, exponents(log_decay(rows), HG_LEVELS)

        def scale(st):
            rows, b = st
            b_mid = b[half - 1:half, :]
            q = q_ref[0, rows, :]
            k = k_ref[0, rows, :]
            return dict(rows=rows, b_mid=b_mid, b_last=b[c - 1:c, :], qe=scaled(q, b),
                        q_mid=scaled(q, b - b_mid), k_mid=scaled(k, b_mid - b))

        def score(st):
            st["s"] = [jnp.where(causal, gram(head(st["q_mid"], h), head(st["k_mid"], h)),
                                 0.0).astype(BF16) for h in heads]
            return st

        def apply(st):
            v = v_ref[0, st["rows"], :]
            st["intra"] = [jnp.dot(st["s"][h], head(v, h), preferred_element_type=F32)
                           for h in heads]
            tail_decay = jnp.exp2(st["b_last"] - st["b_mid"])
            st["kv"] = [lax.dot_general(head(v, h), head(st["k_mid"], h),
                                        (((0,), (0,)), ((), ())), preferred_element_type=F32)
                        * head(tail_decay, h) for h in heads]
            return st

        states = [st_ref[h] for h in heads]

        def recur(st):
            chunk_decay = jnp.exp2(st["b_last"])
            outs = []
            for h in heads:
                o = st["intra"][h] + gram(head(st["qe"], h), states[h].astype(BF16))
                states[h] = states[h] * head(chunk_decay, h) + st["kv"][h]
                outs.append(normed(st["rows"], o, h))
            o_ref[0, st["rows"], :] = jnp.concatenate(outs, axis=1).astype(o_ref.dtype)
            return None

        stages = (scale, score, apply, recur)
        live = [None] * HG_GROUP
        for t in range(HG_GROUP + len(stages)):
            for kk in reversed(range(len(stages) + 1)):
                j = t - kk
                if 0 <= j < HG_GROUP:
                    live[j] = decay(j) if kk == 0 else stages[kk - 1](live[j])
        for h in heads:
            st_ref[h] = states[h]


    @pl.when(jnp.logical_not(safe))
    def _():
        row = lax.broadcasted_iota(jnp.int32, (c, HG_WIDTH), 0)
        ti = lax.broadcasted_iota(jnp.int32, (c, c), 0)
        si = lax.broadcasted_iota(jnp.int32, (c, c), 1)
        txs = ti ^ si

        def one_chunk(j, carry):
            rows = pl.ds(pl.multiple_of(j * c, c), c)
            q = q_ref[0, rows, :]
            k = k_ref[0, rows, :]
            v = v_ref[0, rows, :]
            lf = log_decay(rows)
            scores = [jnp.where(ti == si, gram(head(q, h), head(k, h)), 0.0)
                      for h in range(HG_HEADS)]
            for l in range(HG_LEVELS):
                second = ((row >> l) & 1) == 1
                xl = scaled(jnp.where(second, q, k), exponents(lf, l))
                mask = (ti > si) & ((txs >> l) == 1)
                for h in range(HG_HEADS):
                    xh = head(xl, h)
                    scores[h] = jnp.where(mask, gram(xh, xh), scores[h])
            b = exponents(lf, HG_LEVELS)
            qe = scaled(q, b)
            k_end = scaled(k, exponents(lf, HG_LEVELS + 1))
            chunk_decay = jnp.exp2(b[c - 1:c, :])
            outs = []
            for h in range(HG_HEADS):
                st = st_ref[h]
                o = jnp.dot(scores[h].astype(BF16), head(v, h), preferred_element_type=F32)
                o = o + gram(head(qe, h), st.astype(BF16))
                st_ref[h] = st * head(chunk_decay, h) + lax.dot_general(
                    head(v, h), head(k_end, h), (((0,), (0,)), ((), ())),
                    preferred_element_type=F32)
                outs.append(normed(rows, o, h))
            o_ref[0, rows, :] = jnp.concatenate(outs, axis=1).astype(o_ref.dtype)
            return carry

        lax.fori_loop(0, HG_GROUP, one_chunk, 0)


def _hgrn_call(q, k, lf_hi, lf_lo, v, g, safe):
    bsz, seq, w = q.shape
    rows = HG_GROUP * HG_CHUNK
    sel = _decay_selectors()
    sel = jnp.asarray(np.concatenate([sel, sel], axis=1), dtype=BF16)
    blk = pl.BlockSpec((1, rows, w), lambda b, i, safe_ref: (b, i, 0))
    return pl.pallas_call(
        _hgrn_kernel,
        grid_spec=pltpu.PrefetchScalarGridSpec(
            num_scalar_prefetch=1,
            grid=(bsz, seq // rows),
            in_specs=[blk, blk, blk, blk, blk, blk, _const_spec(sel.shape)],
            out_specs=blk,
            scratch_shapes=[pltpu.VMEM((HG_HEADS, HG_DIM, HG_DIM), F32)],
        ),
        out_shape=jax.ShapeDtypeStruct((bsz, seq, w), BF16),
        compiler_params=pltpu.CompilerParams(
            dimension_semantics=("arbitrary", "arbitrary"), vmem_limit_bytes=VMEM_LIMIT_BYTES),
        name="hgrn2",
    )(safe, q, k, lf_hi, lf_lo, v, g, sel)


def _attn_biases():
    blk, grp = ATT_BLOCK, ATT_GROUP
    slab = blk // grp
    a = np.arange(blk)
    unslab = grp * (a % slab) + a // slab
    full, own = [], []
    for pos in (a, unslab):
        kpos = np.concatenate([pos, pos + blk])
        dist = blk + pos[:, None] - kpos[None, :]
        keep = (dist >= 0) & (dist <= ATT_SPAN)
        full.append(np.where(np.concatenate([keep, keep], axis=0), 0.0, NEG_BIG))
        keep_own = keep[:, blk:]
        own.append(np.where(np.concatenate([keep_own, keep_own], axis=0), 0.0, NEG_BIG))
    return np.stack(full).astype(np.float32), np.stack(own).astype(np.float32)


def _attn_kernel(q_ref, k_ref, v_ref, full_ref, own_ref, wa_ref, wb_ref, wc_ref,
                 o_ref, wa_out, wb_out, wc_out,
                 qd_ref, kd_ref, vd_ref, acc_ref, m_ref, l_ref):
    seq = q_ref.shape[1]
    blk = ATT_BLOCK
    grp = ATT_GROUP
    quarter = seq // grp
    slab = blk // grp
    blocks = seq // blk
    per_res = quarter // blk

    @pl.when(pl.program_id(0) * pl.num_programs(1) + pl.program_id(1) < WCAST_PARTS)
    def _():
        for src, dst in ((wa_ref, wa_out), (wb_ref, wb_out), (wc_ref, wc_out)):
            dst[...] = src[...].astype(dst.dtype)

    def deinterleave(i):
        for src, dst in ((q_ref, qd_ref), (k_ref, kd_ref), (v_ref, vd_ref)):
            for c in range(grp):
                dst[pl.ds(c * quarter + i * blk, blk), :] = (
                    src[0, pl.ds(c + grp * i * blk, blk, stride=grp), :])

    def finish(c):
        for i in range(per_res):
            rows = pl.ds(c * quarter + i * blk, blk)
            o_ref[0, pl.ds(c + grp * i * blk, blk, stride=grp), :] = acc_ref[rows, :] / l_ref[rows, :]

    first_head = lax.broadcasted_iota(jnp.int32, (blk, LANES), 1) < ATT_HEAD_DIM
    stats = (m_ref, l_ref, acc_ref)

    def gather(ref, slices):
        parts = [ref[sl, :] for sl in slices]
        return parts[0] if len(parts) == 1 else jnp.concatenate(parts, axis=0)

    def scatter(ref, slices, val):
        n = val.shape[0] // len(slices)
        for i, sl in enumerate(slices):
            ref[sl, :] = val[i * n:(i + 1) * n]

    def process(items, merge, before=None, after=None):
        def keys(ref, cur, prev):
            own = gather(ref, cur)
            if prev is not None:
                own = jnp.concatenate([gather(ref, prev), own], axis=0)
            return own.astype(BF16)

        def scores(item):
            cur, prev, bias = item
            qf = gather(qd_ref, cur)
            q2 = jnp.concatenate([jnp.where(first_head, qf, 0.0), jnp.where(first_head, 0.0, qf)],
                                 axis=0).astype(BF16)
            return lax.dot_general(q2, keys(kd_ref, cur, prev), (((1,), (1,)), ((), ())),
                                   preferred_element_type=F32) + bias

        def softmax(s):
            m2 = jnp.max(s, axis=-1, keepdims=True)
            p = jnp.exp2(s - m2)
            return m2, jnp.sum(p, axis=-1, keepdims=True), p.astype(BF16)

        def values(item, state):
            cur, prev, _ = item
            m2, l2, p = state
            return m2, l2, jnp.dot(p, keys(vd_ref, cur, prev), preferred_element_type=F32)

        def commit(item, state):
            cur = item[0]
            new = tuple(jnp.where(first_head, x[:blk], x[blk:]) for x in state)
            if merge:
                m_old, l_old, o_old = (gather(ref, cur) for ref in stats)
                m_new, l_new, o_new = new
                m_tot = jnp.maximum(m_old, m_new)
                a_old = jnp.exp2(m_old - m_tot)
                a_new = jnp.exp2(m_new - m_tot)
                new = (m_tot, l_old * a_old + l_new * a_new, o_old * a_old + o_new * a_new)
            for ref, val in zip(stats, new):
                scatter(ref, cur, val)

        stages = (lambda it, st: scores(it), lambda it, st: softmax(st), values, commit)
        state = [None] * len(items)
        for t in range(len(items) + len(stages) - 1):
            for k in reversed(range(len(stages))):
                i = t - k
                if 0 <= i < len(items):
                    if k == 0 and before and i in before:
                        before[i]()
                    state[i] = stages[k](items[i], state[i])
                    if k == len(stages) - 1 and after and i in after:
                        after[i]()

    def rows4(c, n):
        return [pl.ds(c * quarter + n * blk, blk)]

    deinterleave(0)
    process([(rows4(c, n), None, own_ref[0]) if n == 0
             else (rows4(c, n), rows4(c, n - 1), full_ref[0])
             for n in range(per_res) for c in range(grp)], merge=False,
            before={grp * (n - 1): (lambda n=n: deinterleave(n)) for n in range(1, per_res)})

    def rows1(n):
        return [pl.ds(c * quarter + n * slab, slab) for c in range(grp)]

    process([(rows1(0), None, own_ref[1])]
            + [(rows1(n), rows1(n - 1), full_ref[1]) for n in range(1, blocks)], merge=True)

    sub_blocks = seq // (DILATIONS[-1] * blk)

    def rows16(c, e, n):
        return [pl.ds(c * quarter + e + n * grp * blk, blk, stride=grp)]

    items = []
    for c in range(grp):
        for e in range(grp):
            items.append((rows16(c, e, 0), None, own_ref[0]))
            for n in range(1, sub_blocks):
                items.append((rows16(c, e, n), rows16(c, e, n - 1), full_ref[0]))
    per_quarter = grp * sub_blocks
    process(items, merge=True,
            after={(c + 1) * per_quarter - 1: (lambda c=c: finish(c)) for c in range(grp)})


def _attn_call(aq, ak, av, tail_weights):
    bsz, seq, w = aq.shape
    assert LANES == 2 * ATT_HEAD_DIM, "one grid step handles the two heads of a lane slab"
    pairs = w // LANES
    assert bsz * pairs >= WCAST_PARTS
    full, own = (jnp.asarray(x) for x in _attn_biases())
    blk = pl.BlockSpec((1, seq, LANES), lambda b, hp: (b, 0, hp))
    part = lambda b, hp: (jnp.minimum(b * pairs + hp, WCAST_PARTS - 1), 0)
    w_specs = [pl.BlockSpec((wt.shape[0] // WCAST_PARTS, wt.shape[1]), part) for wt in tail_weights]
    return pl.pallas_call(
        _attn_kernel,
        grid=(bsz, pairs),
        in_specs=[blk, blk, blk, _const_spec(full.shape), _const_spec(own.shape)] + w_specs,
        out_specs=[blk] + w_specs,
        out_shape=[jax.ShapeDtypeStruct((bsz, seq, w), F32)]
        + [jax.ShapeDtypeStruct(wt.shape, BF16) for wt in tail_weights],
        scratch_shapes=[pltpu.VMEM((seq, LANES), F32) for _ in range(6)],
        compiler_params=pltpu.CompilerParams(
            dimension_semantics=("arbitrary", "arbitrary"), vmem_limit_bytes=VMEM_LIMIT_BYTES),
        name="dilated_attn",
    )(aq, ak, av, full, own, *tail_weights)


def _tail_kernel(x_ref, hg_ref, att_ref, g1_ref, sh2_ref, sc2_ref, g2_ref,
                 an_ref, n2_ref, fin_ref, wo_ref, wgu_ref, wd_ref, o_ref):
    d_ff = wd_ref.shape[0]
    att = att_ref[...]
    ms = jnp.mean(att * att, axis=-1, keepdims=True)
    att_n = (att * lax.rsqrt(ms + RMS_EPS) * an_ref[...]).astype(BF16)
    mix = jnp.dot(hg_ref[...], wo_ref[:HG_WIDTH, :], preferred_element_type=F32)
    mix = mix + jnp.dot(att_n, wo_ref[HG_WIDTH:, :], preferred_element_type=F32)
    x1 = x_ref[...] + g1_ref[0] * mix

    ms = jnp.mean(x1 * x1, axis=-1, keepdims=True)
    gain = n2_ref[...] * (1.0 + sc2_ref[0])
    hb = (x1 * lax.rsqrt(ms + RMS_EPS) * gain + sh2_ref[0]).astype(BF16)
    ffn = jnp.zeros_like(x1)
    for j in range(d_ff // FF_CHUNK):
        lo = j * FF_CHUNK
        a = jnp.dot(hb, wgu_ref[:, lo:lo + FF_CHUNK], preferred_element_type=F32)
        u = jnp.dot(hb, wgu_ref[:, d_ff + lo:d_ff + lo + FF_CHUNK], preferred_element_type=F32)
        act = (_silu(a) * u).astype(BF16)
        ffn = ffn + jnp.dot(act, wd_ref[lo:lo + FF_CHUNK, :], preferred_element_type=F32)
    x2 = x1 + g2_ref[0] * ffn
    ms = jnp.mean(x2 * x2, axis=-1, keepdims=True)
    o_ref[...] = x2 * lax.rsqrt(ms + RMS_EPS) * fin_ref[...]


def _tail_call(x2, hg, att, mod3, att_g, norm2_g, final_g, wo_bf, wgu_bf, wd_bf, seq):
    t, d = x2.shape
    tm = ROW_TILE
    steps_per_batch = seq // tm
    row = lambda i: (i, 0)
    mod_col = lambda col: pl.BlockSpec((1, 1, d), lambda i: (i // steps_per_batch, 0, col))
    return pl.pallas_call(
        _tail_kernel,
        grid=(t // tm,),
        in_specs=[
            pl.BlockSpec((tm, d), row),
            pl.BlockSpec((tm, HG_WIDTH), row),
            pl.BlockSpec((tm, ATT_WIDTH), row),
            mod_col(2), mod_col(3), mod_col(4), mod_col(5),
            _const_spec((1, ATT_WIDTH)), _const_spec((1, d)), _const_spec((1, d)),
            _const_spec(wo_bf.shape), _const_spec(wgu_bf.shape), _const_spec(wd_bf.shape),
        ],
        out_specs=pl.BlockSpec((tm, d), row),
        out_shape=jax.ShapeDtypeStruct((t, d), F32),
        compiler_params=pltpu.CompilerParams(
            dimension_semantics=("arbitrary",), vmem_limit_bytes=VMEM_LIMIT_BYTES),
        name="outproj_ffn",
    )(x2, hg, att, mod3, mod3, mod3, mod3,
      att_g.reshape(1, ATT_WIDTH), norm2_g.reshape(1, d), final_g.reshape(1, d),
      wo_bf, wgu_bf, wd_bf)


def kernel(x, c, w_ada, b_ada, norm1_g, w_in, hg_lb_logits, hg_onorm_g, att_onorm_g,
           w_out, norm2_g, w_gate_up, w_down, final_g):
    bsz, seq, d = x.shape
    assert w_in.shape[0] == 1 and hg_lb_logits.shape[0] == 2, "single-layer block expected"
    assert seq % (DILATIONS[-1] * ATT_BLOCK) == 0 and seq % ROW_TILE == 0
    t = bsz * seq
    x2 = x.reshape(t, d)

    mod = _mod_call(c, w_ada[0], b_ada[0])
    mod3 = mod.reshape(bsz, 1, 6 * d)

    q, k, lf_hi, lf_lo, v, g, aq, ak, av, half_decay = _inproj_call(
        x2, mod3, norm1_g[0], w_in[0], hg_lb_logits, hg_onorm_g[0], seq)
    safe = (jnp.min(half_decay[:, :, 0].reshape(-1, HG_GROUP * HG_CHUNK // HG_HALF), axis=1)
            >= HG_SAFE_LOG2_DECAY).astype(jnp.int32)

    as_seq = lambda a: a.reshape(bsz, seq, a.shape[-1])
    hg = _hgrn_call(as_seq(q), as_seq(k), as_seq(lf_hi), as_seq(lf_lo), as_seq(v), as_seq(g), safe)
    att, wo_bf, wgu_bf, wd_bf = _attn_call(
        as_seq(aq), as_seq(ak), as_seq(av), (w_out[0], w_gate_up[0], w_down[0]))

    out = _tail_call(
        x2, hg.reshape(t, HG_WIDTH), att.reshape(t, ATT_WIDTH), mod3,
        att_onorm_g[0], norm2_g[0], final_g, wo_bf, wgu_bf, wd_bf, seq)
    return out.reshape(bsz, seq, d)
```

```python
import numpy as np
import jax
import jax.numpy as jnp
from jax import lax
from jax.experimental import pallas as pl
from jax.experimental.pallas import tpu as pltpu

F32 = jnp.float32
BF16 = jnp.bfloat16

LANES = 128
RMS_EPS = 1e-6
LOG2_E = 1.4426950408889634
PROJ_WIDTH = 512

HG_HEADS = 4
HG_DIM = 128
HG_WIDTH = HG_HEADS * HG_DIM
HG_CHUNK = 128
HG_LEVELS = 7
HG_HALF = HG_CHUNK // 2
HG_SAFE_LOG2_DECAY = -85.0
HG_GROUP = 16

ATT_HEAD_DIM = 64
ATT_WIDTH = 512
ATT_BLOCK = 128
DILATIONS = (1, 4, 16)
ATT_SPAN = 128
NEG_BIG = -1e30
ATT_GROUP = 4
WCAST_PARTS = 16

ROW_TILE = 512
FF_CHUNK = 256
WCAST_ROWS = 128
VMEM_LIMIT_BYTES = 56 * 1024 * 1024


def _silu(x):
    return x * jax.nn.sigmoid(x)


def _const_spec(shape):
    nd = len(shape)
    return pl.BlockSpec(shape, lambda *_: (0,) * nd, pipeline_mode=pl.Buffered(1))


def _mod_kernel(c_ref, w_ref, b_ref, o_ref):
    ca = _silu(c_ref[...])
    o_ref[...] = jnp.dot(ca, w_ref[...], preferred_element_type=F32) + b_ref[...]


def _mod_call(c, w_ada, b_ada):
    bsz, d = c.shape
    n = w_ada.shape[1]
    tn = d
    return pl.pallas_call(
        _mod_kernel,
        grid=(n // tn,),
        in_specs=[
            pl.BlockSpec((bsz, d), lambda j: (0, 0)),
            pl.BlockSpec((d, tn), lambda j: (0, j)),
            pl.BlockSpec((1, tn), lambda j: (0, j)),
        ],
        out_specs=pl.BlockSpec((bsz, tn), lambda j: (0, j)),
        out_shape=jax.ShapeDtypeStruct((bsz, n), F32),
        name="adaln_mod",
    )(c, w_ada, b_ada.reshape(1, n))


def _inproj_kernel(x_ref, sh_ref, sc_ref, g_ref, wf_ref, lbl_ref, on_ref,
                   q_o, lfh_o, lfl_o, v_o, g_o, aq_o, ak_o, av_o, dmin_o, w_ref):
    @pl.when(pl.program_id(0) == 0)
    def _():
        for r in range(0, wf_ref.shape[0], WCAST_ROWS):
            w_ref[r:r + WCAST_ROWS, :] = wf_ref[r:r + WCAST_ROWS, :].astype(BF16)

    x = x_ref[...]
    ms = jnp.mean(x * x, axis=-1, keepdims=True)
    gain = g_ref[...] * (1.0 + sc_ref[0])
    hb = (x * lax.rsqrt(ms + RMS_EPS) * gain + sh_ref[0]).astype(BF16)
    tm = hb.shape[0]

    def proj(j):
        return jnp.dot(hb, w_ref[:, j * PROJ_WIDTH:(j + 1) * PROJ_WIDTH],
                       preferred_element_type=F32)

    lbl = lbl_ref[...]
    e = jnp.exp(lbl - jnp.max(lbl, axis=0, keepdims=True))
    lb = e[0:1] / (e[0:1] + e[1:2])

    sg = jax.nn.sigmoid(proj(1))
    lf = jnp.log2(lb + (1.0 - lb) * sg)
    lf_hi = lf.astype(BF16)
    lfh_o[...] = lf_hi
    lfl_o[...] = (lf - lf_hi.astype(F32)).astype(BF16)
    ak_o[...] = proj(5)
    q_o[...] = _silu(proj(0)).astype(BF16)
    av_o[...] = proj(6)
    g_o[...] = (_silu(proj(3)) * on_ref[...]).astype(BF16)
    grp_id = lax.broadcasted_iota(jnp.int32, (tm // HG_HALF, tm), 0)
    row_id = lax.broadcasted_iota(jnp.int32, (tm // HG_HALF, tm), 1)
    member = (row_id // HG_HALF == grp_id).astype(BF16)
    totals = jnp.dot(member, lf_hi, preferred_element_type=F32)
    dmin_o[0] = jnp.broadcast_to(jnp.min(totals, axis=-1, keepdims=True), dmin_o.shape[1:])
    v_o[...] = proj(2).astype(BF16)
    aq_o[...] = proj(4) * (ATT_HEAD_DIM ** -0.5 * LOG2_E)


def _inproj_call(x2, mod3, norm_g, w_in, lb_logits, hg_gain, seq):
    t, d = x2.shape
    tm = ROW_TILE
    steps_per_batch = seq // tm
    row = lambda i: (i, 0)
    out_dtypes = (BF16, BF16, BF16, BF16, BF16, F32, F32, F32)
    return pl.pallas_call(
        _inproj_kernel,
        grid=(t // tm,),
        in_specs=[
            pl.BlockSpec((tm, d), row),
            pl.BlockSpec((1, 1, d), lambda i: (i // steps_per_batch, 0, 0)),
            pl.BlockSpec((1, 1, d), lambda i: (i // steps_per_batch, 0, 1)),
            _const_spec((1, d)),
            _const_spec(w_in.shape),
            _const_spec(lb_logits.shape),
            _const_spec((1, HG_WIDTH)),
        ],
        out_specs=[pl.BlockSpec((tm, PROJ_WIDTH), row) for _ in out_dtypes]
        + [pl.BlockSpec((1, tm // HG_HALF, LANES), lambda i: (i, 0, 0))],
        out_shape=[jax.ShapeDtypeStruct((t, PROJ_WIDTH), dt) for dt in out_dtypes]
        + [jax.ShapeDtypeStruct((t // tm, tm // HG_HALF, LANES), F32)],
        scratch_shapes=[pltpu.VMEM(w_in.shape, BF16)],
        compiler_params=pltpu.CompilerParams(
            dimension_semantics=("arbitrary",), vmem_limit_bytes=VMEM_LIMIT_BYTES),
        name="in_proj",
    )(x2, mod3, mod3, norm_g.reshape(1, d), w_in, lb_logits,
      jnp.tile(hg_gain.reshape(1, HG_DIM), (1, HG_HEADS)))


def _decay_selectors():
    c = HG_CHUNK
    t = np.arange(c)[:, None]
    u = np.arange(c)[None, :]
    blocks = []
    for l in range(HG_LEVELS):
        lo = (t >> l) << l
        hi = lo + (1 << l) - 1
        second = ((t >> l) & 1) == 1
        blocks.append(np.where(second, (u >= lo) & (u <= t), (u > t) & (u <= hi)))
    blocks.append(u <= t)
    blocks.append(u > t)
    return np.concatenate(blocks, axis=0).astype(np.float32)


def _hgrn_kernel(safe_ref, q_ref, lfh_ref, lfl_ref, v_ref, g_ref, p_ref, o_ref, st_ref):
    c = HG_CHUNK
    half = HG_HALF
    step = pl.program_id(0) * pl.num_programs(1) + pl.program_id(1)

    @pl.when(pl.program_id(1) == 0)
    def _():
        st_ref[...] = jnp.zeros_like(st_ref)

    def head(a, h):
        return a[:, h * HG_DIM:(h + 1) * HG_DIM]

    def gram(a, b):
        return lax.dot_general(a, b, (((1,), (1,)), ((), ())), preferred_element_type=F32)

    def log_decay(rows):
        return jnp.concatenate([lfh_ref[0, rows, :], lfl_ref[0, rows, :]], axis=0)

    def keys(rows):
        lf = lfh_ref[0, rows, :].astype(F32) + lfl_ref[0, rows, :].astype(F32)
        return (1.0 - jnp.exp2(lf)).astype(BF16)

    def exponents(lf2, block):
        return jnp.dot(p_ref[block * c:(block + 1) * c, :], lf2, preferred_element_type=F32)

    def scaled(x_bf, log2_scale):
        return x_bf * jnp.exp2(log2_scale).astype(BF16)

    def normed(rows, o, h):
        ms = jnp.mean(o * o, axis=-1, keepdims=True)
        return (o * lax.rsqrt(ms + RMS_EPS)).astype(BF16) * head(g_ref[0, rows, :], h)

    safe = safe_ref[step] > 0

    @pl.when(safe)
    def _():
        causal = (lax.broadcasted_iota(jnp.int32, (c, c), 1)
                  <= lax.broadcasted_iota(jnp.int32, (c, c), 0))
        heads = range(HG_HEADS)

        def decay(j):
            rows = slice(j * c, (j + 1) * c)
            return rows, exponents(log_decay(rows), HG_LEVELS)

        def scale(st):
            rows, b = st
            b_mid = b[half - 1:half, :]
            q = q_ref[0, rows, :]
            k = keys(rows)
            return dict(rows=rows, b_mid=b_mid, b_last=b[c - 1:c, :], qe=scaled(q, b),
                        q_mid=scaled(q, b - b_mid), k_mid=scaled(k, b_mid - b))

        def score(st):
            st["s"] = [jnp.where(causal, gram(head(st["q_mid"], h), head(st["k_mid"], h)),
                                 0.0).astype(BF16) for h in heads]
            return st

        def apply(st):
            v = v_ref[0, st["rows"], :]
            st["intra"] = [jnp.dot(st["s"][h], head(v, h), preferred_element_type=F32)
                           for h in heads]
            tail_decay = jnp.exp2(st["b_last"] - st["b_mid"])
            st["kv"] = [lax.dot_general(head(v, h), head(st["k_mid"], h),
                                        (((0,), (0,)), ((), ())), preferred_element_type=F32)
                        * head(tail_decay, h) for h in heads]
            return st

        states = [st_ref[h] for h in heads]

        def recur(st):
            chunk_decay = jnp.exp2(st["b_last"])
            outs = []
            for h in heads:
                o = st["intra"][h] + gram(head(st["qe"], h), states[h].astype(BF16))
                states[h] = states[h] * head(chunk_decay, h) + st["kv"][h]
                outs.append(normed(st["rows"], o, h))
            o_ref[0, st["rows"], :] = jnp.concatenate(outs, axis=1).astype(o_ref.dtype)
            return None

        stages = (scale, score, apply, recur)
        live = [None] * HG_GROUP
        for t in range(HG_GROUP + len(stages)):
            for kk in reversed(range(len(stages) + 1)):
                j = t - kk
                if 0 <= j < HG_GROUP:
                    live[j] = decay(j) if kk == 0 else stages[kk - 1](live[j])
        for h in heads:
            st_ref[h] = states[h]


    @pl.when(jnp.logical_not(safe))
    def _():
        row = lax.broadcasted_iota(jnp.int32, (c, HG_WIDTH), 0)
        ti = lax.broadcasted_iota(jnp.int32, (c, c), 0)
        si = lax.broadcasted_iota(jnp.int32, (c, c), 1)
        txs = ti ^ si

        def one_chunk(j, carry):
            rows = pl.ds(pl.multiple_of(j * c, c), c)
            q = q_ref[0, rows, :]
            k = keys(rows)
            v = v_ref[0, rows, :]
            lf = log_decay(rows)
            scores = [jnp.where(ti == si, gram(head(q, h), head(k, h)), 0.0)
                      for h in range(HG_HEADS)]
            for l in range(HG_LEVELS):
                second = ((row >> l) & 1) == 1
                xl = scaled(jnp.where(second, q, k), exponents(lf, l))
                mask = (ti > si) & ((txs >> l) == 1)
                for h in range(HG_HEADS):
                    xh = head(xl, h)
                    scores[h] = jnp.where(mask, gram(xh, xh), scores[h])
            b = exponents(lf, HG_LEVELS)
            qe = scaled(q, b)
            k_end = scaled(k, exponents(lf, HG_LEVELS + 1))
            chunk_decay = jnp.exp2(b[c - 1:c, :])
            outs = []
            for h in range(HG_HEADS):
                st = st_ref[h]
                o = jnp.dot(scores[h].astype(BF16), head(v, h), preferred_element_type=F32)
                o = o + gram(head(qe, h), st.astype(BF16))
                st_ref[h] = st * head(chunk_decay, h) + lax.dot_general(
                    head(v, h), head(k_end, h), (((0,), (0,)), ((), ())),
                    preferred_element_type=F32)
                outs.append(normed(rows, o, h))
            o_ref[0, rows, :] = jnp.concatenate(outs, axis=1).astype(o_ref.dtype)
            return carry

        lax.fori_loop(0, HG_GROUP, one_chunk, 0)


def _hgrn_call(q, lf_hi, lf_lo, v, g, safe):
    bsz, seq, w = q.shape
    rows = HG_GROUP * HG_CHUNK
    sel = _decay_selectors()
    sel = jnp.asarray(np.concatenate([sel, sel], axis=1), dtype=BF16)
    blk = pl.BlockSpec((1, rows, w), lambda b, i, safe_ref: (b, i, 0))
    return pl.pallas_call(
        _hgrn_kernel,
        grid_spec=pltpu.PrefetchScalarGridSpec(
            num_scalar_prefetch=1,
            grid=(bsz, seq // rows),
            in_specs=[blk, blk, blk, blk, blk, _const_spec(sel.shape)],
            out_specs=blk,
            scratch_shapes=[pltpu.VMEM((HG_HEADS, HG_DIM, HG_DIM), F32)],
        ),
        out_shape=jax.ShapeDtypeStruct((bsz, seq, w), BF16),
        compiler_params=pltpu.CompilerParams(
            dimension_semantics=("arbitrary", "arbitrary"), vmem_limit_bytes=VMEM_LIMIT_BYTES),
        name="hgrn2",
    )(safe, q, lf_hi, lf_lo, v, g, sel)


def _attn_biases():
    blk, grp = ATT_BLOCK, ATT_GROUP
    slab = blk // grp
    a = np.arange(blk)
    unslab = grp * (a % slab) + a // slab
    full, own = [], []
    for pos in (a, unslab):
        kpos = np.concatenate([pos, pos + blk])
        dist = blk + pos[:, None] - kpos[None, :]
        keep = (dist >= 0) & (dist <= ATT_SPAN)
        full.append(np.where(np.concatenate([keep, keep], axis=0), 0.0, NEG_BIG))
        keep_own = keep[:, blk:]
        own.append(np.where(np.concatenate([keep_own, keep_own], axis=0), 0.0, NEG_BIG))
    return np.stack(full).astype(np.float32), np.stack(own).astype(np.float32)


def _attn_kernel(q_ref, k_ref, v_ref, full_ref, own_ref, wa_ref, wb_ref, wc_ref,
                 o_ref, wa_out, wb_out, wc_out,
                 qd_ref, kd_ref, vd_ref, acc_ref, m_ref, l_ref):
    seq = q_ref.shape[1]
    blk = ATT_BLOCK
    grp = ATT_GROUP
    quarter = seq // grp
    slab = blk // grp
    blocks = seq // blk
    per_res = quarter // blk

    @pl.when(pl.program_id(0) * pl.num_programs(1) + pl.program_id(1) < WCAST_PARTS)
    def _():
        for src, dst in ((wa_ref, wa_out), (wb_ref, wb_out), (wc_ref, wc_out)):
            dst[...] = src[...].astype(dst.dtype)

    def deinterleave(i):
        for src, dst in ((q_ref, qd_ref), (k_ref, kd_ref), (v_ref, vd_ref)):
            for c in range(grp):
                dst[pl.ds(c * quarter + i * blk, blk), :] = (
                    src[0, pl.ds(c + grp * i * blk, blk, stride=grp), :])

    def finish(c):
        for i in range(per_res):
            rows = pl.ds(c * quarter + i * blk, blk)
            o_ref[0, pl.ds(c + grp * i * blk, blk, stride=grp), :] = acc_ref[rows, :] / l_ref[rows, :]

    first_head = lax.broadcasted_iota(jnp.int32, (blk, LANES), 1) < ATT_HEAD_DIM
    stats = (m_ref, l_ref, acc_ref)

    def gather(ref, slices):
        parts = [ref[sl, :] for sl in slices]
        return parts[0] if len(parts) == 1 else jnp.concatenate(parts, axis=0)

    def scatter(ref, slices, val):
        n = val.shape[0] // len(slices)
        for i, sl in enumerate(slices):
            ref[sl, :] = val[i * n:(i + 1) * n]

    def process(items, merge, before=None, after=None):
        def keys(ref, cur, prev):
            own = gather(ref, cur)
            if prev is not None:
                own = jnp.concatenate([gather(ref, prev), own], axis=0)
            return own.astype(BF16)

        def scores(item):
            cur, prev, bias = item
            qf = gather(qd_ref, cur)
            q2 = jnp.concatenate([jnp.where(first_head, qf, 0.0), jnp.where(first_head, 0.0, qf)],
                                 axis=0).astype(BF16)
            return lax.dot_general(q2, keys(kd_ref, cur, prev), (((1,), (1,)), ((), ())),
                                   preferred_element_type=F32) + bias

        def softmax(s):
            m2 = jnp.max(s, axis=-1, keepdims=True)
            p = jnp.exp2(s - m2)
            return m2, jnp.sum(p, axis=-1, keepdims=True), p.astype(BF16)

        def values(item, state):
            cur, prev, _ = item
            m2, l2, p = state
            return m2, l2, jnp.dot(p, keys(vd_ref, cur, prev), preferred_element_type=F32)

        def commit(item, state):
            cur = item[0]
            new = tuple(jnp.where(first_head, x[:blk], x[blk:]) for x in state)
            if merge:
                m_old, l_old, o_old = (gather(ref, cur) for ref in stats)
                m_new, l_new, o_new = new
                m_tot = jnp.maximum(m_old, m_new)
                a_old = jnp.exp2(m_old - m_tot)
                a_new = jnp.exp2(m_new - m_tot)
                new = (m_tot, l_old * a_old + l_new * a_new, o_old * a_old + o_new * a_new)
            for ref, val in zip(stats, new):
                scatter(ref, cur, val)

        stages = (lambda it, st: scores(it), lambda it, st: softmax(st), values, commit)
        state = [None] * len(items)
        for t in range(len(items) + len(stages) - 1):
            for k in reversed(range(len(stages))):
                i = t - k
                if 0 <= i < len(items):
                    if k == 0 and before and i in before:
                        before[i]()
                    state[i] = stages[k](items[i], state[i])
                    if k == len(stages) - 1 and after and i in after:
                        after[i]()

    def rows4(c, n):
        return [pl.ds(c * quarter + n * blk, blk)]

    deinterleave(0)
    process([(rows4(c, n), None, own_ref[0]) if n == 0
             else (rows4(c, n), rows4(c, n - 1), full_ref[0])
             for n in range(per_res) for c in range(grp)], merge=False,
            before={grp * (n - 1): (lambda n=n: deinterleave(n)) for n in range(1, per_res)})

    def rows1(n):
        return [pl.ds(c * quarter + n * slab, slab) for c in range(grp)]

    process([(rows1(0), None, own_ref[1])]
            + [(rows1(n), rows1(n - 1), full_ref[1]) for n in range(1, blocks)], merge=True)

    sub_blocks = seq // (DILATIONS[-1] * blk)

    def rows16(c, e, n):
        return [pl.ds(c * quarter + e + n * grp * blk, blk, stride=grp)]

    items = []
    for c in range(grp):
        for e in range(grp):
            items.append((rows16(c, e, 0), None, own_ref[0]))
            for n in range(1, sub_blocks):
                items.append((rows16(c, e, n), rows16(c, e, n - 1), full_ref[0]))
    per_quarter = grp * sub_blocks
    process(items, merge=True,
            after={(c + 1) * per_quarter - 1: (lambda c=c: finish(c)) for c in range(grp)})


def _attn_call(aq, ak, av, tail_weights):
    bsz, seq, w = aq.shape
    assert LANES == 2 * ATT_HEAD_DIM, "one grid step handles the two heads of a lane slab"
    pairs = w // LANES
    assert bsz * pairs >= WCAST_PARTS
    full, own = (jnp.asarray(x) for x in _attn_biases())
    blk = pl.BlockSpec((1, seq, LANES), lambda b, hp: (b, 0, hp))
    part = lambda b, hp: (jnp.minimum(b * pairs + hp, WCAST_PARTS - 1), 0)
    w_specs = [pl.BlockSpec((wt.shape[0] // WCAST_PARTS, wt.shape[1]), part) for wt in tail_weights]
    return pl.pallas_call(
        _attn_kernel,
        grid=(bsz, pairs),
        in_specs=[blk, blk, blk, _const_spec(full.shape), _const_spec(own.shape)] + w_specs,
        out_specs=[blk] + w_specs,
        out_shape=[jax.ShapeDtypeStruct((bsz, seq, w), F32)]
        + [jax.ShapeDtypeStruct(wt.shape, BF16) for wt in tail_weights],
        scratch_shapes=[pltpu.VMEM((seq, LANES), F32) for _ in range(6)],
        compiler_params=pltpu.CompilerParams(
            dimension_semantics=("arbitrary", "arbitrary"), vmem_limit_bytes=VMEM_LIMIT_BYTES),
        name="dilated_attn",
    )(aq, ak, av, full, own, *tail_weights)


def _tail_kernel(x_ref, hg_ref, att_ref, g1_ref, sh2_ref, sc2_ref, g2_ref,
                 an_ref, n2_ref, fin_ref, wo_ref, wgu_ref, wd_ref, o_ref):
    d_ff = wd_ref.shape[0]
    att = att_ref[...]
    ms = jnp.mean(att * att, axis=-1, keepdims=True)
    att_n = (att * lax.rsqrt(ms + RMS_EPS) * an_ref[...]).astype(BF16)
    mix = jnp.dot(hg_ref[...], wo_ref[:HG_WIDTH, :], preferred_element_type=F32)
    mix = mix + jnp.dot(att_n, wo_ref[HG_WIDTH:, :], preferred_element_type=F32)
    x1 = x_ref[...] + g1_ref[0] * mix

    ms = jnp.mean(x1 * x1, axis=-1, keepdims=True)
    gain = n2_ref[...] * (1.0 + sc2_ref[0])
    hb = (x1 * lax.rsqrt(ms + RMS_EPS) * gain + sh2_ref[0]).astype(BF16)
    ffn = jnp.zeros_like(x1)
    for j in range(d_ff // FF_CHUNK):
        lo = j * FF_CHUNK
        a = jnp.dot(hb, wgu_ref[:, lo:lo + FF_CHUNK], preferred_element_type=F32)
        u = jnp.dot(hb, wgu_ref[:, d_ff + lo:d_ff + lo + FF_CHUNK], preferred_element_type=F32)
        act = (_silu(a) * u).astype(BF16)
        ffn = ffn + jnp.dot(act, wd_ref[lo:lo + FF_CHUNK, :], preferred_element_type=F32)
    x2 = x1 + g2_ref[0] * ffn
    ms = jnp.mean(x2 * x2, axis=-1, keepdims=True)
    o_ref[...] = x2 * lax.rsqrt(ms + RMS_EPS) * fin_ref[...]


def _tail_call(x2, hg, att, mod3, att_g, norm2_g, final_g, wo_bf, wgu_bf, wd_bf, seq):
    t, d = x2.shape
    tm = ROW_TILE
    steps_per_batch = seq // tm
    row = lambda i: (i, 0)
    mod_col = lambda col: pl.BlockSpec((1, 1, d), lambda i: (i // steps_per_batch, 0, col))
    return pl.pallas_call(
        _tail_kernel,
        grid=(t // tm,),
        in_specs=[
            pl.BlockSpec((tm, d), row),
            pl.BlockSpec((tm, HG_WIDTH), row),
            pl.BlockSpec((tm, ATT_WIDTH), row),
            mod_col(2), mod_col(3), mod_col(4), mod_col(5),
            _const_spec((1, ATT_WIDTH)), _const_spec((1, d)), _const_spec((1, d)),
            _const_spec(wo_bf.shape), _const_spec(wgu_bf.shape), _const_spec(wd_bf.shape),
        ],
        out_specs=pl.BlockSpec((tm, d), row),
        out_shape=jax.ShapeDtypeStruct((t, d), F32),
        compiler_params=pltpu.CompilerParams(
            dimension_semantics=("arbitrary",), vmem_limit_bytes=VMEM_LIMIT_BYTES),
        name="outproj_ffn",
    )(x2, hg, att, mod3, mod3, mod3, mod3,
      att_g.reshape(1, ATT_WIDTH), norm2_g.reshape(1, d), final_g.reshape(1, d),
      wo_bf, wgu_bf, wd_bf)


def kernel(x, c, w_ada, b_ada, norm1_g, w_in, hg_lb_logits, hg_onorm_g, att_onorm_g,
           w_out, norm2_g, w_gate_up, w_down, final_g):
    bsz, seq, d = x.shape
    assert w_in.shape[0] == 1 and hg_lb_logits.shape[0] == 2, "single-layer block expected"
    assert seq % (DILATIONS[-1] * ATT_BLOCK) == 0 and seq % ROW_TILE == 0
    t = bsz * seq
    x2 = x.reshape(t, d)

    mod = _mod_call(c, w_ada[0], b_ada[0])
    mod3 = mod.reshape(bsz, 1, 6 * d)

    q, lf_hi, lf_lo, v, g, aq, ak, av, half_decay = _inproj_call(
        x2, mod3, norm1_g[0], w_in[0], hg_lb_logits, hg_onorm_g[0], seq)
    safe = (jnp.min(half_decay[:, :, 0].reshape(-1, HG_GROUP * HG_CHUNK // HG_HALF), axis=1)
            >= HG_SAFE_LOG2_DECAY).astype(jnp.int32)

    as_seq = lambda a: a.reshape(bsz, seq, a.shape[-1])
    hg = _hgrn_call(as_seq(q), as_seq(lf_hi), as_seq(lf_lo), as_seq(v), as_seq(g), safe)
    att, wo_bf, wgu_bf, wd_bf = _attn_call(
        as_seq(aq), as_seq(ak), as_seq(av), (w_out[0], w_gate_up[0], w_down[0]))

    out = _tail_call(
        x2, hg.reshape(t, HG_WIDTH), att.reshape(t, ATT_WIDTH), mod3,
        att_onorm_g[0], norm2_g[0], final_g, wo_bf, wgu_bf, wd_bf, seq)
    return out.reshape(bsz, seq, d)
```

```python
import numpy as np
import jax
import jax.numpy as jnp
from jax import lax
from jax.experimental import pallas as pl
from jax.experimental.pallas import tpu as pltpu

F32 = jnp.float32
BF16 = jnp.bfloat16

LANES = 128
RMS_EPS = 1e-6
LOG2_E = 1.4426950408889634
PROJ_WIDTH = 512

HG_HEADS = 4
HG_DIM = 128
HG_WIDTH = HG_HEADS * HG_DIM
HG_CHUNK = 128
HG_LEVELS = 7
HG_HALF = HG_CHUNK // 2
HG_SAFE_LOG2_DECAY = -85.0
HG_GROUP = 16

ATT_HEAD_DIM = 64
ATT_WIDTH = 512
ATT_BLOCK = 128
DILATIONS = (1, 4, 16)
ATT_SPAN = 128
NEG_BIG = -1e30
ATT_GROUP = 4
WCAST_PARTS = 16

ROW_TILE = 512
FF_CHUNK = 256
WCAST_ROWS = 128
VMEM_LIMIT_BYTES = 56 * 1024 * 1024


def _silu(x):
    return x * jax.nn.sigmoid(x)


def _const_spec(shape):
    nd = len(shape)
    return pl.BlockSpec(shape, lambda *_: (0,) * nd, pipeline_mode=pl.Buffered(1))


def _mod_kernel(c_ref, w_ref, b_ref, o_ref):
    ca = _silu(c_ref[...])
    o_ref[...] = jnp.dot(ca, w_ref[...], preferred_element_type=F32) + b_ref[...]


def _mod_call(c, w_ada, b_ada):
    bsz, d = c.shape
    n = w_ada.shape[1]
    tn = d
    return pl.pallas_call(
        _mod_kernel,
        grid=(n // tn,),
        in_specs=[
            pl.BlockSpec((bsz, d), lambda j: (0, 0)),
            pl.BlockSpec((d, tn), lambda j: (0, j)),
            pl.BlockSpec((1, tn), lambda j: (0, j)),
        ],
        out_specs=pl.BlockSpec((bsz, tn), lambda j: (0, j)),
        out_shape=jax.ShapeDtypeStruct((bsz, n), F32),
        name="adaln_mod",
    )(c, w_ada, b_ada.reshape(1, n))


def _inproj_kernel(x_ref, sh_ref, sc_ref, g_ref, wf_ref, lbl_ref, on_ref,
                   q_o, k_o, lfh_o, lfl_o, v_o, g_o, aq_o, ak_o, av_o, dmin_o, w_ref):
    @pl.when(pl.program_id(0) == 0)
    def _():
        for r in range(0, wf_ref.shape[0], WCAST_ROWS):
            w_ref[r:r + WCAST_ROWS, :] = wf_ref[r:r + WCAST_ROWS, :].astype(BF16)

    x = x_ref[...]
    ms = jnp.mean(x * x, axis=-1, keepdims=True)
    gain = g_ref[...] * (1.0 + sc_ref[0])
    hb = (x * lax.rsqrt(ms + RMS_EPS) * gain + sh_ref[0]).astype(BF16)
    tm = hb.shape[0]

    def proj(j):
        return jnp.dot(hb, w_ref[:, j * PROJ_WIDTH:(j + 1) * PROJ_WIDTH],
                       preferred_element_type=F32)

    lbl = lbl_ref[...]
    e = jnp.exp(lbl - jnp.max(lbl, axis=0, keepdims=True))
    lb = e[0:1] / (e[0:1] + e[1:2])

    sg = jax.nn.sigmoid(proj(1))
    lf = jnp.log2(lb + (1.0 - lb) * sg)
    lf_hi = lf.astype(BF16)
    lfh_o[...] = lf_hi
    lfl_o[...] = (lf - lf_hi.astype(F32)).astype(BF16)
    k_o[...] = ((1.0 - lb) * (1.0 - sg)).astype(BF16)
    ak_o[...] = proj(5)
    q_o[...] = _silu(proj(0)).astype(BF16)
    av_o[...] = proj(6)
    g_o[...] = (_silu(proj(3)) * on_ref[...]).astype(BF16)
    grp_id = lax.broadcasted_iota(jnp.int32, (tm // HG_HALF, tm), 0)
    row_id = lax.broadcasted_iota(jnp.int32, (tm // HG_HALF, tm), 1)
    member = (row_id // HG_HALF == grp_id).astype(BF16)
    totals = jnp.dot(member, lf_hi, preferred_element_type=F32)
    dmin_o[0] = jnp.broadcast_to(jnp.min(totals, axis=-1, keepdims=True), dmin_o.shape[1:])
    v_o[...] = proj(2).astype(BF16)
    aq_o[...] = proj(4) * (ATT_HEAD_DIM ** -0.5 * LOG2_E)


def _inproj_call(x2, mod3, norm_g, w_in, lb_logits, hg_gain, seq):
    t, d = x2.shape
    tm = ROW_TILE
    steps_per_batch = seq // tm
    row = lambda i: (i, 0)
    out_dtypes = (BF16, BF16, BF16, BF16, BF16, BF16, F32, F32, F32)
    return pl.pallas_call(
        _inproj_kernel,
        grid=(t // tm,),
        in_specs=[
            pl.BlockSpec((tm, d), row),
            pl.BlockSpec((1, 1, d), lambda i: (i // steps_per_batch, 0, 0)),
            pl.BlockSpec((1, 1, d), lambda i: (i // steps_per_batch, 0, 1)),
            _const_spec((1, d)),
            _const_spec(w_in.shape),
            _const_spec(lb_logits.shape),
            _const_spec((1, HG_WIDTH)),
        ],
        out_specs=[pl.BlockSpec((tm, PROJ_WIDTH), row) for _ in out_dtypes]
        + [pl.BlockSpec((1, tm // HG_HALF, LANES), lambda i: (i, 0, 0))],
        out_shape=[jax.ShapeDtypeStruct((t, PROJ_WIDTH), dt) for dt in out_dtypes]
        + [jax.ShapeDtypeStruct((t // tm, tm // HG_HALF, LANES), F32)],
        scratch_shapes=[pltpu.VMEM(w_in.shape, BF16)],
        compiler_params=pltpu.CompilerParams(
            dimension_semantics=("arbitrary",), vmem_limit_bytes=VMEM_LIMIT_BYTES),
        name="in_proj",
    )(x2, mod3, mod3, norm_g.reshape(1, d), w_in, lb_logits,
      jnp.tile(hg_gain.reshape(1, HG_DIM), (1, HG_HEADS)))


def _decay_selectors():
    c = HG_CHUNK
    t = np.arange(c)[:, None]
    u = np.arange(c)[None, :]
    blocks = []
    for l in range(HG_LEVELS):
        lo = (t >> l) << l
        hi = lo + (1 << l) - 1
        second = ((t >> l) & 1) == 1
        blocks.append(np.where(second, (u >= lo) & (u <= t), (u > t) & (u <= hi)))
    blocks.append(u <= t)
    blocks.append(u > t)
    return np.concatenate(blocks, axis=0).astype(np.float32)


def _hgrn_kernel(safe_ref, q_ref, k_ref, lfh_ref, lfl_ref, v_ref, g_ref, p_ref, o_ref, st_ref):
    c = HG_CHUNK
    half = HG_HALF
    step = pl.program_id(0) * pl.num_programs(1) + pl.program_id(1)

    @pl.when(pl.program_id(1) == 0)
    def _():
        st_ref[...] = jnp.zeros_like(st_ref)

    def head(a, h):
        return a[:, h * HG_DIM:(h + 1) * HG_DIM]

    def gram(a, b):
        return lax.dot_general(a, b, (((1,), (1,)), ((), ())), preferred_element_type=F32)

    def log_decay(rows):
        return jnp.concatenate([lfh_ref[0, rows, :], lfl_ref[0, rows, :]], axis=0)

    def exponents(lf2, block):
        return jnp.dot(p_ref[block * c:(block + 1) * c, :], lf2, preferred_element_type=F32)

    def scaled(x_bf, log2_scale):
        return x_bf * jnp.exp2(log2_scale).astype(BF16)

    def normed(rows, o, h):
        ms = jnp.mean(o * o, axis=-1, keepdims=True)
        return (o * lax.rsqrt(ms + RMS_EPS)).astype(BF16) * head(g_ref[0, rows, :], h)

    safe = safe_ref[step] > 0

    @pl.when(safe)
    def _():
        causal = (lax.broadcasted_iota(jnp.int32, (c, c), 1)
                  <= lax.broadcasted_iota(jnp.int32, (c, c), 0))
        heads = range(HG_HEADS)

        def decay(j):
            rows = slice(j * c, (j + 1) * c)
            return rows, exponents(log_decay(rows), HG_LEVELS)

        def scale(st):
            rows, b = st
            b_mid = b[half - 1:half, :]
            q = q_ref[0, rows, :]
            k = k_ref[0, rows, :]
            return dict(rows=rows, b_mid=b_mid, b_last=b[c - 1:c, :], qe=scaled(q, b),
                        q_mid=scaled(q, b - b_mid), k_mid=scaled(k, b_mid - b))

        def score(st):
            st["s"] = [jnp.where(causal, gram(head(st["q_mid"], h), head(st["k_mid"], h)),
                                 0.0).astype(BF16) for h in heads]
            return st

        def apply(st):
            v = v_ref[0, st["rows"], :]
            st["intra"] = [jnp.dot(st["s"][h], head(v, h), preferred_element_type=F32)
                           for h in heads]
            tail_decay = jnp.exp2(st["b_last"] - st["b_mid"])
            st["kv"] = [lax.dot_general(head(v, h), head(st["k_mid"], h),
                                        (((0,), (0,)), ((), ())), preferred_element_type=F32)
                        * head(tail_decay, h) for h in heads]
            return st

        states = [st_ref[h] for h in heads]

        def recur(st):
            chunk_decay = jnp.exp2(st["b_last"])
            outs = []
            for h in heads:
                o = st["intra"][h] + gram(head(st["qe"], h), states[h].astype(BF16))
                states[h] = states[h] * head(chunk_decay, h) + st["kv"][h]
                outs.append(normed(st["rows"], o, h))
            o_ref[0, st["rows"], :] = jnp.concatenate(outs, axis=1).astype(o_ref.dtype)
            return None

        stages = (scale, lambda st: apply(score(st)), recur)
        live = [None] * HG_GROUP
        for t in range(HG_GROUP + len(stages)):
            for kk in reversed(range(len(stages) + 1)):
                j = t - kk
                if 0 <= j < HG_GROUP:
                    live[j] = decay(j) if kk == 0 else stages[kk - 1](live[j])
        for h in heads:
            st_ref[h] = states[h]


    @pl.when(jnp.logical_not(safe))
    def _():
        row = lax.broadcasted_iota(jnp.int32, (c, HG_WIDTH), 0)
        ti = lax.broadcasted_iota(jnp.int32, (c, c), 0)
        si = lax.broadcasted_iota(jnp.int32, (c, c), 1)
        txs = ti ^ si

        def one_chunk(j, carry):
            rows = pl.ds(pl.multiple_of(j * c, c), c)
            q = q_ref[0, rows, :]
            k = k_ref[0, rows, :]
            v = v_ref[0, rows, :]
            lf = log_decay(rows)
            scores = [jnp.where(ti == si, gram(head(q, h), head(k, h)), 0.0)
                      for h in range(HG_HEADS)]
            for l in range(HG_LEVELS):
                second = ((row >> l) & 1) == 1
                xl = scaled(jnp.where(second, q, k), exponents(lf, l))
                mask = (ti > si) & ((txs >> l) == 1)
                for h in range(HG_HEADS):
                    xh = head(xl, h)
                    scores[h] = jnp.where(mask, gram(xh, xh), scores[h])
            b = exponents(lf, HG_LEVELS)
            qe = scaled(q, b)
            k_end = scaled(k, exponents(lf, HG_LEVELS + 1))
            chunk_decay = jnp.exp2(b[c - 1:c, :])
            outs = []
            for h in range(HG_HEADS):
                st = st_ref[h]
                o = jnp.dot(scores[h].astype(BF16), head(v, h), preferred_element_type=F32)
                o = o + gram(head(qe, h), st.astype(BF16))
                st_ref[h] = st * head(chunk_decay, h) + lax.dot_general(
                    head(v, h), head(k_end, h), (((0,), (0,)), ((), ())),
                    preferred_element_type=F32)
                outs.append(normed(rows, o, h))
            o_ref[0, rows, :] = jnp.concatenate(outs, axis=1).astype(o_ref.dtype)
            return carry

        lax.fori_loop(0, HG_GROUP, one_chunk, 0)


def _hgrn_call(q, k, lf_hi, lf_lo, v, g, safe):
    bsz, seq, w = q.shape
    rows = HG_GROUP * HG_CHUNK
    sel = _decay_selectors()
    sel = jnp.asarray(np.concatenate([sel, sel], axis=1), dtype=BF16)
    blk = pl.BlockSpec((1, rows, w), lambda b, i, safe_ref: (b, i, 0))
    return pl.pallas_call(
        _hgrn_kernel,
        grid_spec=pltpu.PrefetchScalarGridSpec(
            num_scalar_prefetch=1,
            grid=(bsz, seq // rows),
            in_specs=[blk, blk, blk, blk, blk, blk, _const_spec(sel.shape)],
            out_specs=blk,
            scratch_shapes=[pltpu.VMEM((HG_HEADS, HG_DIM, HG_DIM), F32)],
        ),
        out_shape=jax.ShapeDtypeStruct((bsz, seq, w), BF16),
        compiler_params=pltpu.CompilerParams(
            dimension_semantics=("arbitrary", "arbitrary"), vmem_limit_bytes=VMEM_LIMIT_BYTES),
        name="hgrn2",
    )(safe, q, k, lf_hi, lf_lo, v, g, sel)


def _attn_biases():
    blk, grp = ATT_BLOCK, ATT_GROUP
    slab = blk // grp
    a = np.arange(blk)
    unslab = grp * (a % slab) + a // slab
    full, own = [], []
    for pos in (a, unslab):
        kpos = np.concatenate([pos, pos + blk])
        dist = blk + pos[:, None] - kpos[None, :]
        keep = (dist >= 0) & (dist <= ATT_SPAN)
        full.append(np.where(np.concatenate([keep, keep], axis=0), 0.0, NEG_BIG))
        keep_own = keep[:, blk:]
        own.append(np.where(np.concatenate([keep_own, keep_own], axis=0), 0.0, NEG_BIG))
    return np.stack(full).astype(np.float32), np.stack(own).astype(np.float32)


def _attn_kernel(q_ref, k_ref, v_ref, full_ref, own_ref, wa_ref, wb_ref, wc_ref,
                 o_ref, wa_out, wb_out, wc_out,
                 qd_ref, kd_ref, vd_ref, acc_ref, m_ref, l_ref):
    seq = q_ref.shape[1]
    blk = ATT_BLOCK
    grp = ATT_GROUP
    quarter = seq // grp
    slab = blk // grp
    blocks = seq // blk
    per_res = quarter // blk

    @pl.when(pl.program_id(0) * pl.num_programs(1) + pl.program_id(1) < WCAST_PARTS)
    def _():
        for src, dst in ((wa_ref, wa_out), (wb_ref, wb_out), (wc_ref, wc_out)):
            dst[...] = src[...].astype(dst.dtype)

    def deinterleave(i):
        for src, dst in ((q_ref, qd_ref), (k_ref, kd_ref), (v_ref, vd_ref)):
            for c in range(grp):
                dst[pl.ds(c * quarter + i * blk, blk), :] = (
                    src[0, pl.ds(c + grp * i * blk, blk, stride=grp), :])

    def finish(c):
        for i in range(per_res):
            rows = pl.ds(c * quarter + i * blk, blk)
            o_ref[0, pl.ds(c + grp * i * blk, blk, stride=grp), :] = acc_ref[rows, :] / l_ref[rows, :]

    first_head = lax.broadcasted_iota(jnp.int32, (blk, LANES), 1) < ATT_HEAD_DIM
    stats = (m_ref, l_ref, acc_ref)

    def gather(ref, slices):
        parts = [ref[sl, :] for sl in slices]
        return parts[0] if len(parts) == 1 else jnp.concatenate(parts, axis=0)

    def scatter(ref, slices, val):
        n = val.shape[0] // len(slices)
        for i, sl in enumerate(slices):
            ref[sl, :] = val[i * n:(i + 1) * n]

    def process(items, merge, before=None, after=None):
        def keys(ref, cur, prev):
            own = gather(ref, cur)
            if prev is not None:
                own = jnp.concatenate([gather(ref, prev), own], axis=0)
            return own.astype(BF16)

        def scores(item):
            cur, prev, bias = item
            qf = gather(qd_ref, cur)
            q2 = jnp.concatenate([jnp.where(first_head, qf, 0.0), jnp.where(first_head, 0.0, qf)],
                                 axis=0).astype(BF16)
            return lax.dot_general(q2, keys(kd_ref, cur, prev), (((1,), (1,)), ((), ())),
                                   preferred_element_type=F32) + bias

        def softmax(s):
            m2 = jnp.max(s, axis=-1, keepdims=True)
            p = jnp.exp2(s - m2)
            return m2, jnp.sum(p, axis=-1, keepdims=True), p.astype(BF16)

        def values(item, state):
            cur, prev, _ = item
            m2, l2, p = state
            return m2, l2, jnp.dot(p, keys(vd_ref, cur, prev), preferred_element_type=F32)

        def commit(item, state):
            cur = item[0]
            new = tuple(jnp.where(first_head, x[:blk], x[blk:]) for x in state)
            if merge:
                m_old, l_old, o_old = (gather(ref, cur) for ref in stats)
                m_new, l_new, o_new = new
                m_tot = jnp.maximum(m_old, m_new)
                a_old = jnp.exp2(m_old - m_tot)
                a_new = jnp.exp2(m_new - m_tot)
                new = (m_tot, l_old * a_old + l_new * a_new, o_old * a_old + o_new * a_new)
            for ref, val in zip(stats, new):
                scatter(ref, cur, val)

        stages = (lambda it, st: scores(it), lambda it, st: softmax(st), values, commit)
        state = [None] * len(items)
        for t in range(len(items) + len(stages) - 1):
            for k in reversed(range(len(stages))):
                i = t - k
                if 0 <= i < len(items):
                    if k == 0 and before and i in before:
                        before[i]()
                    state[i] = stages[k](items[i], state[i])
                    if k == len(stages) - 1 and after and i in after:
                        after[i]()

    def rows4(c, n):
        return [pl.ds(c * quarter + n * blk, blk)]

    deinterleave(0)
    process([(rows4(c, n), None, own_ref[0]) if n == 0
             else (rows4(c, n), rows4(c, n - 1), full_ref[0])
             for n in range(per_res) for c in range(grp)], merge=False,
            before={grp * (n - 1): (lambda n=n: deinterleave(n)) for n in range(1, per_res)})

    def rows1(n):
        return [pl.ds(c * quarter + n * slab, slab) for c in range(grp)]

    process([(rows1(0), None, own_ref[1])]
            + [(rows1(n), rows1(n - 1), full_ref[1]) for n in range(1, blocks)], merge=True)

    sub_blocks = seq // (DILATIONS[-1] * blk)

    def rows16(c, e, n):
        return [pl.ds(c * quarter + e + n * grp * blk, blk, stride=grp)]

    items = []
    for c in range(grp):
        for e in range(grp):
            items.append((rows16(c, e, 0), None, own_ref[0]))
            for n in range(1, sub_blocks):
                items.append((rows16(c, e, n), rows16(c, e, n - 1), full_ref[0]))
    per_quarter = grp * sub_blocks
    process(items, merge=True,
            after={(c + 1) * per_quarter - 1: (lambda c=c: finish(c)) for c in range(grp)})


def _attn_call(aq, ak, av, tail_weights):
    bsz, seq, w = aq.shape
    assert LANES == 2 * ATT_HEAD_DIM, "one grid step handles the two heads of a lane slab"
    pairs = w // LANES
    assert bsz * pairs >= WCAST_PARTS
    full, own = (jnp.asarray(x) for x in _attn_biases())
    blk = pl.BlockSpec((1, seq, LANES), lambda b, hp: (b, 0, hp))
    part = lambda b, hp: (jnp.minimum(b * pairs + hp, WCAST_PARTS - 1), 0)
    w_specs = [pl.BlockSpec((wt.shape[0] // WCAST_PARTS, wt.shape[1]), part) for wt in tail_weights]
    return pl.pallas_call(
        _attn_kernel,
        grid=(bsz, pairs),
        in_specs=[blk, blk, blk, _const_spec(full.shape), _const_spec(own.shape)] + w_specs,
        out_specs=[blk] + w_specs,
        out_shape=[jax.ShapeDtypeStruct((bsz, seq, w), F32)]
        + [jax.ShapeDtypeStruct(wt.shape, BF16) for wt in tail_weights],
        scratch_shapes=[pltpu.VMEM((seq, LANES), F32) for _ in range(6)],
        compiler_params=pltpu.CompilerParams(
            dimension_semantics=("arbitrary", "arbitrary"), vmem_limit_bytes=VMEM_LIMIT_BYTES),
        name="dilated_attn",
    )(aq, ak, av, full, own, *tail_weights)


def _tail_kernel(x_ref, hg_ref, att_ref, g1_ref, sh2_ref, sc2_ref, g2_ref,
                 an_ref, n2_ref, fin_ref, wo_ref, wgu_ref, wd_ref, o_ref):
    d_ff = wd_ref.shape[0]
    att = att_ref[...]
    ms = jnp.mean(att * att, axis=-1, keepdims=True)
    att_n = (att * lax.rsqrt(ms + RMS_EPS) * an_ref[...]).astype(BF16)
    mix = jnp.dot(hg_ref[...], wo_ref[:HG_WIDTH, :], preferred_element_type=F32)
    mix = mix + jnp.dot(att_n, wo_ref[HG_WIDTH:, :], preferred_element_type=F32)
    x1 = x_ref[...] + g1_ref[0] * mix

    ms = jnp.mean(x1 * x1, axis=-1, keepdims=True)
    gain = n2_ref[...] * (1.0 + sc2_ref[0])
    hb = (x1 * lax.rsqrt(ms + RMS_EPS) * gain + sh2_ref[0]).astype(BF16)
    ffn = jnp.zeros_like(x1)
    for j in range(d_ff // FF_CHUNK):
        lo = j * FF_CHUNK
        a = jnp.dot(hb, wgu_ref[:, lo:lo + FF_CHUNK], preferred_element_type=F32)
        u = jnp.dot(hb, wgu_ref[:, d_ff + lo:d_ff + lo + FF_CHUNK], preferred_element_type=F32)
        act = (_silu(a) * u).astype(BF16)
        ffn = ffn + jnp.dot(act, wd_ref[lo:lo + FF_CHUNK, :], preferred_element_type=F32)
    x2 = x1 + g2_ref[0] * ffn
    ms = jnp.mean(x2 * x2, axis=-1, keepdims=True)
    o_ref[...] = x2 * lax.rsqrt(ms + RMS_EPS) * fin_ref[...]


def _tail_call(x2, hg, att, mod3, att_g, norm2_g, final_g, wo_bf, wgu_bf, wd_bf, seq):
    t, d = x2.shape
    tm = ROW_TILE
    steps_per_batch = seq // tm
    row = lambda i: (i, 0)
    mod_col = lambda col: pl.BlockSpec((1, 1, d), lambda i: (i // steps_per_batch, 0, col))
    return pl.pallas_call(
        _tail_kernel,
        grid=(t // tm,),
        in_specs=[
            pl.BlockSpec((tm, d), row),
            pl.BlockSpec((tm, HG_WIDTH), row),
            pl.BlockSpec((tm, ATT_WIDTH), row),
            mod_col(2), mod_col(3), mod_col(4), mod_col(5),
            _const_spec((1, ATT_WIDTH)), _const_spec((1, d)), _const_spec((1, d)),
            _const_spec(wo_bf.shape), _const_spec(wgu_bf.shape), _const_spec(wd_bf.shape),
        ],
        out_specs=pl.BlockSpec((tm, d), row),
        out_shape=jax.ShapeDtypeStruct((t, d), F32),
        compiler_params=pltpu.CompilerParams(
            dimension_semantics=("arbitrary",), vmem_limit_bytes=VMEM_LIMIT_BYTES),
        name="outproj_ffn",
    )(x2, hg, att, mod3, mod3, mod3, mod3,
      att_g.reshape(1, ATT_WIDTH), norm2_g.reshape(1, d), final_g.reshape(1, d),
      wo_bf, wgu_bf, wd_bf)


def kernel(x, c, w_ada, b_ada, norm1_g, w_in, hg_lb_logits, hg_onorm_g, att_onorm_g,
           w_out, norm2_g, w_gate_up, w_down, final_g):
    bsz, seq, d = x.shape
    assert w_in.shape[0] == 1 and hg_lb_logits.shape[0] == 2, "single-layer block expected"
    assert seq % (DILATIONS[-1] * ATT_BLOCK) == 0 and seq % ROW_TILE == 0
    t = bsz * seq
    x2 = x.reshape(t, d)

    mod = _mod_call(c, w_ada[0], b_ada[0])
    mod3 = mod.reshape(bsz, 1, 6 * d)

    q, k, lf_hi, lf_lo, v, g, aq, ak, av, half_decay = _inproj_call(
        x2, mod3, norm1_g[0], w_in[0], hg_lb_logits, hg_onorm_g[0], seq)
    safe = (jnp.min(half_decay[:, :, 0].reshape(-1, HG_GROUP * HG_CHUNK // HG_HALF), axis=1)
            >= HG_SAFE_LOG2_DECAY).astype(jnp.int32)

    as_seq = lambda a: a.reshape(bsz, seq, a.shape[-1])
    hg = _hgrn_call(as_seq(q), as_seq(k), as_seq(lf_hi), as_seq(lf_lo), as_seq(v), as_seq(g), safe)
    att, wo_bf, wgu_bf, wd_bf = _attn_call(
        as_seq(aq), as_seq(ak), as_seq(av), (w_out[0], w_gate_up[0], w_down[0]))

    out = _tail_call(
        x2, hg.reshape(t, HG_WIDTH), att.reshape(t, ATT_WIDTH), mod3,
        att_onorm_g[0], norm2_g[0], final_g, wo_bf, wgu_bf, wd_bf, seq)
    return out.reshape(bsz, seq, d)
```

```python
import numpy as np
import jax
import jax.numpy as jnp
from jax import lax
from jax.experimental import pallas as pl
from jax.experimental.pallas import tpu as pltpu

F32 = jnp.float32
BF16 = jnp.bfloat16

LANES = 128
RMS_EPS = 1e-6
LOG2_E = 1.4426950408889634
PROJ_WIDTH = 512

HG_HEADS = 4
HG_DIM = 128
HG_WIDTH = HG_HEADS * HG_DIM
HG_CHUNK = 128
HG_LEVELS = 7
HG_HALF = HG_CHUNK // 2
HG_SAFE_LOG2_DECAY = -85.0
HG_GROUP = 16

ATT_HEAD_DIM = 64
ATT_WIDTH = 512
ATT_BLOCK = 128
DILATIONS = (1, 4, 16)
ATT_SPAN = 128
NEG_BIG = -1e30
ATT_GROUP = 4
WCAST_PARTS = 16

ROW_TILE = 512
FF_CHUNK = 256
WCAST_ROWS = 128
VMEM_LIMIT_BYTES = 56 * 1024 * 1024


def _silu(x):
    return x * jax.nn.sigmoid(x)


def _const_spec(shape):
    nd = len(shape)
    return pl.BlockSpec(shape, lambda *_: (0,) * nd, pipeline_mode=pl.Buffered(1))


def _mod_kernel(c_ref, w_ref, b_ref, o_ref):
    ca = _silu(c_ref[...])
    o_ref[...] = jnp.dot(ca, w_ref[...], preferred_element_type=F32) + b_ref[...]


def _mod_call(c, w_ada, b_ada):
    bsz, d = c.shape
    n = w_ada.shape[1]
    tn = d
    return pl.pallas_call(
        _mod_kernel,
        grid=(n // tn,),
        in_specs=[
            pl.BlockSpec((bsz, d), lambda j: (0, 0)),
            pl.BlockSpec((d, tn), lambda j: (0, j)),
            pl.BlockSpec((1, tn), lambda j: (0, j)),
        ],
        out_specs=pl.BlockSpec((bsz, tn), lambda j: (0, j)),
        out_shape=jax.ShapeDtypeStruct((bsz, n), F32),
        name="adaln_mod",
    )(c, w_ada, b_ada.reshape(1, n))


def _inproj_kernel(x_ref, sh_ref, sc_ref, g_ref, wf_ref, lbl_ref, on_ref,
                   q_o, k_o, lfh_o, lfl_o, v_o, g_o, aq_o, ak_o, av_o, dmin_o, w_ref):
    @pl.when(pl.program_id(0) == 0)
    def _():
        for r in range(0, wf_ref.shape[0], WCAST_ROWS):
            w_ref[r:r + WCAST_ROWS, :] = wf_ref[r:r + WCAST_ROWS, :].astype(BF16)

    x = x_ref[...]
    ms = jnp.mean(x * x, axis=-1, keepdims=True)
    gain = g_ref[...] * (1.0 + sc_ref[0])
    hb = (x * lax.rsqrt(ms + RMS_EPS) * gain + sh_ref[0]).astype(BF16)
    tm = hb.shape[0]

    def proj(j):
        return jnp.dot(hb, w_ref[:, j * PROJ_WIDTH:(j + 1) * PROJ_WIDTH],
                       preferred_element_type=F32)

    lbl = lbl_ref[...]
    e = jnp.exp(lbl - jnp.max(lbl, axis=0, keepdims=True))
    lb = e[0:1] / (e[0:1] + e[1:2])

    sg = jax.nn.sigmoid(proj(1))
    lf = jnp.log2(lb + (1.0 - lb) * sg)
    lf_hi = lf.astype(BF16)
    lfh_o[...] = lf_hi
    lfl_o[...] = (lf - lf_hi.astype(F32)).astype(BF16)
    k_o[...] = ((1.0 - lb) * (1.0 - sg)).astype(BF16)
    ak_o[...] = proj(5)
    q_o[...] = _silu(proj(0)).astype(BF16)
    av_o[...] = proj(6)
    g_o[...] = (_silu(proj(3)) * on_ref[...]).astype(BF16)
    grp_id = lax.broadcasted_iota(jnp.int32, (tm // HG_HALF, tm), 0)
    row_id = lax.broadcasted_iota(jnp.int32, (tm // HG_HALF, tm), 1)
    member = (row_id // HG_HALF == grp_id).astype(BF16)
    totals = jnp.dot(member, lf_hi, preferred_element_type=F32)
    dmin_o[0] = jnp.broadcast_to(jnp.min(totals, axis=-1, keepdims=True), dmin_o.shape[1:])
    v_o[...] = proj(2).astype(BF16)
    aq_o[...] = proj(4) * (ATT_HEAD_DIM ** -0.5 * LOG2_E)


def _inproj_call(x2, mod3, norm_g, w_in, lb_logits, hg_gain, seq):
    t, d = x2.shape
    tm = ROW_TILE
    steps_per_batch = seq // tm
    row = lambda i: (i, 0)
    out_dtypes = (BF16, BF16, BF16, BF16, BF16, BF16, F32, F32, F32)
    return pl.pallas_call(
        _inproj_kernel,
        grid=(t // tm,),
        in_specs=[
            pl.BlockSpec((tm, d), row),
            pl.BlockSpec((1, 1, d), lambda i: (i // steps_per_batch, 0, 0)),
            pl.BlockSpec((1, 1, d), lambda i: (i // steps_per_batch, 0, 1)),
            _const_spec((1, d)),
            _const_spec(w_in.shape),
            _const_spec(lb_logits.shape),
            _const_spec((1, HG_WIDTH)),
        ],
        out_specs=[pl.BlockSpec((tm, PROJ_WIDTH), row) for _ in out_dtypes]
        + [pl.BlockSpec((1, tm // HG_HALF, LANES), lambda i: (i, 0, 0))],
        out_shape=[jax.ShapeDtypeStruct((t, PROJ_WIDTH), dt) for dt in out_dtypes]
        + [jax.ShapeDtypeStruct((t // tm, tm // HG_HALF, LANES), F32)],
        scratch_shapes=[pltpu.VMEM(w_in.shape, BF16)],
        compiler_params=pltpu.CompilerParams(
            dimension_semantics=("arbitrary",), vmem_limit_bytes=VMEM_LIMIT_BYTES),
        name="in_proj",
    )(x2, mod3, mod3, norm_g.reshape(1, d), w_in, lb_logits,
      jnp.tile(hg_gain.reshape(1, HG_DIM), (1, HG_HEADS)))


def _decay_selectors():
    c = HG_CHUNK
    t = np.arange(c)[:, None]
    u = np.arange(c)[None, :]
    blocks = []
    for l in range(HG_LEVELS):
        lo = (t >> l) << l
        hi = lo + (1 << l) - 1
        second = ((t >> l) & 1) == 1
        blocks.append(np.where(second, (u >= lo) & (u <= t), (u > t) & (u <= hi)))
    blocks.append(u <= t)
    blocks.append(u > t)
    return np.concatenate(blocks, axis=0).astype(np.float32)


def _hgrn_kernel(safe_ref, q_ref, k_ref, lfh_ref, lfl_ref, v_ref, g_ref, p_ref, o_ref, st_ref):
    c = HG_CHUNK
    half = HG_HALF
    step = pl.program_id(0) * pl.num_programs(1) + pl.program_id(1)

    @pl.when(pl.program_id(1) == 0)
    def _():
        st_ref[...] = jnp.zeros_like(st_ref)

    def head(a, h):
        return a[:, h * HG_DIM:(h + 1) * HG_DIM]

    def gram(a, b):
        return lax.dot_general(a, b, (((1,), (1,)), ((), ())), preferred_element_type=F32)

    def log_decay(rows):
        return jnp.concatenate([lfh_ref[0, rows, :], lfl_ref[0, rows, :]], axis=0)

    def exponents(lf2, block):
        return jnp.dot(p_ref[block * c:(block + 1) * c, :], lf2, preferred_element_type=F32)

    def scaled(x_bf, log2_scale):
        return x_bf * jnp.exp2(log2_scale).astype(BF16)

    def normed(rows, o, h):
        ms = jnp.mean(o * o, axis=-1, keepdims=True)
        return (o * lax.rsqrt(ms + RMS_EPS)).astype(BF16) * head(g_ref[0, rows, :], h)

    safe = safe_ref[step] > 0

    @pl.when(safe)
    def _():
        causal = (lax.broadcasted_iota(jnp.int32, (c, c), 1)
                  <= lax.broadcasted_iota(jnp.int32, (c, c), 0))
        heads = range(HG_HEADS)

        def decay(j):
            rows = slice(j * c, (j + 1) * c)
            return rows, exponents(log_decay(rows), HG_LEVELS)

        def scale(st):
            rows, b = st
            b_mid = b[half - 1:half, :]
            q = q_ref[0, rows, :]
            k = k_ref[0, rows, :]
            return dict(rows=rows, b_mid=b_mid, b_last=b[c - 1:c, :], qe=scaled(q, b),
                        q_mid=scaled(q, b - b_mid), k_mid=scaled(k, b_mid - b))

        def score(st):
            st["s"] = [jnp.where(causal, gram(head(st["q_mid"], h), head(st["k_mid"], h)),
                                 0.0).astype(BF16) for h in heads]
            return st

        def apply(st):
            v = v_ref[0, st["rows"], :]
            st["intra"] = [jnp.dot(st["s"][h], head(v, h), preferred_element_type=F32)
                           for h in heads]
            tail_decay = jnp.exp2(st["b_last"] - st["b_mid"])
            st["kv"] = [lax.dot_general(head(v, h), head(st["k_mid"], h),
                                        (((0,), (0,)), ((), ())), preferred_element_type=F32)
                        * head(tail_decay, h) for h in heads]
            return st

        states = [st_ref[h] for h in heads]

        def recur(st):
            chunk_decay = jnp.exp2(st["b_last"])
            outs = []
            for h in heads:
                o = st["intra"][h] + gram(head(st["qe"], h), states[h].astype(BF16))
                states[h] = states[h] * head(chunk_decay, h) + st["kv"][h]
                outs.append(normed(st["rows"], o, h))
            o_ref[0, st["rows"], :] = jnp.concatenate(outs, axis=1).astype(o_ref.dtype)
            return None

        stages = (lambda st: apply(score(scale(st))), recur)
        live = [None] * HG_GROUP
        for t in range(HG_GROUP + len(stages)):
            for kk in reversed(range(len(stages) + 1)):
                j = t - kk
                if 0 <= j < HG_GROUP:
                    live[j] = decay(j) if kk == 0 else stages[kk - 1](live[j])
        for h in heads:
            st_ref[h] = states[h]


    @pl.when(jnp.logical_not(safe))
    def _():
        row = lax.broadcasted_iota(jnp.int32, (c, HG_WIDTH), 0)
        ti = lax.broadcasted_iota(jnp.int32, (c, c), 0)
        si = lax.broadcasted_iota(jnp.int32, (c, c), 1)
        txs = ti ^ si

        def one_chunk(j, carry):
            rows = pl.ds(pl.multiple_of(j * c, c), c)
            q = q_ref[0, rows, :]
            k = k_ref[0, rows, :]
            v = v_ref[0, rows, :]
            lf = log_decay(rows)
            scores = [jnp.where(ti == si, gram(head(q, h), head(k, h)), 0.0)
                      for h in range(HG_HEADS)]
            for l in range(HG_LEVELS):
                second = ((row >> l) & 1) == 1
                xl = scaled(jnp.where(second, q, k), exponents(lf, l))
                mask = (ti > si) & ((txs >> l) == 1)
                for h in range(HG_HEADS):
                    xh = head(xl, h)
                    scores[h] = jnp.where(mask, gram(xh, xh), scores[h])
            b = exponents(lf, HG_LEVELS)
            qe = scaled(q, b)
            k_end = scaled(k, exponents(lf, HG_LEVELS + 1))
            chunk_decay = jnp.exp2(b[c - 1:c, :])
            outs = []
            for h in range(HG_HEADS):
                st = st_ref[h]
                o = jnp.dot(scores[h].astype(BF16), head(v, h), preferred_element_type=F32)
                o = o + gram(head(qe, h), st.astype(BF16))
                st_ref[h] = st * head(chunk_decay, h) + lax.dot_general(
                    head(v, h), head(k_end, h), (((0,), (0,)), ((), ())),
                    preferred_element_type=F32)
                outs.append(normed(rows, o, h))
            o_ref[0, rows, :] = jnp.concatenate(outs, axis=1).astype(o_ref.dtype)
            return carry

        lax.fori_loop(0, HG_GROUP, one_chunk, 0)


def _hgrn_call(q, k, lf_hi, lf_lo, v, g, safe):
    bsz, seq, w = q.shape
    rows = HG_GROUP * HG_CHUNK
    sel = _decay_selectors()
    sel = jnp.asarray(np.concatenate([sel, sel], axis=1), dtype=BF16)
    blk = pl.BlockSpec((1, rows, w), lambda b, i, safe_ref: (b, i, 0))
    return pl.pallas_call(
        _hgrn_kernel,
        grid_spec=pltpu.PrefetchScalarGridSpec(
            num_scalar_prefetch=1,
            grid=(bsz, seq // rows),
            in_specs=[blk, blk, blk, blk, blk, blk, _const_spec(sel.shape)],
            out_specs=blk,
            scratch_shapes=[pltpu.VMEM((HG_HEADS, HG_DIM, HG_DIM), F32)],
        ),
        out_shape=jax.ShapeDtypeStruct((bsz, seq, w), BF16),
        compiler_params=pltpu.CompilerParams(
            dimension_semantics=("arbitrary", "arbitrary"), vmem_limit_bytes=VMEM_LIMIT_BYTES),
        name="hgrn2",
    )(safe, q, k, lf_hi, lf_lo, v, g, sel)


def _attn_biases():
    blk, grp = ATT_BLOCK, ATT_GROUP
    slab = blk // grp
    a = np.arange(blk)
    unslab = grp * (a % slab) + a // slab
    full, own = [], []
    for pos in (a, unslab):
        kpos = np.concatenate([pos, pos + blk])
        dist = blk + pos[:, None] - kpos[None, :]
        keep = (dist >= 0) & (dist <= ATT_SPAN)
        full.append(np.where(np.concatenate([keep, keep], axis=0), 0.0, NEG_BIG))
        keep_own = keep[:, blk:]
        own.append(np.where(np.concatenate([keep_own, keep_own], axis=0), 0.0, NEG_BIG))
    return np.stack(full).astype(np.float32), np.stack(own).astype(np.float32)


def _attn_kernel(q_ref, k_ref, v_ref, full_ref, own_ref, wa_ref, wb_ref, wc_ref,
                 o_ref, wa_out, wb_out, wc_out,
                 qd_ref, kd_ref, vd_ref, acc_ref, m_ref, l_ref):
    seq = q_ref.shape[1]
    blk = ATT_BLOCK
    grp = ATT_GROUP
    quarter = seq // grp
    slab = blk // grp
    blocks = seq // blk
    per_res = quarter // blk

    @pl.when(pl.program_id(0) * pl.num_programs(1) + pl.program_id(1) < WCAST_PARTS)
    def _():
        for src, dst in ((wa_ref, wa_out), (wb_ref, wb_out), (wc_ref, wc_out)):
            dst[...] = src[...].astype(dst.dtype)

    def deinterleave(i):
        for src, dst in ((q_ref, qd_ref), (k_ref, kd_ref), (v_ref, vd_ref)):
            for c in range(grp):
                dst[pl.ds(c * quarter + i * blk, blk), :] = (
                    src[0, pl.ds(c + grp * i * blk, blk, stride=grp), :])

    def finish(c):
        for i in range(per_res):
            rows = pl.ds(c * quarter + i * blk, blk)
            o_ref[0, pl.ds(c + grp * i * blk, blk, stride=grp), :] = acc_ref[rows, :] / l_ref[rows, :]

    first_head = lax.broadcasted_iota(jnp.int32, (blk, LANES), 1) < ATT_HEAD_DIM
    stats = (m_ref, l_ref, acc_ref)

    def gather(ref, slices):
        parts = [ref[sl, :] for sl in slices]
        return parts[0] if len(parts) == 1 else jnp.concatenate(parts, axis=0)

    def scatter(ref, slices, val):
        n = val.shape[0] // len(slices)
        for i, sl in enumerate(slices):
            ref[sl, :] = val[i * n:(i + 1) * n]

    def process(items, merge, before=None, after=None):
        def keys(ref, cur, prev):
            own = gather(ref, cur)
            if prev is not None:
                own = jnp.concatenate([gather(ref, prev), own], axis=0)
            return own.astype(BF16)

        def scores(item):
            cur, prev, bias = item
            qf = gather(qd_ref, cur)
            q2 = jnp.concatenate([jnp.where(first_head, qf, 0.0), jnp.where(first_head, 0.0, qf)],
                                 axis=0).astype(BF16)
            return lax.dot_general(q2, keys(kd_ref, cur, prev), (((1,), (1,)), ((), ())),
                                   preferred_element_type=F32) + bias

        def softmax(s):
            m2 = jnp.max(s, axis=-1, keepdims=True)
            p = jnp.exp2(s - m2)
            return m2, jnp.sum(p, axis=-1, keepdims=True), p.astype(BF16)

        def values(item, state):
            cur, prev, _ = item
            m2, l2, p = state
            return m2, l2, jnp.dot(p, keys(vd_ref, cur, prev), preferred_element_type=F32)

        def commit(item, state):
            cur = item[0]
            new = tuple(jnp.where(first_head, x[:blk], x[blk:]) for x in state)
            if merge:
                m_old, l_old, o_old = (gather(ref, cur) for ref in stats)
                m_new, l_new, o_new = new
                m_tot = jnp.maximum(m_old, m_new)
                a_old = jnp.exp2(m_old - m_tot)
                a_new = jnp.exp2(m_new - m_tot)
                new = (m_tot, l_old * a_old + l_new * a_new, o_old * a_old + o_new * a_new)
            for ref, val in zip(stats, new):
                scatter(ref, cur, val)

        stages = (lambda it, st: scores(it), lambda it, st: softmax(st), values, commit)
        state = [None] * len(items)
        for t in range(len(items) + len(stages) - 1):
            for k in reversed(range(len(stages))):
                i = t - k
                if 0 <= i < len(items):
                    if k == 0 and before and i in before:
                        before[i]()
                    state[i] = stages[k](items[i], state[i])
                    if k == len(stages) - 1 and after and i in after:
                        after[i]()

    def rows4(c, n):
        return [pl.ds(c * quarter + n * blk, blk)]

    deinterleave(0)
    process([(rows4(c, n), None, own_ref[0]) if n == 0
             else (rows4(c, n), rows4(c, n - 1), full_ref[0])
             for n in range(per_res) for c in range(grp)], merge=False,
            before={grp * (n - 1): (lambda n=n: deinterleave(n)) for n in range(1, per_res)})

    def rows1(n):
        return [pl.ds(c * quarter + n * slab, slab) for c in range(grp)]

    process([(rows1(0), None, own_ref[1])]
            + [(rows1(n), rows1(n - 1), full_ref[1]) for n in range(1, blocks)], merge=True)

    sub_blocks = seq // (DILATIONS[-1] * blk)

    def rows16(c, e, n):
        return [pl.ds(c * quarter + e + n * grp * blk, blk, stride=grp)]

    items = []
    for c in range(grp):
        for e in range(grp):
            items.append((rows16(c, e, 0), None, own_ref[0]))
            for n in range(1, sub_blocks):
                items.append((rows16(c, e, n), rows16(c, e, n - 1), full_ref[0]))
    per_quarter = grp * sub_blocks
    process(items, merge=True,
            after={(c + 1) * per_quarter - 1: (lambda c=c: finish(c)) for c in range(grp)})


def _attn_call(aq, ak, av, tail_weights):
    bsz, seq, w = aq.shape
    assert LANES == 2 * ATT_HEAD_DIM, "one grid step handles the two heads of a lane slab"
    pairs = w // LANES
    assert bsz * pairs >= WCAST_PARTS
    full, own = (jnp.asarray(x) for x in _attn_biases())
    blk = pl.BlockSpec((1, seq, LANES), lambda b, hp: (b, 0, hp))
    part = lambda b, hp: (jnp.minimum(b * pairs + hp, WCAST_PARTS - 1), 0)
    w_specs = [pl.BlockSpec((wt.shape[0] // WCAST_PARTS, wt.shape[1]), part) for wt in tail_weights]
    return pl.pallas_call(
        _attn_kernel,
        grid=(bsz, pairs),
        in_specs=[blk, blk, blk, _const_spec(full.shape), _const_spec(own.shape)] + w_specs,
        out_specs=[blk] + w_specs,
        out_shape=[jax.ShapeDtypeStruct((bsz, seq, w), F32)]
        + [jax.ShapeDtypeStruct(wt.shape, BF16) for wt in tail_weights],
        scratch_shapes=[pltpu.VMEM((seq, LANES), F32) for _ in range(6)],
        compiler_params=pltpu.CompilerParams(
            dimension_semantics=("arbitrary", "arbitrary"), vmem_limit_bytes=VMEM_LIMIT_BYTES),
        name="dilated_attn",
    )(aq, ak, av, full, own, *tail_weights)


def _tail_kernel(x_ref, hg_ref, att_ref, g1_ref, sh2_ref, sc2_ref, g2_ref,
                 an_ref, n2_ref, fin_ref, wo_ref, wgu_ref, wd_ref, o_ref):
    d_ff = wd_ref.shape[0]
    att = att_ref[...]
    ms = jnp.mean(att * att, axis=-1, keepdims=True)
    att_n = (att * lax.rsqrt(ms + RMS_EPS) * an_ref[...]).astype(BF16)
    mix = jnp.dot(hg_ref[...], wo_ref[:HG_WIDTH, :], preferred_element_type=F32)
    mix = mix + jnp.dot(att_n, wo_ref[HG_WIDTH:, :], preferred_element_type=F32)
    x1 = x_ref[...] + g1_ref[0] * mix

    ms = jnp.mean(x1 * x1, axis=-1, keepdims=True)
    gain = n2_ref[...] * (1.0 + sc2_ref[0])
    hb = (x1 * lax.rsqrt(ms + RMS_EPS) * gain + sh2_ref[0]).astype(BF16)
    ffn = jnp.zeros_like(x1)
    for j in range(d_ff // FF_CHUNK):
        lo = j * FF_CHUNK
        a = jnp.dot(hb, wgu_ref[:, lo:lo + FF_CHUNK], preferred_element_type=F32)
        u = jnp.dot(hb, wgu_ref[:, d_ff + lo:d_ff + lo + FF_CHUNK], preferred_element_type=F32)
        act = (_silu(a) * u).astype(BF16)
        ffn = ffn + jnp.dot(act, wd_ref[lo:lo + FF_CHUNK, :], preferred_element_type=F32)
    x2 = x1 + g2_ref[0] * ffn
    ms = jnp.mean(x2 * x2, axis=-1, keepdims=True)
    o_ref[...] = x2 * lax.rsqrt(ms + RMS_EPS) * fin_ref[...]


def _tail_call(x2, hg, att, mod3, att_g, norm2_g, final_g, wo_bf, wgu_bf, wd_bf, seq):
    t, d = x2.shape
    tm = ROW_TILE
    steps_per_batch = seq // tm
    row = lambda i: (i, 0)
    mod_col = lambda col: pl.BlockSpec((1, 1, d), lambda i: (i // steps_per_batch, 0, col))
    return pl.pallas_call(
        _tail_kernel,
        grid=(t // tm,),
        in_specs=[
            pl.BlockSpec((tm, d), row),
            pl.BlockSpec((tm, HG_WIDTH), row),
            pl.BlockSpec((tm, ATT_WIDTH), row),
            mod_col(2), mod_col(3), mod_col(4), mod_col(5),
            _const_spec((1, ATT_WIDTH)), _const_spec((1, d)), _const_spec((1, d)),
            _const_spec(wo_bf.shape), _const_spec(wgu_bf.shape), _const_spec(wd_bf.shape),
        ],
        out_specs=pl.BlockSpec((tm, d), row),
        out_shape=jax.ShapeDtypeStruct((t, d), F32),
        compiler_params=pltpu.CompilerParams(
            dimension_semantics=("arbitrary",), vmem_limit_bytes=VMEM_LIMIT_BYTES),
        name="outproj_ffn",
    )(x2, hg, att, mod3, mod3, mod3, mod3,
      att_g.reshape(1, ATT_WIDTH), norm2_g.reshape(1, d), final_g.reshape(1, d),
      wo_bf, wgu_bf, wd_bf)


def kernel(x, c, w_ada, b_ada, norm1_g, w_in, hg_lb_logits, hg_onorm_g, att_onorm_g,
           w_out, norm2_g, w_gate_up, w_down, final_g):
    bsz, seq, d = x.shape
    assert w_in.shape[0] == 1 and hg_lb_logits.shape[0] == 2, "single-layer block expected"
    assert seq % (DILATIONS[-1] * ATT_BLOCK) == 0 and seq % ROW_TILE == 0
    t = bsz * seq
    x2 = x.reshape(t, d)

    mod = _mod_call(c, w_ada[0], b_ada[0])
    mod3 = mod.reshape(bsz, 1, 6 * d)

    q, k, lf_hi, lf_lo, v, g, aq, ak, av, half_decay = _inproj_call(
        x2, mod3, norm1_g[0], w_in[0], hg_lb_logits, hg_onorm_g[0], seq)
    safe = (jnp.min(half_decay[:, :, 0].reshape(-1, HG_GROUP * HG_CHUNK // HG_HALF), axis=1)
            >= HG_SAFE_LOG2_DECAY).astype(jnp.int32)

    as_seq = lambda a: a.reshape(bsz, seq, a.shape[-1])
    hg = _hgrn_call(as_seq(q), as_seq(k), as_seq(lf_hi), as_seq(lf_lo), as_seq(v), as_seq(g), safe)
    att, wo_bf, wgu_bf, wd_bf = _attn_call(
        as_seq(aq), as_seq(ak), as_seq(av), (w_out[0], w_gate_up[0], w_down[0]))

    out = _tail_call(
        x2, hg.reshape(t, HG_WIDTH), att.reshape(t, ATT_WIDTH), mod3,
        att_onorm_g[0], norm2_g[0], final_g, wo_bf, wgu_bf, wd_bf, seq)
    return out.reshape(bsz, seq, d)
```

```python
import numpy as np
import jax
import jax.numpy as jnp
from jax import lax
from jax.experimental import pallas as pl
from jax.experimental.pallas import tpu as pltpu

F32 = jnp.float32
BF16 = jnp.bfloat16

LANES = 128
RMS_EPS = 1e-6
LOG2_E = 1.4426950408889634
PROJ_WIDTH = 512

HG_HEADS = 4
HG_DIM = 128
HG_WIDTH = HG_HEADS * HG_DIM
HG_CHUNK = 128
HG_LEVELS = 7
HG_HALF = HG_CHUNK // 2
HG_SAFE_LOG2_DECAY = -85.0
HG_GROUP = 8

ATT_HEAD_DIM = 64
ATT_WIDTH = 512
ATT_BLOCK = 128
DILATIONS = (1, 4, 16)
ATT_SPAN = 128
NEG_BIG = -1e30
ATT_GROUP = 4
WCAST_PARTS = 16

ROW_TILE = 512
FF_CHUNK = 256
WCAST_ROWS = 128
VMEM_LIMIT_BYTES = 56 * 1024 * 1024


def _silu(x):
    return x * jax.nn.sigmoid(x)


def _const_spec(shape):
    nd = len(shape)
    return pl.BlockSpec(shape, lambda *_: (0,) * nd, pipeline_mode=pl.Buffered(1))


def _mod_kernel(c_ref, w_ref, b_ref, o_ref):
    ca = _silu(c_ref[...])
    o_ref[...] = jnp.dot(ca, w_ref[...], preferred_element_type=F32) + b_ref[...]


def _mod_call(c, w_ada, b_ada):
    bsz, d = c.shape
    n = w_ada.shape[1]
    tn = d
    return pl.pallas_call(
        _mod_kernel,
        grid=(n // tn,),
        in_specs=[
            pl.BlockSpec((bsz, d), lambda j: (0, 0)),
            pl.BlockSpec((d, tn), lambda j: (0, j)),
            pl.BlockSpec((1, tn), lambda j: (0, j)),
        ],
        out_specs=pl.BlockSpec((bsz, tn), lambda j: (0, j)),
        out_shape=jax.ShapeDtypeStruct((bsz, n), F32),
        name="adaln_mod",
    )(c, w_ada, b_ada.reshape(1, n))


def _inproj_kernel(x_ref, sh_ref, sc_ref, g_ref, wf_ref, lbl_ref, on_ref,
                   q_o, k_o, lfh_o, lfl_o, v_o, g_o, aq_o, ak_o, av_o, dmin_o, w_ref):
    @pl.when(pl.program_id(0) == 0)
    def _():
        for r in range(0, wf_ref.shape[0], WCAST_ROWS):
            w_ref[r:r + WCAST_ROWS, :] = wf_ref[r:r + WCAST_ROWS, :].astype(BF16)

    x = x_ref[...]
    ms = jnp.mean(x * x, axis=-1, keepdims=True)
    gain = g_ref[...] * (1.0 + sc_ref[0])
    hb = (x * lax.rsqrt(ms + RMS_EPS) * gain + sh_ref[0]).astype(BF16)
    tm = hb.shape[0]

    def proj(j):
        return jnp.dot(hb, w_ref[:, j * PROJ_WIDTH:(j + 1) * PROJ_WIDTH],
                       preferred_element_type=F32)

    lbl = lbl_ref[...]
    e = jnp.exp(lbl - jnp.max(lbl, axis=0, keepdims=True))
    lb = e[0:1] / (e[0:1] + e[1:2])

    sg = jax.nn.sigmoid(proj(1))
    lf = jnp.log2(lb + (1.0 - lb) * sg)
    lf_hi = lf.astype(BF16)
    lfh_o[...] = lf_hi
    lfl_o[...] = (lf - lf_hi.astype(F32)).astype(BF16)
    k_o[...] = ((1.0 - lb) * (1.0 - sg)).astype(BF16)
    ak_o[...] = proj(5)
    q_o[...] = _silu(proj(0)).astype(BF16)
    av_o[...] = proj(6)
    g_o[...] = (_silu(proj(3)) * on_ref[...]).astype(BF16)
    grp_id = lax.broadcasted_iota(jnp.int32, (tm // HG_HALF, tm), 0)
    row_id = lax.broadcasted_iota(jnp.int32, (tm // HG_HALF, tm), 1)
    member = (row_id // HG_HALF == grp_id).astype(BF16)
    totals = jnp.dot(member, lf_hi, preferred_element_type=F32)
    dmin_o[0] = jnp.broadcast_to(jnp.min(totals, axis=-1, keepdims=True), dmin_o.shape[1:])
    v_o[...] = proj(2).astype(BF16)
    aq_o[...] = proj(4) * (ATT_HEAD_DIM ** -0.5 * LOG2_E)


def _inproj_call(x2, mod3, norm_g, w_in, lb_logits, hg_gain, seq):
    t, d = x2.shape
    tm = ROW_TILE
    steps_per_batch = seq // tm
    row = lambda i: (i, 0)
    out_dtypes = (BF16, BF16, BF16, BF16, BF16, BF16, F32, F32, F32)
    return pl.pallas_call(
        _inproj_kernel,
        grid=(t // tm,),
        in_specs=[
            pl.BlockSpec((tm, d), row),
            pl.BlockSpec((1, 1, d), lambda i: (i // steps_per_batch, 0, 0)),
            pl.BlockSpec((1, 1, d), lambda i: (i // steps_per_batch, 0, 1)),
            _const_spec((1, d)),
            _const_spec(w_in.shape),
            _const_spec(lb_logits.shape),
            _const_spec((1, HG_WIDTH)),
        ],
        out_specs=[pl.BlockSpec((tm, PROJ_WIDTH), row) for _ in out_dtypes]
        + [pl.BlockSpec((1, tm // HG_HALF, LANES), lambda i: (i, 0, 0))],
        out_shape=[jax.ShapeDtypeStruct((t, PROJ_WIDTH), dt) for dt in out_dtypes]
        + [jax.ShapeDtypeStruct((t // tm, tm // HG_HALF, LANES), F32)],
        scratch_shapes=[pltpu.VMEM(w_in.shape, BF16)],
        compiler_params=pltpu.CompilerParams(
            dimension_semantics=("arbitrary",), vmem_limit_bytes=VMEM_LIMIT_BYTES),
        name="in_proj",
    )(x2, mod3, mod3, norm_g.reshape(1, d), w_in, lb_logits,
      jnp.tile(hg_gain.reshape(1, HG_DIM), (1, HG_HEADS)))


def _decay_selectors():
    c = HG_CHUNK
    t = np.arange(c)[:, None]
    u = np.arange(c)[None, :]
    blocks = []
    for l in range(HG_LEVELS):
        lo = (t >> l) << l
        hi = lo + (1 << l) - 1
        second = ((t >> l) & 1) == 1
        blocks.append(np.where(second, (u >= lo) & (u <= t), (u > t) & (u <= hi)))
    blocks.append(u <= t)
    blocks.append(u > t)
    return np.concatenate(blocks, axis=0).astype(np.float32)


def _hgrn_kernel(safe_ref, q_ref, k_ref, lfh_ref, lfl_ref, v_ref, g_ref, p_ref, o_ref, st_ref):
    c = HG_CHUNK
    half = HG_HALF
    step = pl.program_id(0) * pl.num_programs(1) + pl.program_id(1)

    @pl.when(pl.program_id(1) == 0)
    def _():
        st_ref[...] = jnp.zeros_like(st_ref)

    def head(a, h):
        return a[:, h * HG_DIM:(h + 1) * HG_DIM]

    def gram(a, b):
        return lax.dot_general(a, b, (((1,), (1,)), ((), ())), preferred_element_type=F32)

    def log_decay(rows):
        return jnp.concatenate([lfh_ref[0, rows, :], lfl_ref[0, rows, :]], axis=0)

    def exponents(lf2, block):
        return jnp.dot(p_ref[block * c:(block + 1) * c, :], lf2, preferred_element_type=F32)

    def scaled(x_bf, log2_scale):
        return x_bf * jnp.exp2(log2_scale).astype(BF16)

    def normed(rows, o, h):
        ms = jnp.mean(o * o, axis=-1, keepdims=True)
        return (o * lax.rsqrt(ms + RMS_EPS)).astype(BF16) * head(g_ref[0, rows, :], h)

    safe = safe_ref[step] > 0

    @pl.when(safe)
    def _():
        causal = (lax.broadcasted_iota(jnp.int32, (c, c), 1)
                  <= lax.broadcasted_iota(jnp.int32, (c, c), 0))
        heads = range(HG_HEADS)

        def decay(j):
            rows = slice(j * c, (j + 1) * c)
            return rows, exponents(log_decay(rows), HG_LEVELS)

        def scale(st):
            rows, b = st
            b_mid = b[half - 1:half, :]
            q = q_ref[0, rows, :]
            k = k_ref[0, rows, :]
            return dict(rows=rows, b_mid=b_mid, b_last=b[c - 1:c, :], qe=scaled(q, b),
                        q_mid=scaled(q, b - b_mid), k_mid=scaled(k, b_mid - b))

        def score(st):
            st["s"] = [jnp.where(causal, gram(head(st["q_mid"], h), head(st["k_mid"], h)),
                                 0.0).astype(BF16) for h in heads]
            return st

        def apply(st):
            v = v_ref[0, st["rows"], :]
            st["intra"] = [jnp.dot(st["s"][h], head(v, h), preferred_element_type=F32)
                           for h in heads]
            tail_decay = jnp.exp2(st["b_last"] - st["b_mid"])
            st["kv"] = [lax.dot_general(head(v, h), head(st["k_mid"], h),
                                        (((0,), (0,)), ((), ())), preferred_element_type=F32)
                        * head(tail_decay, h) for h in heads]
            return st

        states = [st_ref[h] for h in heads]

        def recur(st):
            chunk_decay = jnp.exp2(st["b_last"])
            outs = []
            for h in heads:
                o = st["intra"][h] + gram(head(st["qe"], h), states[h].astype(BF16))
                states[h] = states[h] * head(chunk_decay, h) + st["kv"][h]
                outs.append(normed(st["rows"], o, h))
            o_ref[0, st["rows"], :] = jnp.concatenate(outs, axis=1).astype(o_ref.dtype)
            return None

        stages = (scale, lambda st: apply(score(st)), recur)
        live = [None] * HG_GROUP
        for t in range(HG_GROUP + len(stages)):
            for kk in reversed(range(len(stages) + 1)):
                j = t - kk
                if 0 <= j < HG_GROUP:
                    live[j] = decay(j) if kk == 0 else stages[kk - 1](live[j])
        for h in heads:
            st_ref[h] = states[h]


    @pl.when(jnp.logical_not(safe))
    def _():
        row = lax.broadcasted_iota(jnp.int32, (c, HG_WIDTH), 0)
        ti = lax.broadcasted_iota(jnp.int32, (c, c), 0)
        si = lax.broadcasted_iota(jnp.int32, (c, c), 1)
        txs = ti ^ si

        def one_chunk(j, carry):
            rows = pl.ds(pl.multiple_of(j * c, c), c)
            q = q_ref[0, rows, :]
            k = k_ref[0, rows, :]
            v = v_ref[0, rows, :]
            lf = log_decay(rows)
            scores = [jnp.where(ti == si, gram(head(q, h), head(k, h)), 0.0)
                      for h in range(HG_HEADS)]
            for l in range(HG_LEVELS):
                second = ((row >> l) & 1) == 1
                xl = scaled(jnp.where(second, q, k), exponents(lf, l))
                mask = (ti > si) & ((txs >> l) == 1)
                for h in range(HG_HEADS):
                    xh = head(xl, h)
                    scores[h] = jnp.where(mask, gram(xh, xh), scores[h])
            b = exponents(lf, HG_LEVELS)
            qe = scaled(q, b)
            k_end = scaled(k, exponents(lf, HG_LEVELS + 1))
            chunk_decay = jnp.exp2(b[c - 1:c, :])
            outs = []
            for h in range(HG_HEADS):
                st = st_ref[h]
                o = jnp.dot(scores[h].astype(BF16), head(v, h), preferred_element_type=F32)
                o = o + gram(head(qe, h), st.astype(BF16))
                st_ref[h] = st * head(chunk_decay, h) + lax.dot_general(
                    head(v, h), head(k_end, h), (((0,), (0,)), ((), ())),
                    preferred_element_type=F32)
                outs.append(normed(rows, o, h))
            o_ref[0, rows, :] = jnp.concatenate(outs, axis=1).astype(o_ref.dtype)
            return carry

        lax.fori_loop(0, HG_GROUP, one_chunk, 0)


def _hgrn_call(q, k, lf_hi, lf_lo, v, g, safe):
    bsz, seq, w = q.shape
    rows = HG_GROUP * HG_CHUNK
    sel = _decay_selectors()
    sel = jnp.asarray(np.concatenate([sel, sel], axis=1), dtype=BF16)
    blk = pl.BlockSpec((1, rows, w), lambda b, i, safe_ref: (b, i, 0))
    return pl.pallas_call(
        _hgrn_kernel,
        grid_spec=pltpu.PrefetchScalarGridSpec(
            num_scalar_prefetch=1,
            grid=(bsz, seq // rows),
            in_specs=[blk, blk, blk, blk, blk, blk, _const_spec(sel.shape)],
            out_specs=blk,
            scratch_shapes=[pltpu.VMEM((HG_HEADS, HG_DIM, HG_DIM), F32)],
        ),
        out_shape=jax.ShapeDtypeStruct((bsz, seq, w), BF16),
        compiler_params=pltpu.CompilerParams(
            dimension_semantics=("arbitrary", "arbitrary"), vmem_limit_bytes=VMEM_LIMIT_BYTES),
        name="hgrn2",
    )(safe, q, k, lf_hi, lf_lo, v, g, sel)


def _attn_biases():
    blk, grp = ATT_BLOCK, ATT_GROUP
    slab = blk // grp
    a = np.arange(blk)
    unslab = grp * (a % slab) + a // slab
    full, own = [], []
    for pos in (a, unslab):
        kpos = np.concatenate([pos, pos + blk])
        dist = blk + pos[:, None] - kpos[None, :]
        keep = (dist >= 0) & (dist <= ATT_SPAN)
        full.append(np.where(np.concatenate([keep, keep], axis=0), 0.0, NEG_BIG))
        keep_own = keep[:, blk:]
        own.append(np.where(np.concatenate([keep_own, keep_own], axis=0), 0.0, NEG_BIG))
    return np.stack(full).astype(np.float32), np.stack(own).astype(np.float32)


def _attn_kernel(q_ref, k_ref, v_ref, full_ref, own_ref, wa_ref, wb_ref, wc_ref,
                 o_ref, wa_out, wb_out, wc_out,
                 qd_ref, kd_ref, vd_ref, acc_ref, m_ref, l_ref):
    seq = q_ref.shape[1]
    blk = ATT_BLOCK
    grp = ATT_GROUP
    quarter = seq // grp
    slab = blk // grp
    blocks = seq // blk
    per_res = quarter // blk

    @pl.when(pl.program_id(0) * pl.num_programs(1) + pl.program_id(1) < WCAST_PARTS)
    def _():
        for src, dst in ((wa_ref, wa_out), (wb_ref, wb_out), (wc_ref, wc_out)):
            dst[...] = src[...].astype(dst.dtype)

    def deinterleave(i):
        for src, dst in ((q_ref, qd_ref), (k_ref, kd_ref), (v_ref, vd_ref)):
            for c in range(grp):
                dst[pl.ds(c * quarter + i * blk, blk), :] = (
                    src[0, pl.ds(c + grp * i * blk, blk, stride=grp), :])

    def finish(c):
        for i in range(per_res):
            rows = pl.ds(c * quarter + i * blk, blk)
            o_ref[0, pl.ds(c + grp * i * blk, blk, stride=grp), :] = acc_ref[rows, :] / l_ref[rows, :]

    first_head = lax.broadcasted_iota(jnp.int32, (blk, LANES), 1) < ATT_HEAD_DIM
    stats = (m_ref, l_ref, acc_ref)

    def gather(ref, slices):
        parts = [ref[sl, :] for sl in slices]
        return parts[0] if len(parts) == 1 else jnp.concatenate(parts, axis=0)

    def scatter(ref, slices, val):
        n = val.shape[0] // len(slices)
        for i, sl in enumerate(slices):
            ref[sl, :] = val[i * n:(i + 1) * n]

    def process(items, merge, before=None, after=None):
        def keys(ref, cur, prev):
            own = gather(ref, cur)
            if prev is not None:
                own = jnp.concatenate([gather(ref, prev), own], axis=0)
            return own.astype(BF16)

        def scores(item):
            cur, prev, bias = item
            qf = gather(qd_ref, cur)
            q2 = jnp.concatenate([jnp.where(first_head, qf, 0.0), jnp.where(first_head, 0.0, qf)],
                                 axis=0).astype(BF16)
            return lax.dot_general(q2, keys(kd_ref, cur, prev), (((1,), (1,)), ((), ())),
                                   preferred_element_type=F32) + bias

        def softmax(s):
            m2 = jnp.max(s, axis=-1, keepdims=True)
            p = jnp.exp2(s - m2)
            return m2, jnp.sum(p, axis=-1, keepdims=True), p.astype(BF16)

        def values(item, state):
            cur, prev, _ = item
            m2, l2, p = state
            return m2, l2, jnp.dot(p, keys(vd_ref, cur, prev), preferred_element_type=F32)

        def commit(item, state):
            cur = item[0]
            new = tuple(jnp.where(first_head, x[:blk], x[blk:]) for x in state)
            if merge:
                m_old, l_old, o_old = (gather(ref, cur) for ref in stats)
                m_new, l_new, o_new = new
                m_tot = jnp.maximum(m_old, m_new)
                a_old = jnp.exp2(m_old - m_tot)
                a_new = jnp.exp2(m_new - m_tot)
                new = (m_tot, l_old * a_old + l_new * a_new, o_old * a_old + o_new * a_new)
            for ref, val in zip(stats, new):
                scatter(ref, cur, val)

        stages = (lambda it, st: scores(it), lambda it, st: softmax(st), values, commit)
        state = [None] * len(items)
        for t in range(len(items) + len(stages) - 1):
            for k in reversed(range(len(stages))):
                i = t - k
                if 0 <= i < len(items):
                    if k == 0 and before and i in before:
                        before[i]()
                    state[i] = stages[k](items[i], state[i])
                    if k == len(stages) - 1 and after and i in after:
                        after[i]()

    def rows4(c, n):
        return [pl.ds(c * quarter + n * blk, blk)]

    deinterleave(0)
    process([(rows4(c, n), None, own_ref[0]) if n == 0
             else (rows4(c, n), rows4(c, n - 1), full_ref[0])
             for n in range(per_res) for c in range(grp)], merge=False,
            before={grp * (n - 1): (lambda n=n: deinterleave(n)) for n in range(1, per_res)})

    def rows1(n):
        return [pl.ds(c * quarter + n * slab, slab) for c in range(grp)]

    process([(rows1(0), None, own_ref[1])]
            + [(rows1(n), rows1(n - 1), full_ref[1]) for n in range(1, blocks)], merge=True)

    sub_blocks = seq // (DILATIONS[-1] * blk)

    def rows16(c, e, n):
        return [pl.ds(c * quarter + e + n * grp * blk, blk, stride=grp)]

    items = []
    for c in range(grp):
        for e in range(grp):
            items.append((rows16(c, e, 0), None, own_ref[0]))
            for n in range(1, sub_blocks):
                items.append((rows16(c, e, n), rows16(c, e, n - 1), full_ref[0]))
    per_quarter = grp * sub_blocks
    process(items, merge=True,
            after={(c + 1) * per_quarter - 1: (lambda c=c: finish(c)) for c in range(grp)})


def _attn_call(aq, ak, av, tail_weights):
    bsz, seq, w = aq.shape
    assert LANES == 2 * ATT_HEAD_DIM, "one grid step handles the two heads of a lane slab"
    pairs = w // LANES
    assert bsz * pairs >= WCAST_PARTS
    full, own = (jnp.asarray(x) for x in _attn_biases())
    blk = pl.BlockSpec((1, seq, LANES), lambda b, hp: (b, 0, hp))
    part = lambda b, hp: (jnp.minimum(b * pairs + hp, WCAST_PARTS - 1), 0)
    w_specs = [pl.BlockSpec((wt.shape[0] // WCAST_PARTS, wt.shape[1]), part) for wt in tail_weights]
    return pl.pallas_call(
        _attn_kernel,
        grid=(bsz, pairs),
        in_specs=[blk, blk, blk, _const_spec(full.shape), _const_spec(own.shape)] + w_specs,
        out_specs=[blk] + w_specs,
        out_shape=[jax.ShapeDtypeStruct((bsz, seq, w), F32)]
        + [jax.ShapeDtypeStruct(wt.shape, BF16) for wt in tail_weights],
        scratch_shapes=[pltpu.VMEM((seq, LANES), F32) for _ in range(6)],
        compiler_params=pltpu.CompilerParams(
            dimension_semantics=("arbitrary", "arbitrary"), vmem_limit_bytes=VMEM_LIMIT_BYTES),
        name="dilated_attn",
    )(aq, ak, av, full, own, *tail_weights)


def _tail_kernel(x_ref, hg_ref, att_ref, g1_ref, sh2_ref, sc2_ref, g2_ref,
                 an_ref, n2_ref, fin_ref, wo_ref, wgu_ref, wd_ref, o_ref):
    d_ff = wd_ref.shape[0]
    att = att_ref[...]
    ms = jnp.mean(att * att, axis=-1, keepdims=True)
    att_n = (att * lax.rsqrt(ms + RMS_EPS) * an_ref[...]).astype(BF16)
    mix = jnp.dot(hg_ref[...], wo_ref[:HG_WIDTH, :], preferred_element_type=F32)
    mix = mix + jnp.dot(att_n, wo_ref[HG_WIDTH:, :], preferred_element_type=F32)
    x1 = x_ref[...] + g1_ref[0] * mix

    ms = jnp.mean(x1 * x1, axis=-1, keepdims=True)
    gain = n2_ref[...] * (1.0 + sc2_ref[0])
    hb = (x1 * lax.rsqrt(ms + RMS_EPS) * gain + sh2_ref[0]).astype(BF16)
    ffn = jnp.zeros_like(x1)
    for j in range(d_ff // FF_CHUNK):
        lo = j * FF_CHUNK
        a = jnp.dot(hb, wgu_ref[:, lo:lo + FF_CHUNK], preferred_element_type=F32)
        u = jnp.dot(hb, wgu_ref[:, d_ff + lo:d_ff + lo + FF_CHUNK], preferred_element_type=F32)
        act = (_silu(a) * u).astype(BF16)
        ffn = ffn + jnp.dot(act, wd_ref[lo:lo + FF_CHUNK, :], preferred_element_type=F32)
    x2 = x1 + g2_ref[0] * ffn
    ms = jnp.mean(x2 * x2, axis=-1, keepdims=True)
    o_ref[...] = x2 * lax.rsqrt(ms + RMS_EPS) * fin_ref[...]


def _tail_call(x2, hg, att, mod3, att_g, norm2_g, final_g, wo_bf, wgu_bf, wd_bf, seq):
    t, d = x2.shape
    tm = ROW_TILE
    steps_per_batch = seq // tm
    row = lambda i: (i, 0)
    mod_col = lambda col: pl.BlockSpec((1, 1, d), lambda i: (i // steps_per_batch, 0, col))
    return pl.pallas_call(
        _tail_kernel,
        grid=(t // tm,),
        in_specs=[
            pl.BlockSpec((tm, d), row),
            pl.BlockSpec((tm, HG_WIDTH), row),
            pl.BlockSpec((tm, ATT_WIDTH), row),
            mod_col(2), mod_col(3), mod_col(4), mod_col(5),
            _const_spec((1, ATT_WIDTH)), _const_spec((1, d)), _const_spec((1, d)),
            _const_spec(wo_bf.shape), _const_spec(wgu_bf.shape), _const_spec(wd_bf.shape),
        ],
        out_specs=pl.BlockSpec((tm, d), row),
        out_shape=jax.ShapeDtypeStruct((t, d), F32),
        compiler_params=pltpu.CompilerParams(
            dimension_semantics=("arbitrary",), vmem_limit_bytes=VMEM_LIMIT_BYTES),
        name="outproj_ffn",
    )(x2, hg, att, mod3, mod3, mod3, mod3,
      att_g.reshape(1, ATT_WIDTH), norm2_g.reshape(1, d), final_g.reshape(1, d),
      wo_bf, wgu_bf, wd_bf)


def kernel(x, c, w_ada, b_ada, norm1_g, w_in, hg_lb_logits, hg_onorm_g, att_onorm_g,
           w_out, norm2_g, w_gate_up, w_down, final_g):
    bsz, seq, d = x.shape
    assert w_in.shape[0] == 1 and hg_lb_logits.shape[0] == 2, "single-layer block expected"
    assert seq % (DILATIONS[-1] * ATT_BLOCK) == 0 and seq % ROW_TILE == 0
    t = bsz * seq
    x2 = x.reshape(t, d)

    mod = _mod_call(c, w_ada[0], b_ada[0])
    mod3 = mod.reshape(bsz, 1, 6 * d)

    q, k, lf_hi, lf_lo, v, g, aq, ak, av, half_decay = _inproj_call(
        x2, mod3, norm1_g[0], w_in[0], hg_lb_logits, hg_onorm_g[0], seq)
    safe = (jnp.min(half_decay[:, :, 0].reshape(-1, HG_GROUP * HG_CHUNK // HG_HALF), axis=1)
            >= HG_SAFE_LOG2_DECAY).astype(jnp.int32)

    as_seq = lambda a: a.reshape(bsz, seq, a.shape[-1])
    hg = _hgrn_call(as_seq(q), as_seq(k), as_seq(lf_hi), as_seq(lf_lo), as_seq(v), as_seq(g), safe)
    att, wo_bf, wgu_bf, wd_bf = _attn_call(
        as_seq(aq), as_seq(ak), as_seq(av), (w_out[0], w_gate_up[0], w_down[0]))

    out = _tail_call(
        x2, hg.reshape(t, HG_WIDTH), att.reshape(t, ATT_WIDTH), mod3,
        att_onorm_g[0], norm2_g[0], final_g, wo_bf, wgu_bf, wd_bf, seq)
    return out.reshape(bsz, seq, d)
```

```python
import numpy as np
import jax
import jax.numpy as jnp
from jax import lax
from jax.experimental import pallas as pl
from jax.experimental.pallas import tpu as pltpu

F32 = jnp.float32
BF16 = jnp.bfloat16

LANES = 128
RMS_EPS = 1e-6
LOG2_E = 1.4426950408889634
PROJ_WIDTH = 512

HG_HEADS = 4
HG_DIM = 128
HG_WIDTH = HG_HEADS * HG_DIM
HG_CHUNK = 128
HG_LEVELS = 7
HG_HALF = HG_CHUNK // 2
HG_SAFE_LOG2_DECAY = -85.0
HG_GROUP = 16
HG_STEP_HEADS = 2

ATT_HEAD_DIM = 64
ATT_WIDTH = 512
ATT_BLOCK = 128
DILATIONS = (1, 4, 16)
ATT_SPAN = 128
NEG_BIG = -1e30
ATT_GROUP = 4
WCAST_PARTS = 16

ROW_TILE = 512
FF_CHUNK = 256
WCAST_ROWS = 128
VMEM_LIMIT_BYTES = 56 * 1024 * 1024


def _silu(x):
    return x * jax.nn.sigmoid(x)


def _const_spec(shape):
    nd = len(shape)
    return pl.BlockSpec(shape, lambda *_: (0,) * nd, pipeline_mode=pl.Buffered(1))


def _mod_kernel(c_ref, w_ref, b_ref, o_ref):
    ca = _silu(c_ref[...])
    o_ref[...] = jnp.dot(ca, w_ref[...], preferred_element_type=F32) + b_ref[...]


def _mod_call(c, w_ada, b_ada):
    bsz, d = c.shape
    n = w_ada.shape[1]
    tn = d
    return pl.pallas_call(
        _mod_kernel,
        grid=(n // tn,),
        in_specs=[
            pl.BlockSpec((bsz, d), lambda j: (0, 0)),
            pl.BlockSpec((d, tn), lambda j: (0, j)),
            pl.BlockSpec((1, tn), lambda j: (0, j)),
        ],
        out_specs=pl.BlockSpec((bsz, tn), lambda j: (0, j)),
        out_shape=jax.ShapeDtypeStruct((bsz, n), F32),
        name="adaln_mod",
    )(c, w_ada, b_ada.reshape(1, n))


def _inproj_kernel(x_ref, sh_ref, sc_ref, g_ref, wf_ref, lbl_ref, on_ref,
                   q_o, k_o, lfh_o, lfl_o, v_o, g_o, aq_o, ak_o, av_o, dmin_o, w_ref):
    @pl.when(pl.program_id(0) == 0)
    def _():
        for r in range(0, wf_ref.shape[0], WCAST_ROWS):
            w_ref[r:r + WCAST_ROWS, :] = wf_ref[r:r + WCAST_ROWS, :].astype(BF16)

    x = x_ref[...]
    ms = jnp.mean(x * x, axis=-1, keepdims=True)
    gain = g_ref[...] * (1.0 + sc_ref[0])
    hb = (x * lax.rsqrt(ms + RMS_EPS) * gain + sh_ref[0]).astype(BF16)
    tm = hb.shape[0]

    def proj(j):
        return jnp.dot(hb, w_ref[:, j * PROJ_WIDTH:(j + 1) * PROJ_WIDTH],
                       preferred_element_type=F32)

    lbl = lbl_ref[...]
    e = jnp.exp(lbl - jnp.max(lbl, axis=0, keepdims=True))
    lb = e[0:1] / (e[0:1] + e[1:2])

    sg = jax.nn.sigmoid(proj(1))
    lf = jnp.log2(lb + (1.0 - lb) * sg)
    lf_hi = lf.astype(BF16)
    lfh_o[...] = lf_hi
    lfl_o[...] = (lf - lf_hi.astype(F32)).astype(BF16)
    k_o[...] = ((1.0 - lb) * (1.0 - sg)).astype(BF16)
    ak_o[...] = proj(5)
    q_o[...] = _silu(proj(0)).astype(BF16)
    av_o[...] = proj(6)
    g_o[...] = (_silu(proj(3)) * on_ref[...]).astype(BF16)
    grp_id = lax.broadcasted_iota(jnp.int32, (tm // HG_HALF, tm), 0)
    row_id = lax.broadcasted_iota(jnp.int32, (tm // HG_HALF, tm), 1)
    member = (row_id // HG_HALF == grp_id).astype(BF16)
    totals = jnp.dot(member, lf_hi, preferred_element_type=F32)
    dmin_o[0] = jnp.broadcast_to(jnp.min(totals, axis=-1, keepdims=True), dmin_o.shape[1:])
    v_o[...] = proj(2).astype(BF16)
    aq_o[...] = proj(4) * (ATT_HEAD_DIM ** -0.5 * LOG2_E)


def _inproj_call(x2, mod3, norm_g, w_in, lb_logits, hg_gain, seq):
    t, d = x2.shape
    tm = ROW_TILE
    steps_per_batch = seq // tm
    row = lambda i: (i, 0)
    out_dtypes = (BF16, BF16, BF16, BF16, BF16, BF16, F32, F32, F32)
    return pl.pallas_call(
        _inproj_kernel,
        grid=(t // tm,),
        in_specs=[
            pl.BlockSpec((tm, d), row),
            pl.BlockSpec((1, 1, d), lambda i: (i // steps_per_batch, 0, 0)),
            pl.BlockSpec((1, 1, d), lambda i: (i // steps_per_batch, 0, 1)),
            _const_spec((1, d)),
            _const_spec(w_in.shape),
            _const_spec(lb_logits.shape),
            _const_spec((1, HG_WIDTH)),
        ],
        out_specs=[pl.BlockSpec((tm, PROJ_WIDTH), row) for _ in out_dtypes]
        + [pl.BlockSpec((1, tm // HG_HALF, LANES), lambda i: (i, 0, 0))],
        out_shape=[jax.ShapeDtypeStruct((t, PROJ_WIDTH), dt) for dt in out_dtypes]
        + [jax.ShapeDtypeStruct((t // tm, tm // HG_HALF, LANES), F32)],
        scratch_shapes=[pltpu.VMEM(w_in.shape, BF16)],
        compiler_params=pltpu.CompilerParams(
            dimension_semantics=("arbitrary",), vmem_limit_bytes=VMEM_LIMIT_BYTES),
        name="in_proj",
    )(x2, mod3, mod3, norm_g.reshape(1, d), w_in, lb_logits,
      jnp.tile(hg_gain.reshape(1, HG_DIM), (1, HG_HEADS)))


def _decay_selectors():
    c = HG_CHUNK
    t = np.arange(c)[:, None]
    u = np.arange(c)[None, :]
    blocks = []
    for l in range(HG_LEVELS):
        lo = (t >> l) << l
        hi = lo + (1 << l) - 1
        second = ((t >> l) & 1) == 1
        blocks.append(np.where(second, (u >= lo) & (u <= t), (u > t) & (u <= hi)))
    blocks.append(u <= t)
    blocks.append(u > t)
    return np.concatenate(blocks, axis=0).astype(np.float32)


def _hgrn_kernel(safe_ref, q_ref, k_ref, lfh_ref, lfl_ref, v_ref, g_ref, p_ref, o_ref, st_ref):
    c = HG_CHUNK
    half = HG_HALF
    step = pl.program_id(0) * pl.num_programs(2) + pl.program_id(2)

    @pl.when(pl.program_id(2) == 0)
    def _():
        st_ref[...] = jnp.zeros_like(st_ref)

    def head(a, h):
        return a[:, h * HG_DIM:(h + 1) * HG_DIM]

    def gram(a, b):
        return lax.dot_general(a, b, (((1,), (1,)), ((), ())), preferred_element_type=F32)

    def log_decay(rows):
        return jnp.concatenate([lfh_ref[0, rows, :], lfl_ref[0, rows, :]], axis=0)

    def exponents(lf2, block):
        return jnp.dot(p_ref[block * c:(block + 1) * c, :], lf2, preferred_element_type=F32)

    def scaled(x_bf, log2_scale):
        return x_bf * jnp.exp2(log2_scale).astype(BF16)

    def normed(rows, o, h):
        ms = jnp.mean(o * o, axis=-1, keepdims=True)
        return (o * lax.rsqrt(ms + RMS_EPS)).astype(BF16) * head(g_ref[0, rows, :], h)

    safe = safe_ref[step] > 0

    @pl.when(safe)
    def _():
        causal = (lax.broadcasted_iota(jnp.int32, (c, c), 1)
                  <= lax.broadcasted_iota(jnp.int32, (c, c), 0))
        heads = range(HG_STEP_HEADS)

        def decay(j):
            rows = slice(j * c, (j + 1) * c)
            return rows, exponents(log_decay(rows), HG_LEVELS)

        def scale(st):
            rows, b = st
            b_mid = b[half - 1:half, :]
            q = q_ref[0, rows, :]
            k = k_ref[0, rows, :]
            return dict(rows=rows, b_mid=b_mid, b_last=b[c - 1:c, :], qe=scaled(q, b),
                        q_mid=scaled(q, b - b_mid), k_mid=scaled(k, b_mid - b))

        def score(st):
            st["s"] = [jnp.where(causal, gram(head(st["q_mid"], h), head(st["k_mid"], h)),
                                 0.0).astype(BF16) for h in heads]
            return st

        def apply(st):
            v = v_ref[0, st["rows"], :]
            st["intra"] = [jnp.dot(st["s"][h], head(v, h), preferred_element_type=F32)
                           for h in heads]
            tail_decay = jnp.exp2(st["b_last"] - st["b_mid"])
            st["kv"] = [lax.dot_general(head(v, h), head(st["k_mid"], h),
                                        (((0,), (0,)), ((), ())), preferred_element_type=F32)
                        * head(tail_decay, h) for h in heads]
            return st

        states = [st_ref[h] for h in heads]

        def recur(st):
            chunk_decay = jnp.exp2(st["b_last"])
            outs = []
            for h in heads:
                o = st["intra"][h] + gram(head(st["qe"], h), states[h].astype(BF16))
                states[h] = states[h] * head(chunk_decay, h) + st["kv"][h]
                outs.append(normed(st["rows"], o, h))
            o_ref[0, st["rows"], :] = jnp.concatenate(outs, axis=1).astype(o_ref.dtype)
            return None

        stages = (scale, lambda st: apply(score(st)), recur)
        live = [None] * HG_GROUP
        for t in range(HG_GROUP + len(stages)):
            for kk in reversed(range(len(stages) + 1)):
                j = t - kk
                if 0 <= j < HG_GROUP:
                    live[j] = decay(j) if kk == 0 else stages[kk - 1](live[j])
        for h in heads:
            st_ref[h] = states[h]


    @pl.when(jnp.logical_not(safe))
    def _():
        row = lax.broadcasted_iota(jnp.int32, (c, HG_STEP_HEADS * HG_DIM), 0)
        ti = lax.broadcasted_iota(jnp.int32, (c, c), 0)
        si = lax.broadcasted_iota(jnp.int32, (c, c), 1)
        txs = ti ^ si

        def one_chunk(j, carry):
            rows = pl.ds(pl.multiple_of(j * c, c), c)
            q = q_ref[0, rows, :]
            k = k_ref[0, rows, :]
            v = v_ref[0, rows, :]
            lf = log_decay(rows)
            scores = [jnp.where(ti == si, gram(head(q, h), head(k, h)), 0.0)
                      for h in range(HG_STEP_HEADS)]
            for l in range(HG_LEVELS):
                second = ((row >> l) & 1) == 1
                xl = scaled(jnp.where(second, q, k), exponents(lf, l))
                mask = (ti > si) & ((txs >> l) == 1)
                for h in range(HG_STEP_HEADS):
                    xh = head(xl, h)
                    scores[h] = jnp.where(mask, gram(xh, xh), scores[h])
            b = exponents(lf, HG_LEVELS)
            qe = scaled(q, b)
            k_end = scaled(k, exponents(lf, HG_LEVELS + 1))
            chunk_decay = jnp.exp2(b[c - 1:c, :])
            outs = []
            for h in range(HG_STEP_HEADS):
                st = st_ref[h]
                o = jnp.dot(scores[h].astype(BF16), head(v, h), preferred_element_type=F32)
                o = o + gram(head(qe, h), st.astype(BF16))
                st_ref[h] = st * head(chunk_decay, h) + lax.dot_general(
                    head(v, h), head(k_end, h), (((0,), (0,)), ((), ())),
                    preferred_element_type=F32)
                outs.append(normed(rows, o, h))
            o_ref[0, rows, :] = jnp.concatenate(outs, axis=1).astype(o_ref.dtype)
            return carry

        lax.fori_loop(0, HG_GROUP, one_chunk, 0)


def _hgrn_call(q, k, lf_hi, lf_lo, v, g, safe):
    bsz, seq, w = q.shape
    rows = HG_GROUP * HG_CHUNK
    sel = _decay_selectors()
    sel = jnp.asarray(np.concatenate([sel, sel], axis=1), dtype=BF16)
    blk = pl.BlockSpec((1, rows, HG_STEP_HEADS * HG_DIM), lambda b, hp, i, safe_ref: (b, i, hp))
    return pl.pallas_call(
        _hgrn_kernel,
        grid_spec=pltpu.PrefetchScalarGridSpec(
            num_scalar_prefetch=1,
            grid=(bsz, HG_HEADS // HG_STEP_HEADS, seq // rows),
            in_specs=[blk, blk, blk, blk, blk, blk, _const_spec(sel.shape)],
            out_specs=blk,
            scratch_shapes=[pltpu.VMEM((HG_STEP_HEADS, HG_DIM, HG_DIM), F32)],
        ),
        out_shape=jax.ShapeDtypeStruct((bsz, seq, w), BF16),
        compiler_params=pltpu.CompilerParams(
            dimension_semantics=("arbitrary", "arbitrary", "arbitrary"),
            vmem_limit_bytes=VMEM_LIMIT_BYTES),
        name="hgrn2",
    )(safe, q, k, lf_hi, lf_lo, v, g, sel)


def _attn_biases():
    blk, grp = ATT_BLOCK, ATT_GROUP
    slab = blk // grp
    a = np.arange(blk)
    unslab = grp * (a % slab) + a // slab
    full, own = [], []
    for pos in (a, unslab):
        kpos = np.concatenate([pos, pos + blk])
        dist = blk + pos[:, None] - kpos[None, :]
        keep = (dist >= 0) & (dist <= ATT_SPAN)
        full.append(np.where(np.concatenate([keep, keep], axis=0), 0.0, NEG_BIG))
        keep_own = keep[:, blk:]
        own.append(np.where(np.concatenate([keep_own, keep_own], axis=0), 0.0, NEG_BIG))
    return np.stack(full).astype(np.float32), np.stack(own).astype(np.float32)


def _attn_kernel(q_ref, k_ref, v_ref, full_ref, own_ref, wa_ref, wb_ref, wc_ref,
                 o_ref, wa_out, wb_out, wc_out,
                 qd_ref, kd_ref, vd_ref, acc_ref, m_ref, l_ref):
    seq = q_ref.shape[1]
    blk = ATT_BLOCK
    grp = ATT_GROUP
    quarter = seq // grp
    slab = blk // grp
    blocks = seq // blk
    per_res = quarter // blk

    @pl.when(pl.program_id(0) * pl.num_programs(1) + pl.program_id(1) < WCAST_PARTS)
    def _():
        for src, dst in ((wa_ref, wa_out), (wb_ref, wb_out), (wc_ref, wc_out)):
            dst[...] = src[...].astype(dst.dtype)

    def deinterleave(i):
        for src, dst in ((q_ref, qd_ref), (k_ref, kd_ref), (v_ref, vd_ref)):
            for c in range(grp):
                dst[pl.ds(c * quarter + i * blk, blk), :] = (
                    src[0, pl.ds(c + grp * i * blk, blk, stride=grp), :])

    def finish(c):
        for i in range(per_res):
            rows = pl.ds(c * quarter + i * blk, blk)
            o_ref[0, pl.ds(c + grp * i * blk, blk, stride=grp), :] = acc_ref[rows, :] / l_ref[rows, :]

    first_head = lax.broadcasted_iota(jnp.int32, (blk, LANES), 1) < ATT_HEAD_DIM
    stats = (m_ref, l_ref, acc_ref)

    def gather(ref, slices):
        parts = [ref[sl, :] for sl in slices]
        return parts[0] if len(parts) == 1 else jnp.concatenate(parts, axis=0)

    def scatter(ref, slices, val):
        n = val.shape[0] // len(slices)
        for i, sl in enumerate(slices):
            ref[sl, :] = val[i * n:(i + 1) * n]

    def process(items, merge, before=None, after=None):
        def keys(ref, cur, prev):
            own = gather(ref, cur)
            if prev is not None:
                own = jnp.concatenate([gather(ref, prev), own], axis=0)
            return own.astype(BF16)

        def scores(item):
            cur, prev, bias = item
            qf = gather(qd_ref, cur)
            q2 = jnp.concatenate([jnp.where(first_head, qf, 0.0), jnp.where(first_head, 0.0, qf)],
                                 axis=0).astype(BF16)
            return lax.dot_general(q2, keys(kd_ref, cur, prev), (((1,), (1,)), ((), ())),
                                   preferred_element_type=F32) + bias

        def softmax(s):
            m2 = jnp.max(s, axis=-1, keepdims=True)
            p = jnp.exp2(s - m2)
            return m2, jnp.sum(p, axis=-1, keepdims=True), p.astype(BF16)

        def values(item, state):
            cur, prev, _ = item
            m2, l2, p = state
            return m2, l2, jnp.dot(p, keys(vd_ref, cur, prev), preferred_element_type=F32)

        def commit(item, state):
            cur = item[0]
            new = tuple(jnp.where(first_head, x[:blk], x[blk:]) for x in state)
            if merge:
                m_old, l_old, o_old = (gather(ref, cur) for ref in stats)
                m_new, l_new, o_new = new
                m_tot = jnp.maximum(m_old, m_new)
                a_old = jnp.exp2(m_old - m_tot)
                a_new = jnp.exp2(m_new - m_tot)
                new = (m_tot, l_old * a_old + l_new * a_new, o_old * a_old + o_new * a_new)
            for ref, val in zip(stats, new):
                scatter(ref, cur, val)

        stages = (lambda it, st: scores(it), lambda it, st: softmax(st), values, commit)
        state = [None] * len(items)
        for t in range(len(items) + len(stages) - 1):
            for k in reversed(range(len(stages))):
                i = t - k
                if 0 <= i < len(items):
                    if k == 0 and before and i in before:
                        before[i]()
                    state[i] = stages[k](items[i], state[i])
                    if k == len(stages) - 1 and after and i in after:
                        after[i]()

    def rows4(c, n):
        return [pl.ds(c * quarter + n * blk, blk)]

    deinterleave(0)
    process([(rows4(c, n), None, own_ref[0]) if n == 0
             else (rows4(c, n), rows4(c, n - 1), full_ref[0])
             for n in range(per_res) for c in range(grp)], merge=False,
            before={grp * (n - 1): (lambda n=n: deinterleave(n)) for n in range(1, per_res)})

    def rows1(n):
        return [pl.ds(c * quarter + n * slab, slab) for c in range(grp)]

    process([(rows1(0), None, own_ref[1])]
            + [(rows1(n), rows1(n - 1), full_ref[1]) for n in range(1, blocks)], merge=True)

    sub_blocks = seq // (DILATIONS[-1] * blk)

    def rows16(c, e, n):
        return [pl.ds(c * quarter + e + n * grp * blk, blk, stride=grp)]

    items = []
    for c in range(grp):
        for e in range(grp):
            items.append((rows16(c, e, 0), None, own_ref[0]))
            for n in range(1, sub_blocks):
                items.append((rows16(c, e, n), rows16(c, e, n - 1), full_ref[0]))
    per_quarter = grp * sub_blocks
    process(items, merge=True,
            after={(c + 1) * per_quarter - 1: (lambda c=c: finish(c)) for c in range(grp)})


def _attn_call(aq, ak, av, tail_weights):
    bsz, seq, w = aq.shape
    assert LANES == 2 * ATT_HEAD_DIM, "one grid step handles the two heads of a lane slab"
    pairs = w // LANES
    assert bsz * pairs >= WCAST_PARTS
    full, own = (jnp.asarray(x) for x in _attn_biases())
    blk = pl.BlockSpec((1, seq, LANES), lambda b, hp: (b, 0, hp))
    part = lambda b, hp: (jnp.minimum(b * pairs + hp, WCAST_PARTS - 1), 0)
    w_specs = [pl.BlockSpec((wt.shape[0] // WCAST_PARTS, wt.shape[1]), part) for wt in tail_weights]
    return pl.pallas_call(
        _attn_kernel,
        grid=(bsz, pairs),
        in_specs=[blk, blk, blk, _const_spec(full.shape), _const_spec(own.shape)] + w_specs,
        out_specs=[blk] + w_specs,
        out_shape=[jax.ShapeDtypeStruct((bsz, seq, w), F32)]
        + [jax.ShapeDtypeStruct(wt.shape, BF16) for wt in tail_weights],
        scratch_shapes=[pltpu.VMEM((seq, LANES), F32) for _ in range(6)],
        compiler_params=pltpu.CompilerParams(
            dimension_semantics=("arbitrary", "arbitrary"), vmem_limit_bytes=VMEM_LIMIT_BYTES),
        name="dilated_attn",
    )(aq, ak, av, full, own, *tail_weights)


def _tail_kernel(x_ref, hg_ref, att_ref, g1_ref, sh2_ref, sc2_ref, g2_ref,
                 an_ref, n2_ref, fin_ref, wo_ref, wgu_ref, wd_ref, o_ref):
    d_ff = wd_ref.shape[0]
    att = att_ref[...]
    ms = jnp.mean(att * att, axis=-1, keepdims=True)
    att_n = (att * lax.rsqrt(ms + RMS_EPS) * an_ref[...]).astype(BF16)
    mix = jnp.dot(hg_ref[...], wo_ref[:HG_WIDTH, :], preferred_element_type=F32)
    mix = mix + jnp.dot(att_n, wo_ref[HG_WIDTH:, :], preferred_element_type=F32)
    x1 = x_ref[...] + g1_ref[0] * mix

    ms = jnp.mean(x1 * x1, axis=-1, keepdims=True)
    gain = n2_ref[...] * (1.0 + sc2_ref[0])
    hb = (x1 * lax.rsqrt(ms + RMS_EPS) * gain + sh2_ref[0]).astype(BF16)
    ffn = jnp.zeros_like(x1)
    for j in range(d_ff // FF_CHUNK):
        lo = j * FF_CHUNK
        a = jnp.dot(hb, wgu_ref[:, lo:lo + FF_CHUNK], preferred_element_type=F32)
        u = jnp.dot(hb, wgu_ref[:, d_ff + lo:d_ff + lo + FF_CHUNK], preferred_element_type=F32)
        act = (_silu(a) * u).astype(BF16)
        ffn = ffn + jnp.dot(act, wd_ref[lo:lo + FF_CHUNK, :], preferred_element_type=F32)
    x2 = x1 + g2_ref[0] * ffn
    ms = jnp.mean(x2 * x2, axis=-1, keepdims=True)
    o_ref[...] = x2 * lax.rsqrt(ms + RMS_EPS) * fin_ref[...]


def _tail_call(x2, hg, att, mod3, att_g, norm2_g, final_g, wo_bf, wgu_bf, wd_bf, seq):
    t, d = x2.shape
    tm = ROW_TILE
    steps_per_batch = seq // tm
    row = lambda i: (i, 0)
    mod_col = lambda col: pl.BlockSpec((1, 1, d), lambda i: (i // steps_per_batch, 0, col))
    return pl.pallas_call(
        _tail_kernel,
        grid=(t // tm,),
        in_specs=[
            pl.BlockSpec((tm, d), row),
            pl.BlockSpec((tm, HG_WIDTH), row),
            pl.BlockSpec((tm, ATT_WIDTH), row),
            mod_col(2), mod_col(3), mod_col(4), mod_col(5),
            _const_spec((1, ATT_WIDTH)), _const_spec((1, d)), _const_spec((1, d)),
            _const_spec(wo_bf.shape), _const_spec(wgu_bf.shape), _const_spec(wd_bf.shape),
        ],
        out_specs=pl.BlockSpec((tm, d), row),
        out_shape=jax.ShapeDtypeStruct((t, d), F32),
        compiler_params=pltpu.CompilerParams(
            dimension_semantics=("arbitrary",), vmem_limit_bytes=VMEM_LIMIT_BYTES),
        name="outproj_ffn",
    )(x2, hg, att, mod3, mod3, mod3, mod3,
      att_g.reshape(1, ATT_WIDTH), norm2_g.reshape(1, d), final_g.reshape(1, d),
      wo_bf, wgu_bf, wd_bf)


def kernel(x, c, w_ada, b_ada, norm1_g, w_in, hg_lb_logits, hg_onorm_g, att_onorm_g,
           w_out, norm2_g, w_gate_up, w_down, final_g):
    bsz, seq, d = x.shape
    assert w_in.shape[0] == 1 and hg_lb_logits.shape[0] == 2, "single-layer block expected"
    assert seq % (DILATIONS[-1] * ATT_BLOCK) == 0 and seq % ROW_TILE == 0
    t = bsz * seq
    x2 = x.reshape(t, d)

    mod = _mod_call(c, w_ada[0], b_ada[0])
    mod3 = mod.reshape(bsz, 1, 6 * d)

    q, k, lf_hi, lf_lo, v, g, aq, ak, av, half_decay = _inproj_call(
        x2, mod3, norm1_g[0], w_in[0], hg_lb_logits, hg_onorm_g[0], seq)
    safe = (jnp.min(half_decay[:, :, 0].reshape(-1, HG_GROUP * HG_CHUNK // HG_HALF), axis=1)
            >= HG_SAFE_LOG2_DECAY).astype(jnp.int32)

    as_seq = lambda a: a.reshape(bsz, seq, a.shape[-1])
    hg = _hgrn_call(as_seq(q), as_seq(k), as_seq(lf_hi), as_seq(lf_lo), as_seq(v), as_seq(g), safe)
    att, wo_bf, wgu_bf, wd_bf = _attn_call(
        as_seq(aq), as_seq(ak), as_seq(av), (w_out[0], w_gate_up[0], w_down[0]))

    out = _tail_call(
        x2, hg.reshape(t, HG_WIDTH), att.reshape(t, ATT_WIDTH), mod3,
        att_onorm_g[0], norm2_g[0], final_g, wo_bf, wgu_bf, wd_bf, seq)
    return out.reshape(bsz, seq, d)
```
